```python
import jax, jax.numpy as jnp
from jax import lax
import numpy as np

D_MODEL = 1024
BATCH = 2
SEQ = 16384
DEPTH = 2

MIX_WIDTH = 2 * D_MODEL
HEAD_DIM = 128
HGRN_WIDTH = MIX_WIDTH // 2
NSA_WIDTH = MIX_WIDTH - HGRN_WIDTH
HGRN_HEADS = HGRN_WIDTH // HEAD_DIM
NSA_HEADS = NSA_WIDTH // HEAD_DIM
NSA_KV_GROUPS = 2
NSA_HEADS_PER_GROUP = NSA_HEADS // NSA_KV_GROUPS
KV_WIDTH = NSA_KV_GROUPS * HEAD_DIM
N_BRANCHES = 3
HGRN_CHUNK = 64
CMP_BLOCK = 32
CMP_STRIDE = 16
SEL_BLOCK = 64
N_SELECT = 16
WINDOW = 512
Q_BLOCK = 128
ROPE_THETA = 500000.0
ROT_DIM = HEAD_DIM // 4
NORM_EPS = 1e-6
FORCE_SCORE = 1e4
IN_SPLITS = (HGRN_WIDTH, HGRN_WIDTH, HGRN_WIDTH, HGRN_WIDTH,
             NSA_WIDTH, KV_WIDTH, KV_WIDTH, KV_WIDTH, KV_WIDTH, KV_WIDTH, KV_WIDTH,
             NSA_HEADS * N_BRANCHES, NSA_WIDTH)
IN_COLS = sum(IN_SPLITS)

kernel_name = "hgrn2_nsa_parallel_hybrid"


def rmsnorm(x, g):
    xf = x.astype(jnp.float32)
    y = xf * lax.rsqrt(jnp.mean(xf * xf, axis=-1, keepdims=True) + NORM_EPS)
    return (y * g.astype(jnp.float32)).astype(x.dtype)


def partial_rope(x, pos):
    half = ROT_DIM // 2
    inv = ROPE_THETA ** (-2.0 * jnp.arange(half, dtype=jnp.float32) / ROT_DIM)
    ang = pos.astype(jnp.float32)[:, None] * inv[None, :]
    cos = jnp.cos(ang)[None, :, None, :]
    sin = jnp.sin(ang)[None, :, None, :]
    xf = x.astype(jnp.float32)
    x1 = xf[..., :half]
    x2 = xf[..., half:ROT_DIM]
    out = jnp.concatenate([x1 * cos - x2 * sin, x2 * cos + x1 * sin, xf[..., ROT_DIM:]], axis=-1)
    return out.astype(x.dtype)


def masked_softmax(s, mask, axis):
    s = jnp.where(mask, s, -jnp.inf)
    m = jnp.max(s, axis=axis, keepdims=True)
    m = jnp.where(jnp.isfinite(m), m, 0.0)
    p = jnp.exp(s - m)
    return p / jnp.maximum(jnp.sum(p, axis=axis, keepdims=True), 1e-30)


def hgrn2_mixer(q, f_logit, i, lb):
    B, S, H, D = q.shape
    C = HGRN_CHUNK
    N = S // C
    log_f = jnp.logaddexp(jnp.log(lb), jnp.log1p(-lb) + jax.nn.log_sigmoid(f_logit.astype(jnp.float32)))
    k = -jnp.expm1(log_f)

    def chunks(a):
        return a.astype(jnp.float32).reshape(B, N, C, H, D).transpose(1, 0, 3, 2, 4)

    qc, kc, vc = chunks(q), chunks(k), chunks(i)
    bc = jnp.cumsum(chunks(log_f), axis=3)
    causal = jnp.tril(jnp.ones((C, C), dtype=bool))

    def step(state, xs):
        q_, k_, v_, b_ = xs
        o_inter = jnp.einsum('bhtk,bhkv->bhtv', q_ * jnp.exp(b_), state)
        diff = jnp.where(causal[:, :, None], b_[:, :, :, None, :] - b_[:, :, None, :, :], -jnp.inf)
        attn = jnp.einsum('bhtk,bhsk,bhtsk->bhts', q_, k_, jnp.exp(diff))
        o_intra = jnp.einsum('bhts,bhsv->bhtv', attn, v_)
        b_last = b_[:, :, -1:, :]
        state = state * jnp.exp(b_last[:, :, 0, :, None]) + jnp.einsum(
            'bhsk,bhsv->bhkv', k_ * jnp.exp(b_last - b_), v_)
        return state, o_inter + o_intra

    s0 = jnp.zeros((B, H, D, D), jnp.float32)
    _, o = lax.scan(step, s0, (qc, kc, vc, bc))
    return o.transpose(1, 0, 3, 2, 4).reshape(B, S, H, D).astype(q.dtype)


def compress(kv, pos_emb, w1, b1, w2):
    B, S, G, D = kv.shape
    n_cmp = (S - CMP_BLOCK) // CMP_STRIDE + 1
    idx = jnp.arange(n_cmp)[:, None] * CMP_STRIDE + jnp.arange(CMP_BLOCK)[None, :]
    blocks = kv[:, idx] + pos_emb[None, None, :, None, :]
    flat = blocks.transpose(0, 1, 3, 2, 4).reshape(B, n_cmp, G, CMP_BLOCK * D)
    return jax.nn.silu(flat @ w1 + b1) @ w2


def nsa_mixer(q, kc, vc, ks, vs, kw, vw, gate_logits, cmp_k, cmp_v):
    B, S, H, D = q.shape
    G, HG = NSA_KV_GROUPS, NSA_HEADS_PER_GROUP
    pos = jnp.arange(S)
    q = partial_rope(q, pos)
    ks = partial_rope(ks, pos)
    kw = partial_rope(kw, pos)
    k_cmp = compress(kc, *cmp_k)
    v_cmp = compress(vc, *cmp_v).astype(jnp.float32)
    n_cmp = k_cmp.shape[1]
    cmp_start = jnp.arange(n_cmp) * CMP_STRIDE
    cmp_end = cmp_start + CMP_BLOCK - 1
    k_cmp = partial_rope(k_cmp, cmp_end).astype(jnp.float32)
    n_sel_blocks = S // SEL_BLOCK
    n_top = min(N_SELECT, n_sel_blocks)
    sel_start = jnp.arange(n_sel_blocks) * SEL_BLOCK
    overlap = ((cmp_start[:, None] < sel_start[None, :] + SEL_BLOCK)
               & (cmp_start[:, None] + CMP_BLOCK > sel_start[None, :])).astype(jnp.float32)
    ks_blocks = ks.reshape(B, n_sel_blocks, SEL_BLOCK, G, D).transpose(0, 3, 1, 2, 4)
    vs_blocks = vs.reshape(B, n_sel_blocks, SEL_BLOCK, G, D).transpose(0, 3, 1, 2, 4)
    kw_pad = jnp.pad(kw, ((0, 0), (WINDOW, 0), (0, 0), (0, 0)))
    vw_pad = jnp.pad(vw, ((0, 0), (WINDOW, 0), (0, 0), (0, 0)))
    gates = jax.nn.sigmoid(gate_logits.astype(jnp.float32)).reshape(B, S, G, HG, N_BRANCHES)
    scale = HEAD_DIM ** -0.5
    b_idx = jnp.arange(B)[:, None, None, None]
    g_idx = jnp.arange(G)[None, :, None, None]
    blk = jnp.arange(n_sel_blocks)

    def query_block(qb):
        q0 = qb * Q_BLOCK
        t = q0 + jnp.arange(Q_BLOCK)
        qblk = lax.dynamic_slice_in_dim(q, q0, Q_BLOCK, axis=1).reshape(
            B, Q_BLOCK, G, HG, D).astype(jnp.float32)
        s_c = jnp.einsum('btghd,bngd->bghtn', qblk, k_cmp) * scale
        p_c = masked_softmax(s_c, cmp_end[None, :] <= t[:, None], -1)
        o_c = jnp.einsum('bghtn,bngd->btghd', p_c, v_cmp)
        imp = jnp.einsum('bghtn,nj->bgtj', p_c, overlap)
        cur = t // SEL_BLOCK
        forced = (blk[None, :] == 0) | (blk[None, :] == cur[:, None]) | (blk[None, :] == cur[:, None] - 1)
        valid = blk[None, :] * SEL_BLOCK <= t[:, None]
        imp = jnp.where(forced, FORCE_SCORE, jnp.where(valid, imp, -1.0))
        _, top = lax.top_k(imp, n_top)
        k_s = ks_blocks[b_idx, g_idx, top].astype(jnp.float32)
        v_s = vs_blocks[b_idx, g_idx, top].astype(jnp.float32)
        key_pos = top[..., None] * SEL_BLOCK + jnp.arange(SEL_BLOCK)
        mask_s = (key_pos <= t[None, None, :, None, None])[:, :, None]
        s_s = jnp.einsum('btghd,bgtnld->bghtnl', qblk, k_s) * scale
        p_s = masked_softmax(s_s, mask_s, (-2, -1))
        o_s = jnp.einsum('bghtnl,bgtnld->btghd', p_s, v_s)
        k_w = lax.dynamic_slice_in_dim(kw_pad, q0, WINDOW + Q_BLOCK, axis=1).astype(jnp.float32)
        v_w = lax.dynamic_slice_in_dim(vw_pad, q0, WINDOW + Q_BLOCK, axis=1).astype(jnp.float32)
        kp = q0 - WINDOW + jnp.arange(WINDOW + Q_BLOCK)
        mask_w = (kp[None, :] >= 0) & (kp[None, :] <= t[:, None]) & (t[:, None] - kp[None, :] < WINDOW)
        s_w = jnp.einsum('btghd,bkgd->bghtk', qblk, k_w) * scale
        p_w = masked_softmax(s_w, mask_w, -1)
        o_w = jnp.einsum('bghtk,bkgd->btghd', p_w, v_w)
        g = lax.dynamic_slice_in_dim(gates, q0, Q_BLOCK, axis=1)
        o = g[..., 0:1] * o_c + g[..., 1:2] * o_s + g[..., 2:3] * o_w
        return o.reshape(B, Q_BLOCK, H, D)

    out = lax.map(query_block, jnp.arange(S // Q_BLOCK))
    return out.transpose(1, 0, 2, 3, 4).reshape(B, S, H, D).astype(q.dtype)


def setup_inputs(seed: int = 0) -> dict:
    key = jax.random.key(seed)
    k = jax.random.split(key, 16)

    def nrm(kk, shape, scale):
        return jax.random.normal(kk, shape, jnp.float32) * scale

    L, D = CMP_BLOCK, HEAD_DIM
    return {
        "x": nrm(k[0], (BATCH, SEQ, D_MODEL), 1.0),
        "pre_norm": 1.0 + nrm(k[1], (DEPTH, D_MODEL), 0.02),
        "post_norm": 1.0 + nrm(k[2], (DEPTH, D_MODEL), 0.02),
        "w_in": nrm(k[3], (DEPTH, D_MODEL, IN_COLS), D_MODEL ** -0.5),
        "hgrn_lb_logits": nrm(k[4], (DEPTH, HGRN_WIDTH), 0.5),
        "hgrn_out_norm": 1.0 + nrm(k[5], (DEPTH, HEAD_DIM), 0.02),
        "cmp_pos_k": nrm(k[6], (DEPTH, L, D), 0.1),
        "cmp_w1_k": nrm(k[7], (DEPTH, L * D, D), (L * D) ** -0.5),
        "cmp_b1_k": nrm(k[8], (DEPTH, D), 0.01),
        "cmp_w2_k": nrm(k[9], (DEPTH, D, D), D ** -0.5),
        "cmp_pos_v": nrm(k[10], (DEPTH, L, D), 0.1),
        "cmp_w1_v": nrm(k[11], (DEPTH, L * D, D), (L * D) ** -0.5),
        "cmp_b1_v": nrm(k[12], (DEPTH, D), 0.01),
        "cmp_w2_v": nrm(k[13], (DEPTH, D, D), D ** -0.5),
        "w_out": nrm(k[14], (DEPTH, MIX_WIDTH, D_MODEL), MIX_WIDTH ** -0.5),
    }


def reference(x, pre_norm, post_norm, w_in, hgrn_lb_logits, hgrn_out_norm,
              cmp_pos_k, cmp_w1_k, cmp_b1_k, cmp_w2_k,
              cmp_pos_v, cmp_w1_v, cmp_b1_v, cmp_w2_v, w_out):
    B, S, _ = x.shape
    lb_probs = jax.nn.softmax(hgrn_lb_logits.astype(jnp.float32), axis=0)
    lower_bounds = jnp.maximum(jnp.cumsum(lb_probs, axis=0) - lb_probs[0:1], 0.0)
    offsets = np.cumsum(IN_SPLITS)[:-1].tolist()

    def heads(a):
        return a.reshape(B, S, -1, HEAD_DIM)

    for layer in range(DEPTH):
        h = rmsnorm(x, pre_norm[layer])
        proj = h @ w_in[layer]
        (hq, hf, hi, hz, nq, kc, vc, ks, vs, kw, vw, ng, nz) = jnp.split(proj, offsets, axis=-1)
        o_h = hgrn2_mixer(heads(hq), heads(hf), heads(hi),
                          lower_bounds[layer].reshape(HGRN_HEADS, HEAD_DIM))
        o_h = rmsnorm(o_h, hgrn_out_norm[layer]).reshape(B, S, HGRN_WIDTH) * jax.nn.silu(hz)
        o_n = nsa_mixer(heads(nq), heads(kc), heads(vc), heads(ks), heads(vs), heads(kw), heads(vw), ng,
                        (cmp_pos_k[layer], cmp_w1_k[layer], cmp_b1_k[layer], cmp_w2_k[layer]),
                        (cmp_pos_v[layer], cmp_w1_v[layer], cmp_b1_v[layer], cmp_w2_v[layer]))
        o_n = o_n.reshape(B, S, NSA_WIDTH) * jax.nn.silu(nz)
        y = jnp.concatenate([o_h, o_n], axis=-1).astype(x.dtype) @ w_out[layer]
        x = x + rmsnorm(y, post_norm[layer]).astype(x.dtype)
    return x
```

```python
import functools

import numpy as np
import jax
import jax.numpy as jnp
from jax import lax
from jax.experimental import pallas as pl
from jax.experimental.pallas import tpu as pltpu

F32 = jnp.float32
BF16 = jnp.bfloat16

D_MODEL = 1024
HEAD_DIM = 128
HGRN_HEADS = 8
NSA_HEADS = 8
NSA_KV_GROUPS = 2
HEADS_PER_GROUP = NSA_HEADS // NSA_KV_GROUPS
N_BRANCHES = 3
HGRN_CHUNK = 64
HGRN_SUB = 16
CMP_BLOCK = 32
CMP_STRIDE = 16
SEL_BLOCK = 64
N_SELECT = 16
WINDOW = 512
ROPE_THETA = 500000.0
ROT_DIM = HEAD_DIM // 4
NORM_EPS = 1e-6
FORCE_SCORE = 1e4
NEG_BIG = -1e30

LANES = 128
BLK_HQ, BLK_HF, BLK_HI, BLK_HZ = 0, 8, 16, 24
BLK_NQ, BLK_KS, BLK_KW, BLK_KC, BLK_VC, BLK_VS, BLK_VW = 32, 40, 42, 44, 46, 48, 50
BLK_NZ, BLK_NG = 52, 60
PROJ_BLOCKS = 64
PROJ_COLS = PROJ_BLOCKS * LANES
_SRC = dict(hq=0, hf=1024, hi=2048, hz=3072, nq=4096, kc=5120, vc=5376, ks=5632, vs=5888,
            kw=6144, vw=6400, ng=6656, nz=6680)

PROJ_TM = 1024
PROJ_TN = 1024
ROPE_FULL_TILE = BLK_NQ * LANES // PROJ_TN
ROPE_HEAD_TILE = BLK_KS * LANES // PROJ_TN
ROPE_HEAD_BLOCKS = 4
HGRN_T = 1024
PREP_T = 1024
ATT_TQ = 256
SEL_TK = 1024
OUT_TM = 512
VMEM_LIMIT = 56 * 1024 * 1024


def _nt_dot(a, b):
    return lax.dot_general(a, b, (((1,), (1,)), ((), ())), preferred_element_type=F32)


def _sigmoid(x):
    return 1.0 / (1.0 + jnp.exp(-x))


def _proj_kernel(x_ref, g_ref, w_ref, c_ref, s1_ref, s2_ref, o_ref, h_ref):
    j = pl.program_id(1)

    @pl.when(j == 0)
    def _():
        x = x_ref[...]
        ms = jnp.mean(x * x, axis=-1, keepdims=True)
        h_ref[...] = (x * lax.rsqrt(ms + NORM_EPS) * g_ref[...]).astype(BF16)

    o_ref[...] = jnp.dot(h_ref[...], w_ref[...], preferred_element_type=F32)

    def rope_blocks(n_blocks):
        for blk in range(n_blocks):
            sl = slice(blk * LANES, (blk + 1) * LANES)
            a = o_ref[:, sl]
            o_ref[:, sl] = (a * c_ref[...] + pltpu.roll(a, LANES - ROT_DIM // 2, 1) * s1_ref[...]
                            + pltpu.roll(a, ROT_DIM // 2, 1) * s2_ref[...])

    @pl.when(j == ROPE_FULL_TILE)
    def _():
        rope_blocks(PROJ_TN // LANES)

    @pl.when(j == ROPE_HEAD_TILE)
    def _():
        rope_blocks(ROPE_HEAD_BLOCKS)


def _proj(x2, g, w, rope_c, rope_s1, rope_s2, seq):
    n = x2.shape[0]
    tm = min(PROJ_TM, seq)
    pos_tiles = seq // tm
    return pl.pallas_call(
        _proj_kernel,
        out_shape=jax.ShapeDtypeStruct((n, PROJ_COLS), F32),
        grid=(n // tm, PROJ_COLS // PROJ_TN),
        in_specs=[
            pl.BlockSpec((tm, D_MODEL), lambda i, j: (i, 0)),
            pl.BlockSpec((1, D_MODEL), lambda i, j: (0, 0)),
            pl.BlockSpec((D_MODEL, PROJ_TN), lambda i, j: (0, j)),
            pl.BlockSpec((tm, LANES), lambda i, j: (i % pos_tiles, 0)),
            pl.BlockSpec((tm, LANES), lambda i, j: (i % pos_tiles, 0)),
            pl.BlockSpec((tm, LANES), lambda i, j: (i % pos_tiles, 0)),
        ],
        out_specs=pl.BlockSpec((tm, PROJ_TN), lambda i, j: (i, j)),
        scratch_shapes=[pltpu.VMEM((tm, D_MODEL), BF16)],
        compiler_params=pltpu.CompilerParams(
            dimension_semantics=("arbitrary", "arbitrary"), vmem_limit_bytes=VMEM_LIMIT),
        name="proj",
    )(x2, g, w, rope_c, rope_s1, rope_s2)


def _hgrn_kernel(q_ref, f_ref, i_ref, z_ref, par_ref, tri_ref, ones_ref, y_ref,
                 st_ref, lf_ref, kk_ref, o_ref):
    @pl.when(pl.program_id(2) == 0)
    def _():
        st_ref[...] = jnp.zeros_like(st_ref)

    par = par_ref[0]
    log_lb, log_1m_lb, one_m_lb, g_norm = par[0:1], par[1:2], par[2:3], par[3:4]
    x = f_ref[...]
    log_sig = jnp.minimum(x, 0.0) - jnp.log1p(jnp.exp(-jnp.abs(x)))
    c = log_1m_lb + log_sig
    lf_ref[...] = jnp.maximum(log_lb, c) + jnp.log1p(jnp.exp(-jnp.abs(log_lb - c)))
    kk_ref[...] = one_m_lb / (1.0 + jnp.exp(x))

    n_sub = HGRN_CHUNK // HGRN_SUB
    sub_row = lax.broadcasted_iota(jnp.int32, (n_sub, HGRN_SUB, HEAD_DIM), 1)

    def chunk(ci, carry):
        c0 = pl.multiple_of(ci * HGRN_CHUNK, HGRN_CHUNK)
        rows = pl.ds(c0, HGRN_CHUNK)
        b = jnp.dot(tri_ref[...], lf_ref[rows, :], precision=lax.Precision.HIGHEST,
                    preferred_element_type=F32)
        q = q_ref[rows, :]
        k = kk_ref[rows, :]
        v = i_ref[rows, :]
        st = st_ref[...]
        v_bf = v.astype(BF16)
        o = _nt_dot((q * jnp.exp(b)).astype(BF16), st.astype(BF16))

        off = [jnp.zeros((HGRN_SUB, HEAD_DIM), F32)]
        for i in range(1, n_sub):
            lo = i * HGRN_SUB
            b_ref = b[lo - 1:lo, :]
            qi = (q[lo:lo + HGRN_SUB] * jnp.exp(b[lo:lo + HGRN_SUB] - b_ref)).astype(BF16)
            ki = (k[:lo] * jnp.exp(b_ref - b[:lo])).astype(BF16)
            a = _nt_dot(qi, ki)
            off.append(jnp.dot(a.astype(BF16), v_bf[:lo], preferred_element_type=F32))
        o = o + jnp.concatenate(off, axis=0)

        b4 = b.reshape(n_sub, HGRN_SUB, HEAD_DIM)
        q4 = q.reshape(n_sub, HGRN_SUB, HEAD_DIM)
        k4 = k.reshape(n_sub, HGRN_SUB, HEAD_DIM)
        v4 = v.reshape(n_sub, HGRN_SUB, HEAD_DIM)
        diag = jnp.zeros((n_sub, HGRN_SUB, HEAD_DIM), F32)
        for s in range(HGRN_SUB):
            d = jnp.where(sub_row >= s, b4 - b4[:, s:s + 1, :], NEG_BIG)
            p = jnp.exp(d) * q4 * k4[:, s:s + 1, :]
            r = jnp.dot(p.reshape(HGRN_CHUNK, HEAD_DIM).astype(BF16), ones_ref[...],
                        preferred_element_type=F32)
            diag = diag + r.reshape(n_sub, HGRN_SUB, HEAD_DIM) * v4[:, s:s + 1, :]
        o_ref[rows, :] = o + diag.reshape(HGRN_CHUNK, HEAD_DIM)

        b_last = b[HGRN_CHUNK - 1:HGRN_CHUNK, :]
        k_dec = (k * jnp.exp(b_last - b)).astype(BF16)
        upd = lax.dot_general(v_bf, k_dec, (((0,), (0,)), ((), ())), preferred_element_type=F32)
        st_ref[...] = st * jnp.exp(b_last) + upd
        return carry

    lax.fori_loop(0, q_ref.shape[0] // HGRN_CHUNK, chunk, 0)

    o = o_ref[...]
    ms = jnp.mean(o * o, axis=-1, keepdims=True)
    z = z_ref[...]
    y_ref[...] = (o * lax.rsqrt(ms + NORM_EPS) * g_norm * (z * _sigmoid(z))).astype(BF16)


def _hgrn(proj, par, tri, ones, batch, seq):
    t = min(HGRN_T, seq)
    nt = seq // t

    def col(base):
        return pl.BlockSpec((t, HEAD_DIM), lambda b, h, ti: (b * nt + ti, base + h))

    return pl.pallas_call(
        _hgrn_kernel,
        out_shape=jax.ShapeDtypeStruct((batch * seq, HGRN_HEADS * HEAD_DIM), BF16),
        grid=(batch, HGRN_HEADS, nt),
        in_specs=[
            col(BLK_HQ), col(BLK_HF), col(BLK_HI), col(BLK_HZ),
            pl.BlockSpec((1, 8, HEAD_DIM), lambda b, h, ti: (h, 0, 0)),
            pl.BlockSpec((HGRN_CHUNK, HGRN_CHUNK), lambda b, h, ti: (0, 0)),
            pl.BlockSpec((HEAD_DIM, HEAD_DIM), lambda b, h, ti: (0, 0)),
        ],
        out_specs=pl.BlockSpec((t, HEAD_DIM), lambda b, h, ti: (b * nt + ti, h)),
        scratch_shapes=[
            pltpu.VMEM((HEAD_DIM, HEAD_DIM), F32),
            pltpu.VMEM((t, HEAD_DIM), F32),
            pltpu.VMEM((t, HEAD_DIM), F32),
            pltpu.VMEM((t, HEAD_DIM), F32),
        ],
        compiler_params=pltpu.CompilerParams(
            dimension_semantics=("arbitrary", "arbitrary", "arbitrary"), vmem_limit_bytes=VMEM_LIMIT),
        name="hgrn",
    )(proj, proj, proj, proj, par, tri, ones)


def _compress_kernel(x_ref, w1_ref, b1_ref, w2_ref, pos_ref, tab_ref, o_ref):
    n_rows = o_ref.shape[-2]
    acc_lo = jnp.zeros((n_rows, HEAD_DIM), F32)
    acc_hi = jnp.zeros((n_rows, HEAD_DIM), F32)
    for r in range(CMP_STRIDE):
        xr = x_ref[pl.ds(r, n_rows, stride=CMP_STRIDE), :]
        lo, hi = r, CMP_STRIDE + r
        acc_lo = acc_lo + jnp.dot((xr + pos_ref[0, lo:lo + 1, :]).astype(BF16),
                                  w1_ref[0, lo * HEAD_DIM:(lo + 1) * HEAD_DIM, :],
                                  preferred_element_type=F32)
        acc_hi = acc_hi + jnp.dot((xr + pos_ref[0, hi:hi + 1, :]).astype(BF16),
                                  w1_ref[0, hi * HEAD_DIM:(hi + 1) * HEAD_DIM, :],
                                  preferred_element_type=F32)
    pre = acc_lo + pltpu.roll(acc_hi, n_rows - 1, 0) + b1_ref[0]
    mid = pre * _sigmoid(pre)
    out = jnp.dot(mid.astype(BF16), w2_ref[0], preferred_element_type=F32)
    o_ref[0, 0, 0] = (out * tab_ref[0, 0] + pltpu.roll(out, LANES - ROT_DIM // 2, 1) * tab_ref[0, 1]
                      + pltpu.roll(out, ROT_DIM // 2, 1) * tab_ref[0, 2])


def _compress(proj, w1, b1, w2, pos, tabs, batch, seq):
    n_rows = seq // CMP_STRIDE
    return pl.pallas_call(
        _compress_kernel,
        out_shape=jax.ShapeDtypeStruct((2, batch, NSA_KV_GROUPS, n_rows, HEAD_DIM), F32),
        grid=(2, batch, NSA_KV_GROUPS),
        in_specs=[
            pl.BlockSpec((seq, HEAD_DIM), lambda kv, b, g: (b, BLK_KC + 2 * kv + g)),
            pl.BlockSpec((1, CMP_BLOCK * HEAD_DIM, HEAD_DIM), lambda kv, b, g: (kv, 0, 0)),
            pl.BlockSpec((1, 1, HEAD_DIM), lambda kv, b, g: (kv, 0, 0)),
            pl.BlockSpec((1, HEAD_DIM, HEAD_DIM), lambda kv, b, g: (kv, 0, 0)),
            pl.BlockSpec((1, CMP_BLOCK, HEAD_DIM), lambda kv, b, g: (kv, 0, 0)),
            pl.BlockSpec((1, 3, n_rows, HEAD_DIM), lambda kv, b, g: (kv, 0, 0, 0)),
        ],
        out_specs=pl.BlockSpec((1, 1, 1, n_rows, HEAD_DIM), lambda kv, b, g: (kv, b, g, 0, 0)),
        compiler_params=pltpu.CompilerParams(
            dimension_semantics=("arbitrary", "arbitrary", "arbitrary"), vmem_limit_bytes=VMEM_LIMIT),
        name="compress",
    )(proj, w1, b1, w2, pos, tabs)


def _kvprep_kernel(ks_ref, kw_ref, vs_ref, vw_ref, kaug_ref, vaug_ref, kwb_ref, vwb_ref):
    t = ks_ref.shape[0]
    t0 = pl.program_id(2) * t
    row = lax.broadcasted_iota(jnp.int32, (t, LANES), 0) + t0
    col = lax.broadcasted_iota(jnp.int32, (t, LANES), 1)
    block_lane = lax.shift_right_logical(row, 6) & (LANES - 1)
    kaug_ref[0, 0, :, :HEAD_DIM] = ks_ref[...].astype(BF16)
    kaug_ref[0, 0, :, HEAD_DIM:] = jnp.where(block_lane == col, 1.0, 0.0).astype(BF16)
    vaug_ref[0, 0, :, :HEAD_DIM] = vs_ref[...].astype(BF16)
    vaug_ref[0, 0, :, HEAD_DIM:] = jnp.ones((t, LANES), BF16)
    kwb_ref[0, 0] = kw_ref[...].astype(BF16)
    vwb_ref[0, 0] = vw_ref[...].astype(BF16)


def _kvprep(proj, batch, seq):
    t = min(PREP_T, seq)
    nt = seq // t

    def col(base):
        return pl.BlockSpec((t, HEAD_DIM), lambda b, g, ti: (b * nt + ti, base + g))

    def out(width):
        return pl.BlockSpec((1, 1, t, width), lambda b, g, ti: (b, g, ti, 0))

    def shape(width):
        return jax.ShapeDtypeStruct((batch, NSA_KV_GROUPS, seq, width), BF16)

    return pl.pallas_call(
        _kvprep_kernel,
        out_shape=(shape(2 * HEAD_DIM), shape(2 * HEAD_DIM), shape(HEAD_DIM), shape(HEAD_DIM)),
        grid=(batch, NSA_KV_GROUPS, nt),
        in_specs=[col(BLK_KS), col(BLK_KW), col(BLK_VS), col(BLK_VW)],
        out_specs=(out(2 * HEAD_DIM), out(2 * HEAD_DIM), out(HEAD_DIM), out(HEAD_DIM)),
        compiler_params=pltpu.CompilerParams(
            dimension_semantics=("arbitrary", "arbitrary", "arbitrary"), vmem_limit_bytes=VMEM_LIMIT),
        name="kvprep",
    )(proj, proj, proj, proj)


def _cmpwin_kernel(q_ref, kc_ref, vc_ref, kw_ref, vw_ref, ng_ref, ov_ref, part_ref, selb_ref,
                   val_ref, sel_ref, *, n_blocks, win_len):
    tq = q_ref.shape[0]
    n_cmp = kc_ref.shape[-2]
    nblkp = selb_ref.shape[-1]
    q0 = pl.program_id(2) * tq
    scale = HEAD_DIM ** -0.5

    kc = kc_ref[0, 0, 0].astype(BF16)
    vc = vc_ref[0, 0, 0].astype(BF16)
    t_c = lax.broadcasted_iota(jnp.int32, (tq, n_cmp), 0) + q0
    n_c = lax.broadcasted_iota(jnp.int32, (tq, n_cmp), 1)
    vis_c = n_c * CMP_STRIDE + (CMP_BLOCK - 1) <= t_c

    w0 = pl.multiple_of(jnp.maximum(q0 + tq - win_len, 0), tq)
    kwin = kw_ref[0, 0, pl.ds(w0, win_len), :]
    vwin = vw_ref[0, 0, pl.ds(w0, win_len), :]
    t_w = lax.broadcasted_iota(jnp.int32, (tq, win_len), 0) + q0
    kp_w = lax.broadcasted_iota(jnp.int32, (tq, win_len), 1) + w0
    vis_w = (kp_w <= t_w) & (t_w - kp_w < WINDOW)

    gates = _sigmoid(ng_ref[...])
    p_sum = jnp.zeros((tq, n_cmp), F32)
    for h in range(HEADS_PER_GROUP):
        sl = slice(h * HEAD_DIM, (h + 1) * HEAD_DIM)
        qh = (q_ref[:, sl] * scale).astype(BF16)
        s = jnp.where(vis_c, _nt_dot(qh, kc), NEG_BIG)
        m = jnp.max(s, axis=-1, keepdims=True)
        m = jnp.where(m < 0.5 * NEG_BIG, 0.0, m)
        p = jnp.exp(s - m)
        p = p / jnp.maximum(jnp.sum(p, axis=-1, keepdims=True), 1e-30)
        o_c = jnp.dot(p.astype(BF16), vc, preferred_element_type=F32)
        p_sum = p_sum + p
        s = jnp.where(vis_w, _nt_dot(qh, kwin), NEG_BIG)
        m = jnp.max(s, axis=-1, keepdims=True)
        p = jnp.exp(s - m)
        p = p / jnp.maximum(jnp.sum(p, axis=-1, keepdims=True), 1e-30)
        o_w = jnp.dot(p.astype(BF16), vwin, preferred_element_type=F32)
        g_c = gates[:, 0 * HEADS_PER_GROUP + h:0 * HEADS_PER_GROUP + h + 1]
        g_w = gates[:, 2 * HEADS_PER_GROUP + h:2 * HEADS_PER_GROUP + h + 1]
        part_ref[:, sl] = g_c * o_c + g_w * o_w

    p_hi = p_sum.astype(BF16)
    p_lo = (p_sum - p_hi.astype(F32)).astype(BF16)
    imp = (jnp.dot(p_hi, ov_ref[...], preferred_element_type=F32)
           + jnp.dot(p_lo, ov_ref[...], preferred_element_type=F32))
    t_b = lax.broadcasted_iota(jnp.int32, (tq, nblkp), 0) + q0
    blk = lax.broadcasted_iota(jnp.int32, (tq, nblkp), 1)
    cur = lax.shift_right_logical(t_b, 6)
    forced = (blk == 0) | (blk == cur) | (blk == cur - 1)
    val = jnp.where(forced, FORCE_SCORE, jnp.where(blk * SEL_BLOCK <= t_b, imp, -1.0))
    val_ref[...] = jnp.where(blk < n_blocks, val, -3.0)
    sel_ref[...] = jnp.zeros_like(sel_ref)
    blk_f = blk.astype(F32)

    def pick(_, carry):
        v = val_ref[...]
        m = jnp.max(v, axis=-1, keepdims=True)
        idx = jnp.min(jnp.where(v == m, blk_f, float(nblkp)), axis=-1, keepdims=True)
        hit = blk_f == idx
        sel_ref[...] = jnp.where(hit, 1.0, sel_ref[...])
        val_ref[...] = jnp.where(hit, -4.0, v)
        return carry

    lax.fori_loop(0, min(N_SELECT, n_blocks), pick, 0)
    selb_ref[0, 0] = jnp.where(sel_ref[...] > 0.5, 0.0, NEG_BIG).astype(BF16)


def _cmpwin(proj, cmp_kv, kwb, vwb, ov, batch, seq):
    tq = min(ATT_TQ, seq)
    nq = seq // tq
    n_cmp = seq // CMP_STRIDE
    n_blocks = seq // SEL_BLOCK
    nblkp = ov.shape[1]
    win_len = min(WINDOW + tq, seq)
    q_cols = HEADS_PER_GROUP * HEAD_DIM
    kernel = functools.partial(_cmpwin_kernel, n_blocks=n_blocks, win_len=win_len)
    return pl.pallas_call(
        kernel,
        out_shape=(jax.ShapeDtypeStruct((batch * seq, NSA_HEADS * HEAD_DIM), F32),
                   jax.ShapeDtypeStruct((batch, NSA_KV_GROUPS, seq, nblkp), BF16)),
        grid=(batch, NSA_KV_GROUPS, nq),
        in_specs=[
            pl.BlockSpec((tq, q_cols), lambda b, g, i: (b * nq + i, BLK_NQ * LANES // q_cols + g)),
            pl.BlockSpec((1, 1, 1, n_cmp, HEAD_DIM), lambda b, g, i: (0, b, g, 0, 0)),
            pl.BlockSpec((1, 1, 1, n_cmp, HEAD_DIM), lambda b, g, i: (1, b, g, 0, 0)),
            pl.BlockSpec((1, 1, seq, HEAD_DIM), lambda b, g, i: (b, g, 0, 0)),
            pl.BlockSpec((1, 1, seq, HEAD_DIM), lambda b, g, i: (b, g, 0, 0)),
            pl.BlockSpec((tq, LANES), lambda b, g, i: (b * nq + i, BLK_NG + g)),
            pl.BlockSpec((n_cmp, nblkp), lambda b, g, i: (0, 0)),
        ],
        out_specs=(pl.BlockSpec((tq, q_cols), lambda b, g, i: (b * nq + i, g)),
                   pl.BlockSpec((1, 1, tq, nblkp), lambda b, g, i: (b, g, i, 0))),
        scratch_shapes=[pltpu.VMEM((tq, nblkp), F32), pltpu.VMEM((tq, nblkp), F32)],
        compiler_params=pltpu.CompilerParams(
            dimension_semantics=("arbitrary", "arbitrary", "arbitrary"), vmem_limit_bytes=VMEM_LIMIT),
        name="cmpwin",
    )(proj, cmp_kv, cmp_kv, kwb, vwb, proj, ov)


def _select_kernel(q_ref, selb_ref, kaug_ref, vaug_ref, part_ref, ng_ref, nz_ref, y_ref,
                   qaug_ref, acc_ref, m_ref, *, tk):
    tq = q_ref.shape[0]
    n_halves = selb_ref.shape[-1] // LANES
    q0 = pl.program_id(2) * tq
    scale = HEAD_DIM ** -0.5

    for h in range(HEADS_PER_GROUP):
        qh = (q_ref[:, h * HEAD_DIM:(h + 1) * HEAD_DIM] * scale).astype(BF16)
        for half in range(n_halves):
            qaug_ref[half * HEADS_PER_GROUP + h, :, :HEAD_DIM] = qh
            qaug_ref[half * HEADS_PER_GROUP + h, :, HEAD_DIM:] = selb_ref[0, 0, :, half * LANES:(half + 1) * LANES]
    acc_ref[...] = jnp.zeros_like(acc_ref)
    m_ref[...] = jnp.full_like(m_ref, NEG_BIG)

    def tile(kt, causal):
        k0 = pl.multiple_of(kt * tk, tk)
        k_t = kaug_ref[0, 0, pl.ds(k0, tk), :]
        v_t = vaug_ref[0, 0, pl.ds(k0, tk), :]
        half = k0 // (SEL_BLOCK * LANES)
        if causal:
            t_pos = lax.broadcasted_iota(jnp.int32, (tq, tk), 0) + q0
            k_pos = lax.broadcasted_iota(jnp.int32, (tq, tk), 1) + k0
            vis = k_pos <= t_pos
        for h in range(HEADS_PER_GROUP):
            s = _nt_dot(qaug_ref[half * HEADS_PER_GROUP + h], k_t)
            if causal:
                s = jnp.where(vis, s, NEG_BIG)
            m_prev = m_ref[h]
            m_new = jnp.maximum(m_prev, jnp.max(s, axis=-1, keepdims=True))
            p = jnp.exp(s - pltpu.repeat(m_new, tk // LANES, 1))
            alpha = jnp.exp(m_prev - m_new)
            acc_ref[h] = (acc_ref[h] * pltpu.repeat(alpha, 2, 1)
                          + jnp.dot(p.astype(BF16), v_t, preferred_element_type=F32))
            m_ref[h] = m_new

    n_full = q0 // tk

    def full_tile(kt, carry):
        tile(kt, False)
        return carry

    lax.fori_loop(0, n_full, full_tile, 0)
    tile(n_full, True)

    gates = _sigmoid(ng_ref[...])
    for h in range(HEADS_PER_GROUP):
        sl = slice(h * HEAD_DIM, (h + 1) * HEAD_DIM)
        acc = acc_ref[h]
        o_s = acc[:, :HEAD_DIM] / jnp.maximum(acc[:, HEAD_DIM:], 1e-30)
        g_s = gates[:, 1 * HEADS_PER_GROUP + h:1 * HEADS_PER_GROUP + h + 1]
        z = nz_ref[:, sl]
        y_ref[:, sl] = ((part_ref[:, sl] + g_s * o_s) * (z * _sigmoid(z))).astype(BF16)


def _select(proj, selb, kaug, vaug, part, batch, seq):
    tq = min(ATT_TQ, seq)
    tk = min(SEL_TK, seq)
    nq = seq // tq
    nblkp = selb.shape[-1]
    q_cols = HEADS_PER_GROUP * HEAD_DIM
    kernel = functools.partial(_select_kernel, tk=tk)
    return pl.pallas_call(
        kernel,
        out_shape=jax.ShapeDtypeStruct((batch * seq, NSA_HEADS * HEAD_DIM), BF16),
        grid=(batch, NSA_KV_GROUPS, nq),
        in_specs=[
            pl.BlockSpec((tq, q_cols), lambda b, g, i: (b * nq + i, BLK_NQ * LANES // q_cols + g)),
            pl.BlockSpec((1, 1, tq, nblkp), lambda b, g, i: (b, g, i, 0)),
            pl.BlockSpec((1, 1, seq, 2 * HEAD_DIM), lambda b, g, i: (b, g, 0, 0)),
            pl.BlockSpec((1, 1, seq, 2 * HEAD_DIM), lambda b, g, i: (b, g, 0, 0)),
            pl.BlockSpec((tq, q_cols), lambda b, g, i: (b * nq + i, g)),
            pl.BlockSpec((tq, LANES), lambda b, g, i: (b * nq + i, BLK_NG + g)),
            pl.BlockSpec((tq, q_cols), lambda b, g, i: (b * nq + i, BLK_NZ * LANES // q_cols + g)),
        ],
        out_specs=pl.BlockSpec((tq, q_cols), lambda b, g, i: (b * nq + i, g)),
        scratch_shapes=[
            pltpu.VMEM((nblkp // LANES * HEADS_PER_GROUP, tq, 2 * HEAD_DIM), BF16),
            pltpu.VMEM((HEADS_PER_GROUP, tq, 2 * HEAD_DIM), F32),
            pltpu.VMEM((HEADS_PER_GROUP, tq, LANES), F32),
        ],
        compiler_params=pltpu.CompilerParams(
            dimension_semantics=("arbitrary", "arbitrary", "arbitrary"), vmem_limit_bytes=VMEM_LIMIT),
        name="select",
    )(proj, selb, kaug, vaug, part, proj, proj)


def _outproj_kernel(yh_ref, yn_ref, w_ref, x_ref, g_ref, o_ref):
    half = yh_ref.shape[1]
    z = (jnp.dot(yh_ref[...], w_ref[:half, :], preferred_element_type=F32)
         + jnp.dot(yn_ref[...], w_ref[half:, :], preferred_element_type=F32))
    ms = jnp.mean(z * z, axis=-1, keepdims=True)
    o_ref[...] = x_ref[...] + z * lax.rsqrt(ms + NORM_EPS) * g_ref[...]


def _outproj(yh, yn, w, x2, g):
    n = x2.shape[0]
    tm = min(OUT_TM, n)
    return pl.pallas_call(
        _outproj_kernel,
        out_shape=jax.ShapeDtypeStruct((n, D_MODEL), F32),
        grid=(n // tm,),
        in_specs=[
            pl.BlockSpec((tm, yh.shape[1]), lambda i: (i, 0)),
            pl.BlockSpec((tm, yn.shape[1]), lambda i: (i, 0)),
            pl.BlockSpec(w.shape, lambda i: (0, 0)),
            pl.BlockSpec((tm, D_MODEL), lambda i: (i, 0)),
            pl.BlockSpec((1, D_MODEL), lambda i: (0, 0)),
        ],
        out_specs=pl.BlockSpec((tm, D_MODEL), lambda i: (i, 0)),
        compiler_params=pltpu.CompilerParams(
            dimension_semantics=("arbitrary",), vmem_limit_bytes=VMEM_LIMIT),
        name="outproj",
    )(yh, yn, w, x2, g)


def _proj_column_map():
    idx = np.full((PROJ_COLS,), -1, np.int64)

    def put(dst_blk, src, width):
        idx[dst_blk * LANES:dst_blk * LANES + width] = np.arange(src, src + width)

    for name, blk, width in (("hq", BLK_HQ, 1024), ("hf", BLK_HF, 1024), ("hi", BLK_HI, 1024),
                             ("hz", BLK_HZ, 1024), ("nq", BLK_NQ, 1024), ("ks", BLK_KS, 256),
                             ("kw", BLK_KW, 256), ("kc", BLK_KC, 256), ("vc", BLK_VC, 256),
                             ("vs", BLK_VS, 256), ("vw", BLK_VW, 256), ("nz", BLK_NZ, 1024)):
        put(blk, _SRC[name], width)
    for g in range(NSA_KV_GROUPS):
        for br in range(N_BRANCHES):
            for h in range(HEADS_PER_GROUP):
                idx[(BLK_NG + g) * LANES + br * HEADS_PER_GROUP + h] = (
                    _SRC["ng"] + (g * HEADS_PER_GROUP + h) * N_BRANCHES + br)
    return idx


def _rope_tables(pos):
    half = ROT_DIM // 2
    inv = ROPE_THETA ** (-2.0 * jnp.arange(half, dtype=F32) / ROT_DIM)
    ang = pos.astype(F32)[:, None] * inv[None, :]
    cos, sin = jnp.cos(ang), jnp.sin(ang)
    n = pos.shape[0]
    pad = jnp.zeros((n, HEAD_DIM - ROT_DIM), F32)
    zero = jnp.zeros((n, half), F32)
    c = jnp.concatenate([cos, cos, pad + 1.0], axis=1)
    s1 = jnp.concatenate([-sin, zero, pad], axis=1)
    s2 = jnp.concatenate([zero, sin, pad], axis=1)
    return c, s1, s2


def _overlap_matrix(n_cmp, n_blocks, nblkp):
    start = np.arange(n_cmp)[:, None] * CMP_STRIDE
    sel = np.arange(nblkp)[None, :] * SEL_BLOCK
    ov = (start < sel + SEL_BLOCK) & (start + CMP_BLOCK > sel)
    ov &= np.arange(nblkp)[None, :] < n_blocks
    ov &= np.arange(n_cmp)[:, None] < (n_cmp - 1)
    return jnp.asarray(ov, BF16)


def kernel(x, pre_norm, post_norm, w_in, hgrn_lb_logits, hgrn_out_norm, cmp_pos_k, cmp_w1_k, cmp_b1_k,
           cmp_w2_k, cmp_pos_v, cmp_w1_v, cmp_b1_v, cmp_w2_v, w_out):
    batch, seq, _ = x.shape
    depth = w_in.shape[0]
    n_cmp = seq // CMP_STRIDE
    n_blocks = seq // SEL_BLOCK
    nblkp = -(-n_blocks // LANES) * LANES

    lb_probs = jax.nn.softmax(hgrn_lb_logits.astype(F32), axis=0)
    lower = jnp.maximum(jnp.cumsum(lb_probs, axis=0) - lb_probs[0:1], 0.0)
    lower = lower.reshape(depth, HGRN_HEADS, 1, HEAD_DIM)
    g_out = jnp.broadcast_to(hgrn_out_norm.astype(F32)[:, None, None, :], lower.shape)
    hgrn_par = jnp.concatenate(
        [jnp.log(lower), jnp.log1p(-lower), 1.0 - lower, g_out, jnp.zeros((depth, HGRN_HEADS, 4, HEAD_DIM), F32)],
        axis=2)

    col_map = _proj_column_map()
    col_src = jnp.asarray(np.maximum(col_map, 0), jnp.int32)
    col_ok = jnp.asarray(col_map >= 0)

    rope_c, rope_s1, rope_s2 = _rope_tables(jnp.arange(seq))
    cmp_tab_k = jnp.stack(_rope_tables(jnp.arange(n_cmp) * CMP_STRIDE + CMP_BLOCK - 1))
    cmp_tab_v = jnp.stack([jnp.ones((n_cmp, HEAD_DIM), F32), jnp.zeros((n_cmp, HEAD_DIM), F32),
                           jnp.zeros((n_cmp, HEAD_DIM), F32)])
    cmp_tabs = jnp.stack([cmp_tab_k, cmp_tab_v])
    ov = _overlap_matrix(n_cmp, n_blocks, nblkp)
    tri = jnp.asarray(np.tril(np.ones((HGRN_CHUNK, HGRN_CHUNK), np.float32)))
    ones = jnp.ones((HEAD_DIM, HEAD_DIM), BF16)

    x2 = x.reshape(batch * seq, D_MODEL)
    for layer in range(depth):
        w_perm = jnp.where(col_ok[None, :], jnp.take(w_in[layer], col_src, axis=1), 0.0).astype(BF16)
        proj = _proj(x2, pre_norm[layer][None, :], w_perm, rope_c, rope_s1, rope_s2, seq)
        y_h = _hgrn(proj, hgrn_par[layer], tri, ones, batch, seq)
        cmp_kv = _compress(
            proj,
            jnp.stack([cmp_w1_k[layer], cmp_w1_v[layer]]).astype(BF16),
            jnp.stack([cmp_b1_k[layer], cmp_b1_v[layer]])[:, None, :],
            jnp.stack([cmp_w2_k[layer], cmp_w2_v[layer]]).astype(BF16),
            jnp.stack([cmp_pos_k[layer], cmp_pos_v[layer]]),
            cmp_tabs, batch, seq)
        kaug, vaug, kwb, vwb = _kvprep(proj, batch, seq)
        part, selb = _cmpwin(proj, cmp_kv, kwb, vwb, ov, batch, seq)
        y_n = _select(proj, selb, kaug, vaug, part, batch, seq)
        x2 = _outproj(y_h, y_n, w_out[layer].astype(BF16), x2, post_norm[layer][None, :])
    return x2.reshape(batch, seq, D_MODEL)
```

```python
import functools

import numpy as np
import jax
import jax.numpy as jnp
from jax import lax
from jax.experimental import pallas as pl
from jax.experimental.pallas import tpu as pltpu

F32 = jnp.float32
BF16 = jnp.bfloat16

D_MODEL = 1024
HEAD_DIM = 128
HGRN_HEADS = 8
NSA_HEADS = 8
NSA_KV_GROUPS = 2
HEADS_PER_GROUP = NSA_HEADS // NSA_KV_GROUPS
N_BRANCHES = 3
HGRN_CHUNK = 64
HGRN_SUB = 16
CMP_BLOCK = 32
CMP_STRIDE = 16
SEL_BLOCK = 64
N_SELECT = 16
WINDOW = 512
ROPE_THETA = 500000.0
ROT_DIM = HEAD_DIM // 4
NORM_EPS = 1e-6
FORCE_SCORE = 1e4
NEG_BIG = -1e30

LANES = 128
BLK_HQ, BLK_HF, BLK_HI, BLK_HZ = 0, 8, 16, 24
BLK_NQ, BLK_KS, BLK_KW, BLK_KC, BLK_VC, BLK_VS, BLK_VW = 32, 40, 42, 44, 46, 48, 50
BLK_NZ, BLK_NG = 52, 60
PROJ_BLOCKS = 64
PROJ_COLS = PROJ_BLOCKS * LANES
_SRC = dict(hq=0, hf=1024, hi=2048, hz=3072, nq=4096, kc=5120, vc=5376, ks=5632, vs=5888,
            kw=6144, vw=6400, ng=6656, nz=6680)

PROJ_TM = 1024
PROJ_TN = 1024
ROPE_FULL_TILE = BLK_NQ * LANES // PROJ_TN
ROPE_HEAD_TILE = BLK_KS * LANES // PROJ_TN
ROPE_HEAD_BLOCKS = 4
HGRN_T = 512
PREP_T = 1024
ATT_TQ = 256
SEL_TK = 1024
OUT_TM = 512
VMEM_LIMIT = 56 * 1024 * 1024


def _nt_dot(a, b):
    return lax.dot_general(a, b, (((1,), (1,)), ((), ())), preferred_element_type=F32)


def _sigmoid(x):
    return 1.0 / (1.0 + jnp.exp(-x))


def _proj_kernel(x_ref, g_ref, w_ref, c_ref, s1_ref, s2_ref, o_ref, h_ref):
    j = pl.program_id(1)

    @pl.when(j == 0)
    def _():
        x = x_ref[...]
        ms = jnp.mean(x * x, axis=-1, keepdims=True)
        h_ref[...] = (x * lax.rsqrt(ms + NORM_EPS) * g_ref[...]).astype(BF16)

    o_ref[...] = jnp.dot(h_ref[...], w_ref[...], preferred_element_type=F32)

    def rope_blocks(n_blocks):
        for blk in range(n_blocks):
            sl = slice(blk * LANES, (blk + 1) * LANES)
            a = o_ref[:, sl]
            o_ref[:, sl] = (a * c_ref[...] + pltpu.roll(a, LANES - ROT_DIM // 2, 1) * s1_ref[...]
                            + pltpu.roll(a, ROT_DIM // 2, 1) * s2_ref[...])

    @pl.when(j == ROPE_FULL_TILE)
    def _():
        rope_blocks(PROJ_TN // LANES)

    @pl.when(j == ROPE_HEAD_TILE)
    def _():
        rope_blocks(ROPE_HEAD_BLOCKS)


def _proj(x2, g, w, rope_c, rope_s1, rope_s2, seq):
    n = x2.shape[0]
    tm = min(PROJ_TM, seq)
    pos_tiles = seq // tm
    return pl.pallas_call(
        _proj_kernel,
        out_shape=jax.ShapeDtypeStruct((n, PROJ_COLS), F32),
        grid=(n // tm, PROJ_COLS // PROJ_TN),
        in_specs=[
            pl.BlockSpec((tm, D_MODEL), lambda i, j: (i, 0)),
            pl.BlockSpec((1, D_MODEL), lambda i, j: (0, 0)),
            pl.BlockSpec((D_MODEL, PROJ_TN), lambda i, j: (0, j)),
            pl.BlockSpec((tm, LANES), lambda i, j: (i % pos_tiles, 0)),
            pl.BlockSpec((tm, LANES), lambda i, j: (i % pos_tiles, 0)),
            pl.BlockSpec((tm, LANES), lambda i, j: (i % pos_tiles, 0)),
        ],
        out_specs=pl.BlockSpec((tm, PROJ_TN), lambda i, j: (i, j)),
        scratch_shapes=[pltpu.VMEM((tm, D_MODEL), BF16)],
        compiler_params=pltpu.CompilerParams(
            dimension_semantics=("arbitrary", "arbitrary"), vmem_limit_bytes=VMEM_LIMIT),
        name="proj",
    )(x2, g, w, rope_c, rope_s1, rope_s2)


def _hgrn_kernel(q_ref, f_ref, i_ref, z_ref, par_ref, tri_ref, ones_ref, y_ref, st_ref, o_ref):
    @pl.when(pl.program_id(2) == 0)
    def _():
        st_ref[...] = jnp.zeros_like(st_ref)

    t = q_ref.shape[0]
    nc = t // HGRN_CHUNK
    shape3 = (nc, HGRN_CHUNK, HEAD_DIM)
    par = par_ref[0]
    log_lb, log_1m_lb, one_m_lb, g_norm = par[0:1], par[1:2], par[2:3], par[3:4]
    x = f_ref[...]
    log_sig = jnp.minimum(x, 0.0) - jnp.log1p(jnp.exp(-jnp.abs(x)))
    c = log_1m_lb + log_sig
    lf = jnp.maximum(log_lb, c) + jnp.log1p(jnp.exp(-jnp.abs(log_lb - c)))
    k3 = (one_m_lb / (1.0 + jnp.exp(x))).reshape(shape3)
    q3 = q_ref[...].reshape(shape3)
    v3 = i_ref[...].reshape(shape3)
    v_bf = v3.astype(BF16)

    lf3 = lf.reshape(shape3)
    b3 = jnp.stack([jnp.dot(tri_ref[...], lf3[ci], precision=lax.Precision.HIGHEST,
                            preferred_element_type=F32) for ci in range(nc)])
    row = lax.broadcasted_iota(jnp.int32, shape3, 1)
    t_idx = lax.broadcasted_iota(jnp.int32, (nc, HGRN_CHUNK, HGRN_CHUNK), 1)
    s_idx = lax.broadcasted_iota(jnp.int32, (nc, HGRN_CHUNK, HGRN_CHUNK), 2)

    attn = jnp.zeros((nc, HGRN_CHUNK, HGRN_CHUNK), F32)
    n = HGRN_CHUNK // 2
    while n >= HGRN_SUB:
        blocks = (nc * HGRN_CHUNK // (2 * n), 2 * n, HEAD_DIM)
        b_blk = b3.reshape(blocks)
        ref = jnp.broadcast_to(b_blk[:, n - 1:n, :], blocks).reshape(shape3)
        upper = (row & n) != 0
        q_n = jnp.where(upper, q3 * jnp.exp(jnp.where(upper, b3 - ref, 0.0)), 0.0).astype(BF16)
        k_n = jnp.where(upper, 0.0, k3 * jnp.exp(jnp.where(upper, 0.0, ref - b3))).astype(BF16)
        a_n = jnp.stack([_nt_dot(q_n[ci], k_n[ci]) for ci in range(nc)])
        if 2 * n < HGRN_CHUNK:
            same = (t_idx & -(2 * n)) == (s_idx & -(2 * n))
            a_n = jnp.where(same, a_n, 0.0)
        attn = attn + a_n
        n //= 2
    attn_bf = attn.astype(BF16)
    o_off = jnp.stack([jnp.dot(attn_bf[ci], v_bf[ci], preferred_element_type=F32) for ci in range(nc)])

    subs = (nc * HGRN_CHUNK // HGRN_SUB, HGRN_SUB, HEAD_DIM)
    sub_row = lax.broadcasted_iota(jnp.int32, subs, 1)
    b4, q4, k4, v4 = b3.reshape(subs), q3.reshape(subs), k3.reshape(subs), v3.reshape(subs)
    diag = jnp.zeros(subs, F32)
    for s in range(HGRN_SUB):
        d = jnp.where(sub_row >= s, b4 - b4[:, s:s + 1, :], NEG_BIG)
        p = jnp.exp(d) * q4 * k4[:, s:s + 1, :]
        r = jnp.dot(p.reshape(t, HEAD_DIM).astype(BF16), ones_ref[...], preferred_element_type=F32)
        diag = diag + r.reshape(subs) * v4[:, s:s + 1, :]
    o_intra = o_off + diag.reshape(shape3)

    b_last = b3[:, HGRN_CHUNK - 1:HGRN_CHUNK, :]
    decay = jnp.exp(b_last)
    q_dec = (q3 * jnp.exp(b3)).astype(BF16)
    k_dec = (k3 * jnp.exp(b_last - b3)).astype(BF16)
    upd = [lax.dot_general(v_bf[ci], k_dec[ci], (((0,), (0,)), ((), ())), preferred_element_type=F32)
           for ci in range(nc)]
    st = st_ref[...]
    for ci in range(nc):
        o_ref[ci * HGRN_CHUNK:(ci + 1) * HGRN_CHUNK, :] = o_intra[ci] + _nt_dot(q_dec[ci], st.astype(BF16))
        st = st * decay[ci] + upd[ci]
    st_ref[...] = st

    o = o_ref[...]
    ms = jnp.mean(o * o, axis=-1, keepdims=True)
    z = z_ref[...]
    y_ref[...] = (o * lax.rsqrt(ms + NORM_EPS) * g_norm * (z * _sigmoid(z))).astype(BF16)


def _hgrn(proj, par, tri, ones, batch, seq):
    t = min(HGRN_T, seq)
    nt = seq // t

    def col(base):
        return pl.BlockSpec((t, HEAD_DIM), lambda b, h, ti: (b * nt + ti, base + h))

    return pl.pallas_call(
        _hgrn_kernel,
        out_shape=jax.ShapeDtypeStruct((batch * seq, HGRN_HEADS * HEAD_DIM), BF16),
        grid=(batch, HGRN_HEADS, nt),
        in_specs=[
            col(BLK_HQ), col(BLK_HF), col(BLK_HI), col(BLK_HZ),
            pl.BlockSpec((1, 8, HEAD_DIM), lambda b, h, ti: (h, 0, 0)),
            pl.BlockSpec((HGRN_CHUNK, HGRN_CHUNK), lambda b, h, ti: (0, 0)),
            pl.BlockSpec((HEAD_DIM, HEAD_DIM), lambda b, h, ti: (0, 0)),
        ],
        out_specs=pl.BlockSpec((t, HEAD_DIM), lambda b, h, ti: (b * nt + ti, h)),
        scratch_shapes=[pltpu.VMEM((HEAD_DIM, HEAD_DIM), F32), pltpu.VMEM((t, HEAD_DIM), F32)],
        compiler_params=pltpu.CompilerParams(
            dimension_semantics=("arbitrary", "arbitrary", "arbitrary"), vmem_limit_bytes=VMEM_LIMIT),
        name="hgrn",
    )(proj, proj, proj, proj, par, tri, ones)


def _compress_kernel(x_ref, w1_ref, b1_ref, w2_ref, pos_ref, tab_ref, o_ref):
    n_rows = o_ref.shape[-2]
    acc_lo = jnp.zeros((n_rows, HEAD_DIM), F32)
    acc_hi = jnp.zeros((n_rows, HEAD_DIM), F32)
    for r in range(CMP_STRIDE):
        xr = x_ref[pl.ds(r, n_rows, stride=CMP_STRIDE), :]
        lo, hi = r, CMP_STRIDE + r
        acc_lo = acc_lo + jnp.dot((xr + pos_ref[0, lo:lo + 1, :]).astype(BF16),
                                  w1_ref[0, lo * HEAD_DIM:(lo + 1) * HEAD_DIM, :],
                                  preferred_element_type=F32)
        acc_hi = acc_hi + jnp.dot((xr + pos_ref[0, hi:hi + 1, :]).astype(BF16),
                                  w1_ref[0, hi * HEAD_DIM:(hi + 1) * HEAD_DIM, :],
                                  preferred_element_type=F32)
    pre = acc_lo + pltpu.roll(acc_hi, n_rows - 1, 0) + b1_ref[0]
    mid = pre * _sigmoid(pre)
    out = jnp.dot(mid.astype(BF16), w2_ref[0], preferred_element_type=F32)
    o_ref[0, 0, 0] = (out * tab_ref[0, 0] + pltpu.roll(out, LANES - ROT_DIM // 2, 1) * tab_ref[0, 1]
                      + pltpu.roll(out, ROT_DIM // 2, 1) * tab_ref[0, 2])


def _compress(proj, w1, b1, w2, pos, tabs, batch, seq):
    n_rows = seq // CMP_STRIDE
    return pl.pallas_call(
        _compress_kernel,
        out_shape=jax.ShapeDtypeStruct((2, batch, NSA_KV_GROUPS, n_rows, HEAD_DIM), F32),
        grid=(2, batch, NSA_KV_GROUPS),
        in_specs=[
            pl.BlockSpec((seq, HEAD_DIM), lambda kv, b, g: (b, BLK_KC + 2 * kv + g)),
            pl.BlockSpec((1, CMP_BLOCK * HEAD_DIM, HEAD_DIM), lambda kv, b, g: (kv, 0, 0)),
            pl.BlockSpec((1, 1, HEAD_DIM), lambda kv, b, g: (kv, 0, 0)),
            pl.BlockSpec((1, HEAD_DIM, HEAD_DIM), lambda kv, b, g: (kv, 0, 0)),
            pl.BlockSpec((1, CMP_BLOCK, HEAD_DIM), lambda kv, b, g: (kv, 0, 0)),
            pl.BlockSpec((1, 3, n_rows, HEAD_DIM), lambda kv, b, g: (kv, 0, 0, 0)),
        ],
        out_specs=pl.BlockSpec((1, 1, 1, n_rows, HEAD_DIM), lambda kv, b, g: (kv, b, g, 0, 0)),
        compiler_params=pltpu.CompilerParams(
            dimension_semantics=("arbitrary", "arbitrary", "arbitrary"), vmem_limit_bytes=VMEM_LIMIT),
        name="compress",
    )(proj, w1, b1, w2, pos, tabs)


def _kvprep_kernel(ks_ref, kw_ref, vs_ref, vw_ref, kaug_ref, vaug_ref, kwb_ref, vwb_ref):
    t = ks_ref.shape[0]
    t0 = pl.program_id(2) * t
    row = lax.broadcasted_iota(jnp.int32, (t, LANES), 0) + t0
    col = lax.broadcasted_iota(jnp.int32, (t, LANES), 1)
    block_lane = lax.shift_right_logical(row, 6) & (LANES - 1)
    kaug_ref[0, 0, :, :HEAD_DIM] = ks_ref[...].astype(BF16)
    kaug_ref[0, 0, :, HEAD_DIM:] = jnp.where(block_lane == col, 1.0, 0.0).astype(BF16)
    vaug_ref[0, 0, :, :HEAD_DIM] = vs_ref[...].astype(BF16)
    vaug_ref[0, 0, :, HEAD_DIM:] = jnp.ones((t, LANES), BF16)
    kwb_ref[0, 0] = kw_ref[...].astype(BF16)
    vwb_ref[0, 0] = vw_ref[...].astype(BF16)


def _kvprep(proj, batch, seq):
    t = min(PREP_T, seq)
    nt = seq // t

    def col(base):
        return pl.BlockSpec((t, HEAD_DIM), lambda b, g, ti: (b * nt + ti, base + g))

    def out(width):
        return pl.BlockSpec((1, 1, t, width), lambda b, g, ti: (b, g, ti, 0))

    def shape(width):
        return jax.ShapeDtypeStruct((batch, NSA_KV_GROUPS, seq, width), BF16)

    return pl.pallas_call(
        _kvprep_kernel,
        out_shape=(shape(2 * HEAD_DIM), shape(2 * HEAD_DIM), shape(HEAD_DIM), shape(HEAD_DIM)),
        grid=(batch, NSA_KV_GROUPS, nt),
        in_specs=[col(BLK_KS), col(BLK_KW), col(BLK_VS), col(BLK_VW)],
        out_specs=(out(2 * HEAD_DIM), out(2 * HEAD_DIM), out(HEAD_DIM), out(HEAD_DIM)),
        compiler_params=pltpu.CompilerParams(
            dimension_semantics=("arbitrary", "arbitrary", "arbitrary"), vmem_limit_bytes=VMEM_LIMIT),
        name="kvprep",
    )(proj, proj, proj, proj)


def _cmpwin_kernel(q_ref, kc_ref, vc_ref, kw_ref, vw_ref, ng_ref, ov_ref, part_ref, selb_ref,
                   val_ref, sel_ref, *, n_blocks, win_len):
    tq = q_ref.shape[0]
    n_cmp = kc_ref.shape[-2]
    nblkp = selb_ref.shape[-1]
    q0 = pl.program_id(2) * tq
    scale = HEAD_DIM ** -0.5

    kc = kc_ref[0, 0, 0].astype(BF16)
    vc = vc_ref[0, 0, 0].astype(BF16)
    t_c = lax.broadcasted_iota(jnp.int32, (tq, n_cmp), 0) + q0
    n_c = lax.broadcasted_iota(jnp.int32, (tq, n_cmp), 1)
    vis_c = n_c * CMP_STRIDE + (CMP_BLOCK - 1) <= t_c

    w0 = pl.multiple_of(jnp.maximum(q0 + tq - win_len, 0), tq)
    kwin = kw_ref[0, 0, pl.ds(w0, win_len), :]
    vwin = vw_ref[0, 0, pl.ds(w0, win_len), :]
    t_w = lax.broadcasted_iota(jnp.int32, (tq, win_len), 0) + q0
    kp_w = lax.broadcasted_iota(jnp.int32, (tq, win_len), 1) + w0
    vis_w = (kp_w <= t_w) & (t_w - kp_w < WINDOW)

    gates = _sigmoid(ng_ref[...])
    p_sum = jnp.zeros((tq, n_cmp), F32)
    for h in range(HEADS_PER_GROUP):
        sl = slice(h * HEAD_DIM, (h + 1) * HEAD_DIM)
        qh = (q_ref[:, sl] * scale).astype(BF16)
        s = jnp.where(vis_c, _nt_dot(qh, kc), NEG_BIG)
        m = jnp.max(s, axis=-1, keepdims=True)
        m = jnp.where(m < 0.5 * NEG_BIG, 0.0, m)
        p = jnp.exp(s - m)
        p = p / jnp.maximum(jnp.sum(p, axis=-1, keepdims=True), 1e-30)
        o_c = jnp.dot(p.astype(BF16), vc, preferred_element_type=F32)
        p_sum = p_sum + p
        s = jnp.where(vis_w, _nt_dot(qh, kwin), NEG_BIG)
        m = jnp.max(s, axis=-1, keepdims=True)
        p = jnp.exp(s - m)
        p = p / jnp.maximum(jnp.sum(p, axis=-1, keepdims=True), 1e-30)
        o_w = jnp.dot(p.astype(BF16), vwin, preferred_element_type=F32)
        g_c = gates[:, 0 * HEADS_PER_GROUP + h:0 * HEADS_PER_GROUP + h + 1]
        g_w = gates[:, 2 * HEADS_PER_GROUP + h:2 * HEADS_PER_GROUP + h + 1]
        part_ref[:, sl] = g_c * o_c + g_w * o_w

    p_hi = p_sum.astype(BF16)
    p_lo = (p_sum - p_hi.astype(F32)).astype(BF16)
    imp = (jnp.dot(p_hi, ov_ref[...], preferred_element_type=F32)
           + jnp.dot(p_lo, ov_ref[...], preferred_element_type=F32))
    t_b = lax.broadcasted_iota(jnp.int32, (tq, nblkp), 0) + q0
    blk = lax.broadcasted_iota(jnp.int32, (tq, nblkp), 1)
    cur = lax.shift_right_logical(t_b, 6)
    forced = (blk == 0) | (blk == cur) | (blk == cur - 1)
    val = jnp.where(forced, FORCE_SCORE, jnp.where(blk * SEL_BLOCK <= t_b, imp, -1.0))
    val_ref[...] = jnp.where(blk < n_blocks, val, -3.0)
    sel_ref[...] = jnp.zeros_like(sel_ref)
    blk_f = blk.astype(F32)

    def pick(_, carry):
        v = val_ref[...]
        m = jnp.max(v, axis=-1, keepdims=True)
        idx = jnp.min(jnp.where(v == m, blk_f, float(nblkp)), axis=-1, keepdims=True)
        hit = blk_f == idx
        sel_ref[...] = jnp.where(hit, 1.0, sel_ref[...])
        val_ref[...] = jnp.where(hit, -4.0, v)
        return carry

    lax.fori_loop(0, min(N_SELECT, n_blocks), pick, 0)
    selb_ref[0, 0] = jnp.where(sel_ref[...] > 0.5, 0.0, NEG_BIG).astype(BF16)


def _cmpwin(proj, cmp_kv, kwb, vwb, ov, batch, seq):
    tq = min(ATT_TQ, seq)
    nq = seq // tq
    n_cmp = seq // CMP_STRIDE
    n_blocks = seq // SEL_BLOCK
    nblkp = ov.shape[1]
    win_len = min(WINDOW + tq, seq)
    q_cols = HEADS_PER_GROUP * HEAD_DIM
    kernel = functools.partial(_cmpwin_kernel, n_blocks=n_blocks, win_len=win_len)
    return pl.pallas_call(
        kernel,
        out_shape=(jax.ShapeDtypeStruct((batch * seq, NSA_HEADS * HEAD_DIM), F32),
                   jax.ShapeDtypeStruct((batch, NSA_KV_GROUPS, seq, nblkp), BF16)),
        grid=(batch, NSA_KV_GROUPS, nq),
        in_specs=[
            pl.BlockSpec((tq, q_cols), lambda b, g, i: (b * nq + i, BLK_NQ * LANES // q_cols + g)),
            pl.BlockSpec((1, 1, 1, n_cmp, HEAD_DIM), lambda b, g, i: (0, b, g, 0, 0)),
            pl.BlockSpec((1, 1, 1, n_cmp, HEAD_DIM), lambda b, g, i: (1, b, g, 0, 0)),
            pl.BlockSpec((1, 1, seq, HEAD_DIM), lambda b, g, i: (b, g, 0, 0)),
            pl.BlockSpec((1, 1, seq, HEAD_DIM), lambda b, g, i: (b, g, 0, 0)),
            pl.BlockSpec((tq, LANES), lambda b, g, i: (b * nq + i, BLK_NG + g)),
            pl.BlockSpec((n_cmp, nblkp), lambda b, g, i: (0, 0)),
        ],
        out_specs=(pl.BlockSpec((tq, q_cols), lambda b, g, i: (b * nq + i, g)),
                   pl.BlockSpec((1, 1, tq, nblkp), lambda b, g, i: (b, g, i, 0))),
        scratch_shapes=[pltpu.VMEM((tq, nblkp), F32), pltpu.VMEM((tq, nblkp), F32)],
        compiler_params=pltpu.CompilerParams(
            dimension_semantics=("arbitrary", "arbitrary", "arbitrary"), vmem_limit_bytes=VMEM_LIMIT),
        name="cmpwin",
    )(proj, cmp_kv, cmp_kv, kwb, vwb, proj, ov)


def _select_kernel(q_ref, selb_ref, kaug_ref, vaug_ref, part_ref, ng_ref, nz_ref, y_ref,
                   qaug_ref, acc_ref, m_ref, *, tk):
    tq = q_ref.shape[0]
    n_halves = selb_ref.shape[-1] // LANES
    q0 = pl.program_id(2) * tq
    scale = HEAD_DIM ** -0.5

    for h in range(HEADS_PER_GROUP):
        qh = (q_ref[:, h * HEAD_DIM:(h + 1) * HEAD_DIM] * scale).astype(BF16)
        for half in range(n_halves):
            qaug_ref[half * HEADS_PER_GROUP + h, :, :HEAD_DIM] = qh
            qaug_ref[half * HEADS_PER_GROUP + h, :, HEAD_DIM:] = selb_ref[0, 0, :, half * LANES:(half + 1) * LANES]
    acc_ref[...] = jnp.zeros_like(acc_ref)
    m_ref[...] = jnp.full_like(m_ref, NEG_BIG)

    def tile(kt, causal):
        k0 = pl.multiple_of(kt * tk, tk)
        k_t = kaug_ref[0, 0, pl.ds(k0, tk), :]
        v_t = vaug_ref[0, 0, pl.ds(k0, tk), :]
        half = k0 // (SEL_BLOCK * LANES)
        if causal:
            t_pos = lax.broadcasted_iota(jnp.int32, (tq, tk), 0) + q0
            k_pos = lax.broadcasted_iota(jnp.int32, (tq, tk), 1) + k0
            vis = k_pos <= t_pos
        for h in range(HEADS_PER_GROUP):
            s = _nt_dot(qaug_ref[half * HEADS_PER_GROUP + h], k_t)
            if causal:
                s = jnp.where(vis, s, NEG_BIG)
            m_prev = m_ref[h]
            m_new = jnp.maximum(m_prev, jnp.max(s, axis=-1, keepdims=True))
            p = jnp.exp(s - pltpu.repeat(m_new, tk // LANES, 1))
            alpha = jnp.exp(m_prev - m_new)
            acc_ref[h] = (acc_ref[h] * pltpu.repeat(alpha, 2, 1)
                          + jnp.dot(p.astype(BF16), v_t, preferred_element_type=F32))
            m_ref[h] = m_new

    n_full = q0 // tk

    def full_tile(kt, carry):
        tile(kt, False)
        return carry

    lax.fori_loop(0, n_full, full_tile, 0)
    tile(n_full, True)

    gates = _sigmoid(ng_ref[...])
    for h in range(HEADS_PER_GROUP):
        sl = slice(h * HEAD_DIM, (h + 1) * HEAD_DIM)
        acc = acc_ref[h]
        o_s = acc[:, :HEAD_DIM] / jnp.maximum(acc[:, HEAD_DIM:], 1e-30)
        g_s = gates[:, 1 * HEADS_PER_GROUP + h:1 * HEADS_PER_GROUP + h + 1]
        z = nz_ref[:, sl]
        y_ref[:, sl] = ((part_ref[:, sl] + g_s * o_s) * (z * _sigmoid(z))).astype(BF16)


def _select(proj, selb, kaug, vaug, part, batch, seq):
    tq = min(ATT_TQ, seq)
    tk = min(SEL_TK, seq)
    nq = seq // tq
    nblkp = selb.shape[-1]
    q_cols = HEADS_PER_GROUP * HEAD_DIM
    kernel = functools.partial(_select_kernel, tk=tk)
    return pl.pallas_call(
        kernel,
        out_shape=jax.ShapeDtypeStruct((batch * seq, NSA_HEADS * HEAD_DIM), BF16),
        grid=(batch, NSA_KV_GROUPS, nq),
        in_specs=[
            pl.BlockSpec((tq, q_cols), lambda b, g, i: (b * nq + i, BLK_NQ * LANES // q_cols + g)),
            pl.BlockSpec((1, 1, tq, nblkp), lambda b, g, i: (b, g, i, 0)),
            pl.BlockSpec((1, 1, seq, 2 * HEAD_DIM), lambda b, g, i: (b, g, 0, 0)),
            pl.BlockSpec((1, 1, seq, 2 * HEAD_DIM), lambda b, g, i: (b, g, 0, 0)),
            pl.BlockSpec((tq, q_cols), lambda b, g, i: (b * nq + i, g)),
            pl.BlockSpec((tq, LANES), lambda b, g, i: (b * nq + i, BLK_NG + g)),
            pl.BlockSpec((tq, q_cols), lambda b, g, i: (b * nq + i, BLK_NZ * LANES // q_cols + g)),
        ],
        out_specs=pl.BlockSpec((tq, q_cols), lambda b, g, i: (b * nq + i, g)),
        scratch_shapes=[
            pltpu.VMEM((nblkp // LANES * HEADS_PER_GROUP, tq, 2 * HEAD_DIM), BF16),
            pltpu.VMEM((HEADS_PER_GROUP, tq, 2 * HEAD_DIM), F32),
            pltpu.VMEM((HEADS_PER_GROUP, tq, LANES), F32),
        ],
        compiler_params=pltpu.CompilerParams(
            dimension_semantics=("arbitrary", "arbitrary", "arbitrary"), vmem_limit_bytes=VMEM_LIMIT),
        name="select",
    )(proj, selb, kaug, vaug, part, proj, proj)


def _outproj_kernel(yh_ref, yn_ref, w_ref, x_ref, g_ref, o_ref):
    half = yh_ref.shape[1]
    z = (jnp.dot(yh_ref[...], w_ref[:half, :], preferred_element_type=F32)
         + jnp.dot(yn_ref[...], w_ref[half:, :], preferred_element_type=F32))
    ms = jnp.mean(z * z, axis=-1, keepdims=True)
    o_ref[...] = x_ref[...] + z * lax.rsqrt(ms + NORM_EPS) * g_ref[...]


def _outproj(yh, yn, w, x2, g):
    n = x2.shape[0]
    tm = min(OUT_TM, n)
    return pl.pallas_call(
        _outproj_kernel,
        out_shape=jax.ShapeDtypeStruct((n, D_MODEL), F32),
        grid=(n // tm,),
        in_specs=[
            pl.BlockSpec((tm, yh.shape[1]), lambda i: (i, 0)),
            pl.BlockSpec((tm, yn.shape[1]), lambda i: (i, 0)),
            pl.BlockSpec(w.shape, lambda i: (0, 0)),
            pl.BlockSpec((tm, D_MODEL), lambda i: (i, 0)),
            pl.BlockSpec((1, D_MODEL), lambda i: (0, 0)),
        ],
        out_specs=pl.BlockSpec((tm, D_MODEL), lambda i: (i, 0)),
        compiler_params=pltpu.CompilerParams(
            dimension_semantics=("arbitrary",), vmem_limit_bytes=VMEM_LIMIT),
        name="outproj",
    )(yh, yn, w, x2, g)


def _proj_column_map():
    idx = np.full((PROJ_COLS,), -1, np.int64)

    def put(dst_blk, src, width):
        idx[dst_blk * LANES:dst_blk * LANES + width] = np.arange(src, src + width)

    for name, blk, width in (("hq", BLK_HQ, 1024), ("hf", BLK_HF, 1024), ("hi", BLK_HI, 1024),
                             ("hz", BLK_HZ, 1024), ("nq", BLK_NQ, 1024), ("ks", BLK_KS, 256),
                             ("kw", BLK_KW, 256), ("kc", BLK_KC, 256), ("vc", BLK_VC, 256),
                             ("vs", BLK_VS, 256), ("vw", BLK_VW, 256), ("nz", BLK_NZ, 1024)):
        put(blk, _SRC[name], width)
    for g in range(NSA_KV_GROUPS):
        for br in range(N_BRANCHES):
            for h in range(HEADS_PER_GROUP):
                idx[(BLK_NG + g) * LANES + br * HEADS_PER_GROUP + h] = (
                    _SRC["ng"] + (g * HEADS_PER_GROUP + h) * N_BRANCHES + br)
    return idx


def _rope_tables(pos):
    half = ROT_DIM // 2
    inv = ROPE_THETA ** (-2.0 * jnp.arange(half, dtype=F32) / ROT_DIM)
    ang = pos.astype(F32)[:, None] * inv[None, :]
    cos, sin = jnp.cos(ang), jnp.sin(ang)
    n = pos.shape[0]
    pad = jnp.zeros((n, HEAD_DIM - ROT_DIM), F32)
    zero = jnp.zeros((n, half), F32)
    c = jnp.concatenate([cos, cos, pad + 1.0], axis=1)
    s1 = jnp.concatenate([-sin, zero, pad], axis=1)
    s2 = jnp.concatenate([zero, sin, pad], axis=1)
    return c, s1, s2


def _overlap_matrix(n_cmp, n_blocks, nblkp):
    start = np.arange(n_cmp)[:, None] * CMP_STRIDE
    sel = np.arange(nblkp)[None, :] * SEL_BLOCK
    ov = (start < sel + SEL_BLOCK) & (start + CMP_BLOCK > sel)
    ov &= np.arange(nblkp)[None, :] < n_blocks
    ov &= np.arange(n_cmp)[:, None] < (n_cmp - 1)
    return jnp.asarray(ov, BF16)


def kernel(x, pre_norm, post_norm, w_in, hgrn_lb_logits, hgrn_out_norm, cmp_pos_k, cmp_w1_k, cmp_b1_k,
           cmp_w2_k, cmp_pos_v, cmp_w1_v, cmp_b1_v, cmp_w2_v, w_out):
    batch, seq, _ = x.shape
    depth = w_in.shape[0]
    n_cmp = seq // CMP_STRIDE
    n_blocks = seq // SEL_BLOCK
    nblkp = -(-n_blocks // LANES) * LANES

    lb_probs = jax.nn.softmax(hgrn_lb_logits.astype(F32), axis=0)
    lower = jnp.maximum(jnp.cumsum(lb_probs, axis=0) - lb_probs[0:1], 0.0)
    lower = lower.reshape(depth, HGRN_HEADS, 1, HEAD_DIM)
    g_out = jnp.broadcast_to(hgrn_out_norm.astype(F32)[:, None, None, :], lower.shape)
    hgrn_par = jnp.concatenate(
        [jnp.log(lower), jnp.log1p(-lower), 1.0 - lower, g_out, jnp.zeros((depth, HGRN_HEADS, 4, HEAD_DIM), F32)],
        axis=2)

    col_map = _proj_column_map()
    col_src = jnp.asarray(np.maximum(col_map, 0), jnp.int32)
    col_ok = jnp.asarray(col_map >= 0)

    rope_c, rope_s1, rope_s2 = _rope_tables(jnp.arange(seq))
    cmp_tab_k = jnp.stack(_rope_tables(jnp.arange(n_cmp) * CMP_STRIDE + CMP_BLOCK - 1))
    cmp_tab_v = jnp.stack([jnp.ones((n_cmp, HEAD_DIM), F32), jnp.zeros((n_cmp, HEAD_DIM), F32),
                           jnp.zeros((n_cmp, HEAD_DIM), F32)])
    cmp_tabs = jnp.stack([cmp_tab_k, cmp_tab_v])
    ov = _overlap_matrix(n_cmp, n_blocks, nblkp)
    tri = jnp.asarray(np.tril(np.ones((HGRN_CHUNK, HGRN_CHUNK), np.float32)))
    ones = jnp.ones((HEAD_DIM, HEAD_DIM), BF16)

    x2 = x.reshape(batch * seq, D_MODEL)
    for layer in range(depth):
        w_perm = jnp.where(col_ok[None, :], jnp.take(w_in[layer], col_src, axis=1), 0.0).astype(BF16)
        proj = _proj(x2, pre_norm[layer][None, :], w_perm, rope_c, rope_s1, rope_s2, seq)
        y_h = _hgrn(proj, hgrn_par[layer], tri, ones, batch, seq)
        cmp_kv = _compress(
            proj,
            jnp.stack([cmp_w1_k[layer], cmp_w1_v[layer]]).astype(BF16),
            jnp.stack([cmp_b1_k[layer], cmp_b1_v[layer]])[:, None, :],
            jnp.stack([cmp_w2_k[layer], cmp_w2_v[layer]]).astype(BF16),
            jnp.stack([cmp_pos_k[layer], cmp_pos_v[layer]]),
            cmp_tabs, batch, seq)
        kaug, vaug, kwb, vwb = _kvprep(proj, batch, seq)
        part, selb = _cmpwin(proj, cmp_kv, kwb, vwb, ov, batch, seq)
        y_n = _select(proj, selb, kaug, vaug, part, batch, seq)
        x2 = _outproj(y_h, y_n, w_out[layer].astype(BF16), x2, post_norm[layer][None, :])
    return x2.reshape(batch, seq, D_MODEL)
```

```python
import functools

import numpy as np
import jax
import jax.numpy as jnp
from jax import lax
from jax.experimental import pallas as pl
from jax.experimental.pallas import tpu as pltpu

F32 = jnp.float32
BF16 = jnp.bfloat16

D_MODEL = 1024
HEAD_DIM = 128
HGRN_HEADS = 8
NSA_HEADS = 8
NSA_KV_GROUPS = 2
HEADS_PER_GROUP = NSA_HEADS // NSA_KV_GROUPS
N_BRANCHES = 3
HGRN_CHUNK = 64
HGRN_SUB = 16
CMP_BLOCK = 32
CMP_STRIDE = 16
SEL_BLOCK = 64
N_SELECT = 16
WINDOW = 512
ROPE_THETA = 500000.0
ROT_DIM = HEAD_DIM // 4
NORM_EPS = 1e-6
FORCE_SCORE = 1e4
NEG_BIG = -1e30

LANES = 128
BLK_HQ, BLK_HF, BLK_HI, BLK_HZ = 0, 8, 16, 24
BLK_NQ, BLK_KS, BLK_KW, BLK_KC, BLK_VC, BLK_VS, BLK_VW = 32, 40, 42, 44, 46, 48, 50
BLK_NZ, BLK_NG = 52, 60
PROJ_BLOCKS = 64
PROJ_COLS = PROJ_BLOCKS * LANES
_SRC = dict(hq=0, hf=1024, hi=2048, hz=3072, nq=4096, kc=5120, vc=5376, ks=5632, vs=5888,
            kw=6144, vw=6400, ng=6656, nz=6680)

PROJ_TM = 1024
PROJ_TN = 1024
ROPE_FULL_TILE = BLK_NQ * LANES // PROJ_TN
ROPE_HEAD_TILE = BLK_KS * LANES // PROJ_TN
ROPE_HEAD_BLOCKS = 4
HGRN_T = 512
PREP_T = 1024
ATT_TQ = 256
SEL_TQ = 512
SEL_TK = 1024
OUT_TM = 512
VMEM_LIMIT = 56 * 1024 * 1024


def _nt_dot(a, b):
    return lax.dot_general(a, b, (((1,), (1,)), ((), ())), preferred_element_type=F32)


def _sigmoid(x):
    return 1.0 / (1.0 + jnp.exp(-x))


def _proj_kernel(x_ref, g_ref, w_ref, c_ref, s1_ref, s2_ref, o_ref, h_ref):
    j = pl.program_id(1)

    @pl.when(j == 0)
    def _():
        x = x_ref[...]
        ms = jnp.mean(x * x, axis=-1, keepdims=True)
        h_ref[...] = (x * lax.rsqrt(ms + NORM_EPS) * g_ref[...]).astype(BF16)

    o_ref[...] = jnp.dot(h_ref[...], w_ref[...], preferred_element_type=F32)

    def rope_blocks(n_blocks):
        for blk in range(n_blocks):
            sl = slice(blk * LANES, (blk + 1) * LANES)
            a = o_ref[:, sl]
            o_ref[:, sl] = (a * c_ref[...] + pltpu.roll(a, LANES - ROT_DIM // 2, 1) * s1_ref[...]
                            + pltpu.roll(a, ROT_DIM // 2, 1) * s2_ref[...])

    @pl.when(j == ROPE_FULL_TILE)
    def _():
        rope_blocks(PROJ_TN // LANES)

    @pl.when(j == ROPE_HEAD_TILE)
    def _():
        rope_blocks(ROPE_HEAD_BLOCKS)


def _proj(x2, g, w, rope_c, rope_s1, rope_s2, seq):
    n = x2.shape[0]
    tm = min(PROJ_TM, seq)
    pos_tiles = seq // tm
    return pl.pallas_call(
        _proj_kernel,
        out_shape=jax.ShapeDtypeStruct((n, PROJ_COLS), F32),
        grid=(n // tm, PROJ_COLS // PROJ_TN),
        in_specs=[
            pl.BlockSpec((tm, D_MODEL), lambda i, j: (i, 0)),
            pl.BlockSpec((1, D_MODEL), lambda i, j: (0, 0)),
            pl.BlockSpec((D_MODEL, PROJ_TN), lambda i, j: (0, j)),
            pl.BlockSpec((tm, LANES), lambda i, j: (i % pos_tiles, 0)),
            pl.BlockSpec((tm, LANES), lambda i, j: (i % pos_tiles, 0)),
            pl.BlockSpec((tm, LANES), lambda i, j: (i % pos_tiles, 0)),
        ],
        out_specs=pl.BlockSpec((tm, PROJ_TN), lambda i, j: (i, j)),
        scratch_shapes=[pltpu.VMEM((tm, D_MODEL), BF16)],
        compiler_params=pltpu.CompilerParams(
            dimension_semantics=("arbitrary", "arbitrary"), vmem_limit_bytes=VMEM_LIMIT),
        name="proj",
    )(x2, g, w, rope_c, rope_s1, rope_s2)


def _hgrn_kernel(q_ref, f_ref, i_ref, z_ref, par_ref, tri_ref, ones_ref, y_ref, st_ref, o_ref):
    @pl.when(pl.program_id(2) == 0)
    def _():
        st_ref[...] = jnp.zeros_like(st_ref)

    t = q_ref.shape[0]
    nc = t // HGRN_CHUNK
    shape3 = (nc, HGRN_CHUNK, HEAD_DIM)
    par = par_ref[0]
    log_lb, log_1m_lb, one_m_lb, g_norm = par[0:1], par[1:2], par[2:3], par[3:4]
    x = f_ref[...]
    log_sig = jnp.minimum(x, 0.0) - jnp.log1p(jnp.exp(-jnp.abs(x)))
    c = log_1m_lb + log_sig
    lf = jnp.maximum(log_lb, c) + jnp.log1p(jnp.exp(-jnp.abs(log_lb - c)))
    k3 = (one_m_lb / (1.0 + jnp.exp(x))).reshape(shape3)
    q3 = q_ref[...].reshape(shape3)
    v3 = i_ref[...].reshape(shape3)
    v_bf = v3.astype(BF16)

    lf3 = lf.reshape(shape3)
    b3 = jnp.stack([jnp.dot(tri_ref[...], lf3[ci], precision=lax.Precision.HIGHEST,
                            preferred_element_type=F32) for ci in range(nc)])
    row = lax.broadcasted_iota(jnp.int32, shape3, 1)
    t_idx = lax.broadcasted_iota(jnp.int32, (nc, HGRN_CHUNK, HGRN_CHUNK), 1)
    s_idx = lax.broadcasted_iota(jnp.int32, (nc, HGRN_CHUNK, HGRN_CHUNK), 2)

    attn = jnp.zeros((nc, HGRN_CHUNK, HGRN_CHUNK), F32)
    n = HGRN_CHUNK // 2
    while n >= HGRN_SUB:
        blocks = (nc * HGRN_CHUNK // (2 * n), 2 * n, HEAD_DIM)
        b_blk = b3.reshape(blocks)
        ref = jnp.broadcast_to(b_blk[:, n - 1:n, :], blocks).reshape(shape3)
        upper = (row & n) != 0
        q_n = jnp.where(upper, q3 * jnp.exp(jnp.where(upper, b3 - ref, 0.0)), 0.0).astype(BF16)
        k_n = jnp.where(upper, 0.0, k3 * jnp.exp(jnp.where(upper, 0.0, ref - b3))).astype(BF16)
        a_n = jnp.stack([_nt_dot(q_n[ci], k_n[ci]) for ci in range(nc)])
        if 2 * n < HGRN_CHUNK:
            same = (t_idx & -(2 * n)) == (s_idx & -(2 * n))
            a_n = jnp.where(same, a_n, 0.0)
        attn = attn + a_n
        n //= 2
    attn_bf = attn.astype(BF16)
    o_off = jnp.stack([jnp.dot(attn_bf[ci], v_bf[ci], preferred_element_type=F32) for ci in range(nc)])

    subs = (nc * HGRN_CHUNK // HGRN_SUB, HGRN_SUB, HEAD_DIM)
    sub_row = lax.broadcasted_iota(jnp.int32, subs, 1)
    b4, q4, k4, v4 = b3.reshape(subs), q3.reshape(subs), k3.reshape(subs), v3.reshape(subs)
    diag = jnp.zeros(subs, F32)
    for s in range(HGRN_SUB):
        d = jnp.where(sub_row >= s, b4 - b4[:, s:s + 1, :], NEG_BIG)
        p = jnp.exp(d) * q4 * k4[:, s:s + 1, :]
        r = jnp.dot(p.reshape(t, HEAD_DIM).astype(BF16), ones_ref[...], preferred_element_type=F32)
        diag = diag + r.reshape(subs) * v4[:, s:s + 1, :]
    o_intra = o_off + diag.reshape(shape3)

    b_last = b3[:, HGRN_CHUNK - 1:HGRN_CHUNK, :]
    decay = jnp.exp(b_last)
    q_dec = (q3 * jnp.exp(b3)).astype(BF16)
    k_dec = (k3 * jnp.exp(b_last - b3)).astype(BF16)
    upd = [lax.dot_general(v_bf[ci], k_dec[ci], (((0,), (0,)), ((), ())), preferred_element_type=F32)
           for ci in range(nc)]
    st = st_ref[...]
    for ci in range(nc):
        o_ref[ci * HGRN_CHUNK:(ci + 1) * HGRN_CHUNK, :] = o_intra[ci] + _nt_dot(q_dec[ci], st.astype(BF16))
        st = st * decay[ci] + upd[ci]
    st_ref[...] = st

    o = o_ref[...]
    ms = jnp.mean(o * o, axis=-1, keepdims=True)
    z = z_ref[...]
    y_ref[...] = (o * lax.rsqrt(ms + NORM_EPS) * g_norm * (z * _sigmoid(z))).astype(BF16)


def _hgrn(proj, par, tri, ones, batch, seq):
    t = min(HGRN_T, seq)
    nt = seq // t

    def col(base):
        return pl.BlockSpec((t, HEAD_DIM), lambda b, h, ti: (b * nt + ti, base + h))

    return pl.pallas_call(
        _hgrn_kernel,
        out_shape=jax.ShapeDtypeStruct((batch * seq, HGRN_HEADS * HEAD_DIM), BF16),
        grid=(batch, HGRN_HEADS, nt),
        in_specs=[
            col(BLK_HQ), col(BLK_HF), col(BLK_HI), col(BLK_HZ),
            pl.BlockSpec((1, 8, HEAD_DIM), lambda b, h, ti: (h, 0, 0)),
            pl.BlockSpec((HGRN_CHUNK, HGRN_CHUNK), lambda b, h, ti: (0, 0)),
            pl.BlockSpec((HEAD_DIM, HEAD_DIM), lambda b, h, ti: (0, 0)),
        ],
        out_specs=pl.BlockSpec((t, HEAD_DIM), lambda b, h, ti: (b * nt + ti, h)),
        scratch_shapes=[pltpu.VMEM((HEAD_DIM, HEAD_DIM), F32), pltpu.VMEM((t, HEAD_DIM), F32)],
        compiler_params=pltpu.CompilerParams(
            dimension_semantics=("arbitrary", "arbitrary", "arbitrary"), vmem_limit_bytes=VMEM_LIMIT),
        name="hgrn",
    )(proj, proj, proj, proj, par, tri, ones)


def _compress_kernel(x_ref, w1_ref, b1_ref, w2_ref, pos_ref, tab_ref, o_ref):
    n_rows = o_ref.shape[-2]
    acc_lo = jnp.zeros((n_rows, HEAD_DIM), F32)
    acc_hi = jnp.zeros((n_rows, HEAD_DIM), F32)
    for r in range(CMP_STRIDE):
        xr = x_ref[pl.ds(r, n_rows, stride=CMP_STRIDE), :]
        lo, hi = r, CMP_STRIDE + r
        acc_lo = acc_lo + jnp.dot((xr + pos_ref[0, lo:lo + 1, :]).astype(BF16),
                                  w1_ref[0, lo * HEAD_DIM:(lo + 1) * HEAD_DIM, :],
                                  preferred_element_type=F32)
        acc_hi = acc_hi + jnp.dot((xr + pos_ref[0, hi:hi + 1, :]).astype(BF16),
                                  w1_ref[0, hi * HEAD_DIM:(hi + 1) * HEAD_DIM, :],
                                  preferred_element_type=F32)
    pre = acc_lo + pltpu.roll(acc_hi, n_rows - 1, 0) + b1_ref[0]
    mid = pre * _sigmoid(pre)
    out = jnp.dot(mid.astype(BF16), w2_ref[0], preferred_element_type=F32)
    o_ref[0, 0, 0] = (out * tab_ref[0, 0] + pltpu.roll(out, LANES - ROT_DIM // 2, 1) * tab_ref[0, 1]
                      + pltpu.roll(out, ROT_DIM // 2, 1) * tab_ref[0, 2])


def _compress(proj, w1, b1, w2, pos, tabs, batch, seq):
    n_rows = seq // CMP_STRIDE
    return pl.pallas_call(
        _compress_kernel,
        out_shape=jax.ShapeDtypeStruct((2, batch, NSA_KV_GROUPS, n_rows, HEAD_DIM), F32),
        grid=(2, batch, NSA_KV_GROUPS),
        in_specs=[
            pl.BlockSpec((seq, HEAD_DIM), lambda kv, b, g: (b, BLK_KC + 2 * kv + g)),
            pl.BlockSpec((1, CMP_BLOCK * HEAD_DIM, HEAD_DIM), lambda kv, b, g: (kv, 0, 0)),
            pl.BlockSpec((1, 1, HEAD_DIM), lambda kv, b, g: (kv, 0, 0)),
            pl.BlockSpec((1, HEAD_DIM, HEAD_DIM), lambda kv, b, g: (kv, 0, 0)),
            pl.BlockSpec((1, CMP_BLOCK, HEAD_DIM), lambda kv, b, g: (kv, 0, 0)),
            pl.BlockSpec((1, 3, n_rows, HEAD_DIM), lambda kv, b, g: (kv, 0, 0, 0)),
        ],
        out_specs=pl.BlockSpec((1, 1, 1, n_rows, HEAD_DIM), lambda kv, b, g: (kv, b, g, 0, 0)),
        compiler_params=pltpu.CompilerParams(
            dimension_semantics=("arbitrary", "arbitrary", "arbitrary"), vmem_limit_bytes=VMEM_LIMIT),
        name="compress",
    )(proj, w1, b1, w2, pos, tabs)


def _kvprep_kernel(ks_ref, kw_ref, vs_ref, vw_ref, kaug_ref, vaug_ref, kwb_ref, vwb_ref):
    t = ks_ref.shape[0]
    t0 = pl.program_id(2) * t
    row = lax.broadcasted_iota(jnp.int32, (t, LANES), 0) + t0
    col = lax.broadcasted_iota(jnp.int32, (t, LANES), 1)
    block_lane = lax.shift_right_logical(row, 6) & (LANES - 1)
    kaug_ref[0, 0, :, :HEAD_DIM] = ks_ref[...].astype(BF16)
    kaug_ref[0, 0, :, HEAD_DIM:] = jnp.where(block_lane == col, 1.0, 0.0).astype(BF16)
    vaug_ref[0, 0, :, :HEAD_DIM] = vs_ref[...].astype(BF16)
    vaug_ref[0, 0, :, HEAD_DIM:] = jnp.ones((t, LANES), BF16)
    kwb_ref[0, 0] = kw_ref[...].astype(BF16)
    vwb_ref[0, 0] = vw_ref[...].astype(BF16)


def _kvprep(proj, batch, seq):
    t = min(PREP_T, seq)
    nt = seq // t

    def col(base):
        return pl.BlockSpec((t, HEAD_DIM), lambda b, g, ti: (b * nt + ti, base + g))

    def out(width):
        return pl.BlockSpec((1, 1, t, width), lambda b, g, ti: (b, g, ti, 0))

    def shape(width):
        return jax.ShapeDtypeStruct((batch, NSA_KV_GROUPS, seq, width), BF16)

    return pl.pallas_call(
        _kvprep_kernel,
        out_shape=(shape(2 * HEAD_DIM), shape(2 * HEAD_DIM), shape(HEAD_DIM), shape(HEAD_DIM)),
        grid=(batch, NSA_KV_GROUPS, nt),
        in_specs=[col(BLK_KS), col(BLK_KW), col(BLK_VS), col(BLK_VW)],
        out_specs=(out(2 * HEAD_DIM), out(2 * HEAD_DIM), out(HEAD_DIM), out(HEAD_DIM)),
        compiler_params=pltpu.CompilerParams(
            dimension_semantics=("arbitrary", "arbitrary", "arbitrary"), vmem_limit_bytes=VMEM_LIMIT),
        name="kvprep",
    )(proj, proj, proj, proj)


def _cmpwin_kernel(q_ref, kc_ref, vc_ref, kw_ref, vw_ref, ng_ref, ov_ref, part_ref, selb_ref,
                   val_ref, sel_ref, *, n_blocks, win_len):
    tq = q_ref.shape[0]
    n_cmp = kc_ref.shape[-2]
    nblkp = selb_ref.shape[-1]
    q0 = pl.program_id(2) * tq
    scale = HEAD_DIM ** -0.5

    kc = kc_ref[0, 0, 0].astype(BF16)
    vc = vc_ref[0, 0, 0].astype(BF16)
    t_c = lax.broadcasted_iota(jnp.int32, (tq, n_cmp), 0) + q0
    n_c = lax.broadcasted_iota(jnp.int32, (tq, n_cmp), 1)
    vis_c = n_c * CMP_STRIDE + (CMP_BLOCK - 1) <= t_c

    w0 = pl.multiple_of(jnp.maximum(q0 + tq - win_len, 0), tq)
    kwin = kw_ref[0, 0, pl.ds(w0, win_len), :]
    vwin = vw_ref[0, 0, pl.ds(w0, win_len), :]
    t_w = lax.broadcasted_iota(jnp.int32, (tq, win_len), 0) + q0
    kp_w = lax.broadcasted_iota(jnp.int32, (tq, win_len), 1) + w0
    vis_w = (kp_w <= t_w) & (t_w - kp_w < WINDOW)

    gates = _sigmoid(ng_ref[...])
    p_sum = jnp.zeros((tq, n_cmp), F32)
    for h in range(HEADS_PER_GROUP):
        sl = slice(h * HEAD_DIM, (h + 1) * HEAD_DIM)
        qh = (q_ref[:, sl] * scale).astype(BF16)
        s = jnp.where(vis_c, _nt_dot(qh, kc), NEG_BIG)
        m = jnp.max(s, axis=-1, keepdims=True)
        m = jnp.where(m < 0.5 * NEG_BIG, 0.0, m)
        p = jnp.exp(s - m)
        p = p / jnp.maximum(jnp.sum(p, axis=-1, keepdims=True), 1e-30)
        o_c = jnp.dot(p.astype(BF16), vc, preferred_element_type=F32)
        p_sum = p_sum + p
        s = jnp.where(vis_w, _nt_dot(qh, kwin), NEG_BIG)
        m = jnp.max(s, axis=-1, keepdims=True)
        p = jnp.exp(s - m)
        p = p / jnp.maximum(jnp.sum(p, axis=-1, keepdims=True), 1e-30)
        o_w = jnp.dot(p.astype(BF16), vwin, preferred_element_type=F32)
        g_c = gates[:, 0 * HEADS_PER_GROUP + h:0 * HEADS_PER_GROUP + h + 1]
        g_w = gates[:, 2 * HEADS_PER_GROUP + h:2 * HEADS_PER_GROUP + h + 1]
        part_ref[:, sl] = g_c * o_c + g_w * o_w

    p_hi = p_sum.astype(BF16)
    p_lo = (p_sum - p_hi.astype(F32)).astype(BF16)
    imp = (jnp.dot(p_hi, ov_ref[...], preferred_element_type=F32)
           + jnp.dot(p_lo, ov_ref[...], preferred_element_type=F32))
    t_b = lax.broadcasted_iota(jnp.int32, (tq, nblkp), 0) + q0
    blk = lax.broadcasted_iota(jnp.int32, (tq, nblkp), 1)
    cur = lax.shift_right_logical(t_b, 6)
    forced = (blk == 0) | (blk == cur) | (blk == cur - 1)
    val = jnp.where(forced, FORCE_SCORE, jnp.where(blk * SEL_BLOCK <= t_b, imp, -1.0))
    val_ref[...] = jnp.where(blk < n_blocks, val, -3.0)
    sel_ref[...] = jnp.zeros_like(sel_ref)
    blk_f = blk.astype(F32)

    def pick(_, carry):
        v = val_ref[...]
        m = jnp.max(v, axis=-1, keepdims=True)
        idx = jnp.min(jnp.where(v == m, blk_f, float(nblkp)), axis=-1, keepdims=True)
        hit = blk_f == idx
        sel_ref[...] = jnp.where(hit, 1.0, sel_ref[...])
        val_ref[...] = jnp.where(hit, -4.0, v)
        return carry

    lax.fori_loop(0, min(N_SELECT, n_blocks), pick, 0)
    selb_ref[0, 0] = jnp.where(sel_ref[...] > 0.5, 0.0, NEG_BIG).astype(BF16)


def _cmpwin(proj, cmp_kv, kwb, vwb, ov, batch, seq):
    tq = min(ATT_TQ, seq)
    nq = seq // tq
    n_cmp = seq // CMP_STRIDE
    n_blocks = seq // SEL_BLOCK
    nblkp = ov.shape[1]
    win_len = min(WINDOW + tq, seq)
    q_cols = HEADS_PER_GROUP * HEAD_DIM
    kernel = functools.partial(_cmpwin_kernel, n_blocks=n_blocks, win_len=win_len)
    return pl.pallas_call(
        kernel,
        out_shape=(jax.ShapeDtypeStruct((batch * seq, NSA_HEADS * HEAD_DIM), F32),
                   jax.ShapeDtypeStruct((batch, NSA_KV_GROUPS, seq, nblkp), BF16)),
        grid=(batch, NSA_KV_GROUPS, nq),
        in_specs=[
            pl.BlockSpec((tq, q_cols), lambda b, g, i: (b * nq + i, BLK_NQ * LANES // q_cols + g)),
            pl.BlockSpec((1, 1, 1, n_cmp, HEAD_DIM), lambda b, g, i: (0, b, g, 0, 0)),
            pl.BlockSpec((1, 1, 1, n_cmp, HEAD_DIM), lambda b, g, i: (1, b, g, 0, 0)),
            pl.BlockSpec((1, 1, seq, HEAD_DIM), lambda b, g, i: (b, g, 0, 0)),
            pl.BlockSpec((1, 1, seq, HEAD_DIM), lambda b, g, i: (b, g, 0, 0)),
            pl.BlockSpec((tq, LANES), lambda b, g, i: (b * nq + i, BLK_NG + g)),
            pl.BlockSpec((n_cmp, nblkp), lambda b, g, i: (0, 0)),
        ],
        out_specs=(pl.BlockSpec((tq, q_cols), lambda b, g, i: (b * nq + i, g)),
                   pl.BlockSpec((1, 1, tq, nblkp), lambda b, g, i: (b, g, i, 0))),
        scratch_shapes=[pltpu.VMEM((tq, nblkp), F32), pltpu.VMEM((tq, nblkp), F32)],
        compiler_params=pltpu.CompilerParams(
            dimension_semantics=("arbitrary", "arbitrary", "arbitrary"), vmem_limit_bytes=VMEM_LIMIT),
        name="cmpwin",
    )(proj, cmp_kv, cmp_kv, kwb, vwb, proj, ov)


def _select_kernel(q_ref, selb_ref, kaug_ref, vaug_ref, part_ref, ng_ref, nz_ref, y_ref,
                   qaug_ref, acc_ref, m_ref, p_ref, alpha_ref, *, tk):
    tq = q_ref.shape[0]
    n_halves = selb_ref.shape[-1] // LANES
    q0 = pl.program_id(2) * tq
    scale = HEAD_DIM ** -0.5

    for h in range(HEADS_PER_GROUP):
        qh = (q_ref[:, h * HEAD_DIM:(h + 1) * HEAD_DIM] * scale).astype(BF16)
        for half in range(n_halves):
            qaug_ref[half * HEADS_PER_GROUP + h, :, :HEAD_DIM] = qh
            qaug_ref[half * HEADS_PER_GROUP + h, :, HEAD_DIM:] = selb_ref[0, 0, :, half * LANES:(half + 1) * LANES]
    acc_ref[...] = jnp.zeros_like(acc_ref)
    m_ref[...] = jnp.full_like(m_ref, NEG_BIG)

    def tile_step(kt, pending, causal):
        if kt is not None:
            k0 = pl.multiple_of(kt * tk, tk)
            k_t = kaug_ref[0, 0, pl.ds(k0, tk), :]
            half = k0 // (SEL_BLOCK * LANES)
            if causal:
                t_pos = lax.broadcasted_iota(jnp.int32, (tq, tk), 0) + q0
                k_pos = lax.broadcasted_iota(jnp.int32, (tq, tk), 1) + k0
                vis = k_pos <= t_pos
        if pending is not None:
            v_t = vaug_ref[0, 0, pl.ds(pl.multiple_of(pending * tk, tk), tk), :]
        if kt is not None:
            s_next = _nt_dot(qaug_ref[half * HEADS_PER_GROUP], k_t)
        for h in range(HEADS_PER_GROUP):
            if kt is not None:
                s = s_next
                if h + 1 < HEADS_PER_GROUP:
                    s_next = _nt_dot(qaug_ref[half * HEADS_PER_GROUP + h + 1], k_t)
            if pending is not None:
                alpha = alpha_ref[h]
                acc_ref[h] = (acc_ref[h] * jnp.concatenate([alpha, alpha], axis=1)
                              + jnp.dot(p_ref[h], v_t, preferred_element_type=F32))
            if kt is not None:
                if causal:
                    s = jnp.where(vis, s, NEG_BIG)
                m_prev = m_ref[h]
                m_new = jnp.maximum(m_prev, jnp.max(s, axis=-1, keepdims=True))
                p_ref[h] = jnp.exp(s - jnp.concatenate([m_new] * (tk // LANES), axis=1)).astype(BF16)
                alpha_ref[h] = jnp.exp(m_prev - m_new)
                m_ref[h] = m_new

    n_full = q0 // tk

    @pl.when(n_full > 0)
    def _():
        tile_step(0, None, False)

    @pl.when(n_full == 0)
    def _():
        p_ref[...] = jnp.zeros_like(p_ref)
        alpha_ref[...] = jnp.ones_like(alpha_ref)

    def step(kt, carry):
        tile_step(kt, kt - 1, False)
        return carry

    lax.fori_loop(1, n_full, step, 0)
    tile_step(n_full, jnp.maximum(n_full - 1, 0), True)
    tile_step(None, n_full, False)

    gates = _sigmoid(ng_ref[...])
    for h in range(HEADS_PER_GROUP):
        sl = slice(h * HEAD_DIM, (h + 1) * HEAD_DIM)
        acc = acc_ref[h]
        o_s = acc[:, :HEAD_DIM] / jnp.maximum(acc[:, HEAD_DIM:], 1e-30)
        g_s = gates[:, 1 * HEADS_PER_GROUP + h:1 * HEADS_PER_GROUP + h + 1]
        z = nz_ref[:, sl]
        y_ref[:, sl] = ((part_ref[:, sl] + g_s * o_s) * (z * _sigmoid(z))).astype(BF16)


def _select(proj, selb, kaug, vaug, part, batch, seq):
    tq = min(SEL_TQ, seq)
    tk = min(SEL_TK, seq)
    nq = seq // tq
    nblkp = selb.shape[-1]
    q_cols = HEADS_PER_GROUP * HEAD_DIM
    kernel = functools.partial(_select_kernel, tk=tk)
    return pl.pallas_call(
        kernel,
        out_shape=jax.ShapeDtypeStruct((batch * seq, NSA_HEADS * HEAD_DIM), BF16),
        grid=(batch, NSA_KV_GROUPS, nq),
        in_specs=[
            pl.BlockSpec((tq, q_cols), lambda b, g, i: (b * nq + i, BLK_NQ * LANES // q_cols + g)),
            pl.BlockSpec((1, 1, tq, nblkp), lambda b, g, i: (b, g, i, 0)),
            pl.BlockSpec((1, 1, seq, 2 * HEAD_DIM), lambda b, g, i: (b, g, 0, 0), pipeline_mode=pl.Buffered(1)),
            pl.BlockSpec((1, 1, seq, 2 * HEAD_DIM), lambda b, g, i: (b, g, 0, 0), pipeline_mode=pl.Buffered(1)),
            pl.BlockSpec((tq, q_cols), lambda b, g, i: (b * nq + i, g)),
            pl.BlockSpec((tq, LANES), lambda b, g, i: (b * nq + i, BLK_NG + g)),
            pl.BlockSpec((tq, q_cols), lambda b, g, i: (b * nq + i, BLK_NZ * LANES // q_cols + g)),
        ],
        out_specs=pl.BlockSpec((tq, q_cols), lambda b, g, i: (b * nq + i, g)),
        scratch_shapes=[
            pltpu.VMEM((nblkp // LANES * HEADS_PER_GROUP, tq, 2 * HEAD_DIM), BF16),
            pltpu.VMEM((HEADS_PER_GROUP, tq, 2 * HEAD_DIM), F32),
            pltpu.VMEM((HEADS_PER_GROUP, tq, LANES), F32),
            pltpu.VMEM((HEADS_PER_GROUP, tq, tk), BF16),
            pltpu.VMEM((HEADS_PER_GROUP, tq, LANES), F32),
        ],
        compiler_params=pltpu.CompilerParams(
            dimension_semantics=("arbitrary", "arbitrary", "arbitrary"), vmem_limit_bytes=VMEM_LIMIT),
        name="select",
    )(proj, selb, kaug, vaug, part, proj, proj)


def _outproj_kernel(yh_ref, yn_ref, w_ref, x_ref, g_ref, o_ref):
    half = yh_ref.shape[1]
    z = (jnp.dot(yh_ref[...], w_ref[:half, :], preferred_element_type=F32)
         + jnp.dot(yn_ref[...], w_ref[half:, :], preferred_element_type=F32))
    ms = jnp.mean(z * z, axis=-1, keepdims=True)
    o_ref[...] = x_ref[...] + z * lax.rsqrt(ms + NORM_EPS) * g_ref[...]


def _outproj(yh, yn, w, x2, g):
    n = x2.shape[0]
    tm = min(OUT_TM, n)
    return pl.pallas_call(
        _outproj_kernel,
        out_shape=jax.ShapeDtypeStruct((n, D_MODEL), F32),
        grid=(n // tm,),
        in_specs=[
            pl.BlockSpec((tm, yh.shape[1]), lambda i: (i, 0)),
            pl.BlockSpec((tm, yn.shape[1]), lambda i: (i, 0)),
            pl.BlockSpec(w.shape, lambda i: (0, 0)),
            pl.BlockSpec((tm, D_MODEL), lambda i: (i, 0)),
            pl.BlockSpec((1, D_MODEL), lambda i: (0, 0)),
        ],
        out_specs=pl.BlockSpec((tm, D_MODEL), lambda i: (i, 0)),
        compiler_params=pltpu.CompilerParams(
            dimension_semantics=("arbitrary",), vmem_limit_bytes=VMEM_LIMIT),
        name="outproj",
    )(yh, yn, w, x2, g)


def _proj_column_map():
    idx = np.full((PROJ_COLS,), -1, np.int64)

    def put(dst_blk, src, width):
        idx[dst_blk * LANES:dst_blk * LANES + width] = np.arange(src, src + width)

    for name, blk, width in (("hq", BLK_HQ, 1024), ("hf", BLK_HF, 1024), ("hi", BLK_HI, 1024),
                             ("hz", BLK_HZ, 1024), ("nq", BLK_NQ, 1024), ("ks", BLK_KS, 256),
                             ("kw", BLK_KW, 256), ("kc", BLK_KC, 256), ("vc", BLK_VC, 256),
                             ("vs", BLK_VS, 256), ("vw", BLK_VW, 256), ("nz", BLK_NZ, 1024)):
        put(blk, _SRC[name], width)
    for g in range(NSA_KV_GROUPS):
        for br in range(N_BRANCHES):
            for h in range(HEADS_PER_GROUP):
                idx[(BLK_NG + g) * LANES + br * HEADS_PER_GROUP + h] = (
                    _SRC["ng"] + (g * HEADS_PER_GROUP + h) * N_BRANCHES + br)
    return idx


def _rope_tables(pos):
    half = ROT_DIM // 2
    inv = ROPE_THETA ** (-2.0 * jnp.arange(half, dtype=F32) / ROT_DIM)
    ang = pos.astype(F32)[:, None] * inv[None, :]
    cos, sin = jnp.cos(ang), jnp.sin(ang)
    n = pos.shape[0]
    pad = jnp.zeros((n, HEAD_DIM - ROT_DIM), F32)
    zero = jnp.zeros((n, half), F32)
    c = jnp.concatenate([cos, cos, pad + 1.0], axis=1)
    s1 = jnp.concatenate([-sin, zero, pad], axis=1)
    s2 = jnp.concatenate([zero, sin, pad], axis=1)
    return c, s1, s2


def _overlap_matrix(n_cmp, n_blocks, nblkp):
    start = np.arange(n_cmp)[:, None] * CMP_STRIDE
    sel = np.arange(nblkp)[None, :] * SEL_BLOCK
    ov = (start < sel + SEL_BLOCK) & (start + CMP_BLOCK > sel)
    ov &= np.arange(nblkp)[None, :] < n_blocks
    ov &= np.arange(n_cmp)[:, None] < (n_cmp - 1)
    return jnp.asarray(ov, BF16)


def kernel(x, pre_norm, post_norm, w_in, hgrn_lb_logits, hgrn_out_norm, cmp_pos_k, cmp_w1_k, cmp_b1_k,
           cmp_w2_k, cmp_pos_v, cmp_w1_v, cmp_b1_v, cmp_w2_v, w_out):
    batch, seq, _ = x.shape
    depth = w_in.shape[0]
    n_cmp = seq // CMP_STRIDE
    n_blocks = seq // SEL_BLOCK
    nblkp = -(-n_blocks // LANES) * LANES

    lb_probs = jax.nn.softmax(hgrn_lb_logits.astype(F32), axis=0)
    lower = jnp.maximum(jnp.cumsum(lb_probs, axis=0) - lb_probs[0:1], 0.0)
    lower = lower.reshape(depth, HGRN_HEADS, 1, HEAD_DIM)
    g_out = jnp.broadcast_to(hgrn_out_norm.astype(F32)[:, None, None, :], lower.shape)
    hgrn_par = jnp.concatenate(
        [jnp.log(lower), jnp.log1p(-lower), 1.0 - lower, g_out, jnp.zeros((depth, HGRN_HEADS, 4, HEAD_DIM), F32)],
        axis=2)

    col_map = _proj_column_map()
    col_src = jnp.asarray(np.maximum(col_map, 0), jnp.int32)
    col_ok = jnp.asarray(col_map >= 0)

    rope_c, rope_s1, rope_s2 = _rope_tables(jnp.arange(seq))
    cmp_tab_k = jnp.stack(_rope_tables(jnp.arange(n_cmp) * CMP_STRIDE + CMP_BLOCK - 1))
    cmp_tab_v = jnp.stack([jnp.ones((n_cmp, HEAD_DIM), F32), jnp.zeros((n_cmp, HEAD_DIM), F32),
                           jnp.zeros((n_cmp, HEAD_DIM), F32)])
    cmp_tabs = jnp.stack([cmp_tab_k, cmp_tab_v])
    ov = _overlap_matrix(n_cmp, n_blocks, nblkp)
    tri = jnp.asarray(np.tril(np.ones((HGRN_CHUNK, HGRN_CHUNK), np.float32)))
    ones = jnp.ones((HEAD_DIM, HEAD_DIM), BF16)

    x2 = x.reshape(batch * seq, D_MODEL)
    for layer in range(depth):
        w_perm = jnp.where(col_ok[None, :], jnp.take(w_in[layer], col_src, axis=1), 0.0).astype(BF16)
        proj = _proj(x2, pre_norm[layer][None, :], w_perm, rope_c, rope_s1, rope_s2, seq)
        y_h = _hgrn(proj, hgrn_par[layer], tri, ones, batch, seq)
        cmp_kv = _compress(
            proj,
            jnp.stack([cmp_w1_k[layer], cmp_w1_v[layer]]).astype(BF16),
            jnp.stack([cmp_b1_k[layer], cmp_b1_v[layer]])[:, None, :],
            jnp.stack([cmp_w2_k[layer], cmp_w2_v[layer]]).astype(BF16),
            jnp.stack([cmp_pos_k[layer], cmp_pos_v[layer]]),
            cmp_tabs, batch, seq)
        kaug, vaug, kwb, vwb = _kvprep(proj, batch, seq)
        part, selb = _cmpwin(proj, cmp_kv, kwb, vwb, ov, batch, seq)
        y_n = _select(proj, selb, kaug, vaug, part, batch, seq)
        x2 = _outproj(y_h, y_n, w_out[layer].astype(BF16), x2, post_norm[layer][None, :])
    return x2.reshape(batch, seq, D_MODEL)
```

```python
import functools

import numpy as np
import jax
import jax.numpy as jnp
from jax import lax
from jax.experimental import pallas as pl
from jax.experimental.pallas import tpu as pltpu

F32 = jnp.float32
BF16 = jnp.bfloat16

D_MODEL = 1024
HEAD_DIM = 128
HGRN_HEADS = 8
NSA_HEADS = 8
NSA_KV_GROUPS = 2
HEADS_PER_GROUP = NSA_HEADS // NSA_KV_GROUPS
N_BRANCHES = 3
HGRN_CHUNK = 64
HGRN_SUB = 8
CMP_BLOCK = 32
CMP_STRIDE = 16
SEL_BLOCK = 64
N_SELECT = 16
WINDOW = 512
ROPE_THETA = 500000.0
ROT_DIM = HEAD_DIM // 4
NORM_EPS = 1e-6
FORCE_SCORE = 1e4
NEG_BIG = -1e30
LOG2_E = 1.4426950408889634

LANES = 128
BLK_HQ, BLK_HF, BLK_HI, BLK_HZ = 0, 8, 16, 24
BLK_NQ, BLK_KS, BLK_KW, BLK_KC, BLK_VC, BLK_VS, BLK_VW = 32, 40, 42, 44, 46, 48, 50
BLK_NZ, BLK_NG = 52, 60
PROJ_BLOCKS = 64
PROJ_COLS = PROJ_BLOCKS * LANES
_SRC = dict(hq=0, hf=1024, hi=2048, hz=3072, nq=4096, kc=5120, vc=5376, ks=5632, vs=5888,
            kw=6144, vw=6400, ng=6656, nz=6680)

PROJ_TM = 1024
PROJ_TN = 1024
ROPE_FULL_TILE = BLK_NQ * LANES // PROJ_TN
ROPE_HEAD_TILE = BLK_KS * LANES // PROJ_TN
ROPE_HEAD_BLOCKS = 4
HGRN_T = 512
PREP_T = 1024
ATT_TQ = 256
CMP_KEY_CHUNK = 256
SEL_TQ = 512
SEL_TK = 1024
OUT_TM = 512
VMEM_LIMIT = 56 * 1024 * 1024


def _nt_dot(a, b):
    return lax.dot_general(a, b, (((1,), (1,)), ((), ())), preferred_element_type=F32)


def _sigmoid(x):
    return 1.0 / (1.0 + jnp.exp(-x))


def _proj_kernel(x_ref, g_ref, w_ref, c_ref, s1_ref, s2_ref, o_ref, h_ref):
    j = pl.program_id(1)

    @pl.when(j == 0)
    def _():
        x = x_ref[...]
        ms = jnp.mean(x * x, axis=-1, keepdims=True)
        h_ref[...] = (x * lax.rsqrt(ms + NORM_EPS) * g_ref[...]).astype(BF16)

    o_ref[...] = jnp.dot(h_ref[...], w_ref[...], preferred_element_type=F32)

    def rope_blocks(n_blocks):
        for blk in range(n_blocks):
            sl = slice(blk * LANES, (blk + 1) * LANES)
            a = o_ref[:, sl]
            o_ref[:, sl] = (a * c_ref[...] + pltpu.roll(a, LANES - ROT_DIM // 2, 1) * s1_ref[...]
                            + pltpu.roll(a, ROT_DIM // 2, 1) * s2_ref[...])

    @pl.when(j == ROPE_FULL_TILE)
    def _():
        rope_blocks(PROJ_TN // LANES)

    @pl.when(j == ROPE_HEAD_TILE)
    def _():
        rope_blocks(ROPE_HEAD_BLOCKS)


def _proj(x2, g, w, rope_c, rope_s1, rope_s2, seq):
    n = x2.shape[0]
    tm = min(PROJ_TM, seq)
    pos_tiles = seq // tm
    return pl.pallas_call(
        _proj_kernel,
        out_shape=jax.ShapeDtypeStruct((n, PROJ_COLS), F32),
        grid=(n // tm, PROJ_COLS // PROJ_TN),
        in_specs=[
            pl.BlockSpec((tm, D_MODEL), lambda i, j: (i, 0)),
            pl.BlockSpec((1, D_MODEL), lambda i, j: (0, 0)),
            pl.BlockSpec((D_MODEL, PROJ_TN), lambda i, j: (0, j)),
            pl.BlockSpec((tm, LANES), lambda i, j: (i % pos_tiles, 0)),
            pl.BlockSpec((tm, LANES), lambda i, j: (i % pos_tiles, 0)),
            pl.BlockSpec((tm, LANES), lambda i, j: (i % pos_tiles, 0)),
        ],
        out_specs=pl.BlockSpec((tm, PROJ_TN), lambda i, j: (i, j)),
        scratch_shapes=[pltpu.VMEM((tm, D_MODEL), BF16)],
        compiler_params=pltpu.CompilerParams(
            dimension_semantics=("arbitrary", "arbitrary"), vmem_limit_bytes=VMEM_LIMIT),
        name="proj",
    )(x2, g, w, rope_c, rope_s1, rope_s2)


def _hgrn_kernel(q_ref, f_ref, i_ref, z_ref, par_ref, tri_ref, ones_ref, y_ref, st_ref, o_ref):
    @pl.when(pl.program_id(2) == 0)
    def _():
        st_ref[...] = jnp.zeros_like(st_ref)

    t = q_ref.shape[0]
    nc = t // HGRN_CHUNK
    shape3 = (nc, HGRN_CHUNK, HEAD_DIM)
    par = par_ref[0]
    log_lb, log_1m_lb, one_m_lb, g_norm = par[0:1], par[1:2], par[2:3], par[3:4]
    x = f_ref[...]
    e = jnp.exp(-jnp.abs(x))
    c = log_1m_lb + jnp.minimum(x, 0.0) - jnp.log(1.0 + e)
    lf = jnp.maximum(log_lb, c) + jnp.log(1.0 + jnp.exp(-jnp.abs(log_lb - c)))
    k3 = (one_m_lb * jnp.where(x >= 0.0, e, 1.0) / (1.0 + e)).reshape(shape3)
    q3 = q_ref[...].reshape(shape3)
    v3 = i_ref[...].reshape(shape3)
    v_bf = v3.astype(BF16)

    lf3 = lf.reshape(shape3)
    b3 = jnp.stack([jnp.dot(tri_ref[...], lf3[ci], precision=lax.Precision.HIGHEST,
                            preferred_element_type=F32) for ci in range(nc)])
    row = lax.broadcasted_iota(jnp.int32, shape3, 1)
    t_idx = lax.broadcasted_iota(jnp.int32, (nc, HGRN_CHUNK, HGRN_CHUNK), 1)
    s_idx = lax.broadcasted_iota(jnp.int32, (nc, HGRN_CHUNK, HGRN_CHUNK), 2)

    attn = jnp.zeros((nc, HGRN_CHUNK, HGRN_CHUNK), F32)
    n = HGRN_CHUNK // 2
    while n >= HGRN_SUB:
        blocks = (nc * HGRN_CHUNK // (2 * n), 2 * n, HEAD_DIM)
        b_blk = b3.reshape(blocks)
        ref = jnp.broadcast_to(b_blk[:, n - 1:n, :], blocks).reshape(shape3)
        upper = (row & n) != 0
        q_n = jnp.where(upper, q3 * jnp.exp(jnp.where(upper, b3 - ref, 0.0)), 0.0).astype(BF16)
        k_n = jnp.where(upper, 0.0, k3 * jnp.exp(jnp.where(upper, 0.0, ref - b3))).astype(BF16)
        a_n = jnp.stack([_nt_dot(q_n[ci], k_n[ci]) for ci in range(nc)])
        if 2 * n < HGRN_CHUNK:
            same = (t_idx & -(2 * n)) == (s_idx & -(2 * n))
            a_n = jnp.where(same, a_n, 0.0)
        attn = attn + a_n
        n //= 2
    attn_bf = attn.astype(BF16)
    o_off = jnp.stack([jnp.dot(attn_bf[ci], v_bf[ci], preferred_element_type=F32) for ci in range(nc)])

    subs = (nc * HGRN_CHUNK // HGRN_SUB, HGRN_SUB, HEAD_DIM)
    sub_row = lax.broadcasted_iota(jnp.int32, subs, 1)
    b4, q4, k4, v4 = b3.reshape(subs), q3.reshape(subs), k3.reshape(subs), v3.reshape(subs)
    diag = jnp.zeros(subs, F32)
    for s in range(HGRN_SUB):
        d = jnp.where(sub_row >= s, b4 - b4[:, s:s + 1, :], NEG_BIG)
        p = jnp.exp(d) * q4 * k4[:, s:s + 1, :]
        r = jnp.dot(p.reshape(t, HEAD_DIM).astype(BF16), ones_ref[...], preferred_element_type=F32)
        diag = diag + r.reshape(subs) * v4[:, s:s + 1, :]
    o_intra = o_off + diag.reshape(shape3)

    b_last = b3[:, HGRN_CHUNK - 1:HGRN_CHUNK, :]
    decay = jnp.exp(b_last)
    q_dec = (q3 * jnp.exp(b3)).astype(BF16)
    k_dec = (k3 * jnp.exp(b_last - b3)).astype(BF16)
    upd = [lax.dot_general(v_bf[ci], k_dec[ci], (((0,), (0,)), ((), ())), preferred_element_type=F32)
           for ci in range(nc)]
    st = st_ref[...]
    for ci in range(nc):
        o_ref[ci * HGRN_CHUNK:(ci + 1) * HGRN_CHUNK, :] = o_intra[ci] + _nt_dot(q_dec[ci], st.astype(BF16))
        st = st * decay[ci] + upd[ci]
    st_ref[...] = st

    o = o_ref[...]
    ms = jnp.mean(o * o, axis=-1, keepdims=True)
    z = z_ref[...]
    y_ref[...] = (o * lax.rsqrt(ms + NORM_EPS) * g_norm * (z * _sigmoid(z))).astype(BF16)


def _hgrn(proj, par, tri, ones, batch, seq):
    t = min(HGRN_T, seq)
    nt = seq // t

    def col(base):
        return pl.BlockSpec((t, HEAD_DIM), lambda b, h, ti: (b * nt + ti, base + h))

    return pl.pallas_call(
        _hgrn_kernel,
        out_shape=jax.ShapeDtypeStruct((batch * seq, HGRN_HEADS * HEAD_DIM), BF16),
        grid=(batch, HGRN_HEADS, nt),
        in_specs=[
            col(BLK_HQ), col(BLK_HF), col(BLK_HI), col(BLK_HZ),
            pl.BlockSpec((1, 8, HEAD_DIM), lambda b, h, ti: (h, 0, 0)),
            pl.BlockSpec((HGRN_CHUNK, HGRN_CHUNK), lambda b, h, ti: (0, 0)),
            pl.BlockSpec((HEAD_DIM, HEAD_DIM), lambda b, h, ti: (0, 0)),
        ],
        out_specs=pl.BlockSpec((t, HEAD_DIM), lambda b, h, ti: (b * nt + ti, h)),
        scratch_shapes=[pltpu.VMEM((HEAD_DIM, HEAD_DIM), F32), pltpu.VMEM((t, HEAD_DIM), F32)],
        compiler_params=pltpu.CompilerParams(
            dimension_semantics=("arbitrary", "arbitrary", "arbitrary"), vmem_limit_bytes=VMEM_LIMIT),
        name="hgrn",
    )(proj, proj, proj, proj, par, tri, ones)


def _compress_kernel(x_ref, w1_ref, b1_ref, w2_ref, pos_ref, tab_ref, o_ref):
    n_rows = o_ref.shape[-2]
    acc_lo = jnp.zeros((n_rows, HEAD_DIM), F32)
    acc_hi = jnp.zeros((n_rows, HEAD_DIM), F32)
    for r in range(CMP_STRIDE):
        xr = x_ref[pl.ds(r, n_rows, stride=CMP_STRIDE), :]
        lo, hi = r, CMP_STRIDE + r
        acc_lo = acc_lo + jnp.dot((xr + pos_ref[0, lo:lo + 1, :]).astype(BF16),
                                  w1_ref[0, lo * HEAD_DIM:(lo + 1) * HEAD_DIM, :],
                                  preferred_element_type=F32)
        acc_hi = acc_hi + jnp.dot((xr + pos_ref[0, hi:hi + 1, :]).astype(BF16),
                                  w1_ref[0, hi * HEAD_DIM:(hi + 1) * HEAD_DIM, :],
                                  preferred_element_type=F32)
    pre = acc_lo + pltpu.roll(acc_hi, n_rows - 1, 0) + b1_ref[0]
    mid = pre * _sigmoid(pre)
    out = jnp.dot(mid.astype(BF16), w2_ref[0], preferred_element_type=F32)
    o_ref[0, 0, 0] = (out * tab_ref[0, 0] + pltpu.roll(out, LANES - ROT_DIM // 2, 1) * tab_ref[0, 1]
                      + pltpu.roll(out, ROT_DIM // 2, 1) * tab_ref[0, 2]).astype(BF16)


def _compress(proj, w1, b1, w2, pos, tabs, batch, seq):
    n_rows = seq // CMP_STRIDE
    return pl.pallas_call(
        _compress_kernel,
        out_shape=jax.ShapeDtypeStruct((2, batch, NSA_KV_GROUPS, n_rows, HEAD_DIM), BF16),
        grid=(2, batch, NSA_KV_GROUPS),
        in_specs=[
            pl.BlockSpec((seq, HEAD_DIM), lambda kv, b, g: (b, BLK_KC + 2 * kv + g)),
            pl.BlockSpec((1, CMP_BLOCK * HEAD_DIM, HEAD_DIM), lambda kv, b, g: (kv, 0, 0)),
            pl.BlockSpec((1, 1, HEAD_DIM), lambda kv, b, g: (kv, 0, 0)),
            pl.BlockSpec((1, HEAD_DIM, HEAD_DIM), lambda kv, b, g: (kv, 0, 0)),
            pl.BlockSpec((1, CMP_BLOCK, HEAD_DIM), lambda kv, b, g: (kv, 0, 0)),
            pl.BlockSpec((1, 3, n_rows, HEAD_DIM), lambda kv, b, g: (kv, 0, 0, 0)),
        ],
        out_specs=pl.BlockSpec((1, 1, 1, n_rows, HEAD_DIM), lambda kv, b, g: (kv, b, g, 0, 0)),
        compiler_params=pltpu.CompilerParams(
            dimension_semantics=("arbitrary", "arbitrary", "arbitrary"), vmem_limit_bytes=VMEM_LIMIT),
        name="compress",
    )(proj, w1, b1, w2, pos, tabs)


def _kvprep_kernel(ks_ref, kw_ref, vs_ref, vw_ref, kaug_ref, vaug_ref, kwb_ref, vwb_ref):
    t = ks_ref.shape[0]
    t0 = pl.program_id(2) * t
    row = lax.broadcasted_iota(jnp.int32, (t, LANES), 0) + t0
    col = lax.broadcasted_iota(jnp.int32, (t, LANES), 1)
    block_lane = lax.shift_right_logical(row, 6) & (LANES - 1)
    kaug_ref[0, 0, :, :HEAD_DIM] = ks_ref[...].astype(BF16)
    kaug_ref[0, 0, :, HEAD_DIM:] = jnp.where(block_lane == col, 1.0, 0.0).astype(BF16)
    vaug_ref[0, 0, :, :HEAD_DIM] = vs_ref[...].astype(BF16)
    vaug_ref[0, 0, :, HEAD_DIM:] = jnp.ones((t, LANES), BF16)
    kwb_ref[0, 0] = kw_ref[...].astype(BF16)
    vwb_ref[0, 0] = vw_ref[...].astype(BF16)


def _kvprep(proj, batch, seq):
    t = min(PREP_T, seq)
    nt = seq // t

    def col(base):
        return pl.BlockSpec((t, HEAD_DIM), lambda b, g, ti: (b * nt + ti, base + g))

    def out(width):
        return pl.BlockSpec((1, 1, t, width), lambda b, g, ti: (b, g, ti, 0))

    def shape(width):
        return jax.ShapeDtypeStruct((batch, NSA_KV_GROUPS, seq, width), BF16)

    return pl.pallas_call(
        _kvprep_kernel,
        out_shape=(shape(2 * HEAD_DIM), shape(2 * HEAD_DIM), shape(HEAD_DIM), shape(HEAD_DIM)),
        grid=(batch, NSA_KV_GROUPS, nt),
        in_specs=[col(BLK_KS), col(BLK_KW), col(BLK_VS), col(BLK_VW)],
        out_specs=(out(2 * HEAD_DIM), out(2 * HEAD_DIM), out(HEAD_DIM), out(HEAD_DIM)),
        compiler_params=pltpu.CompilerParams(
            dimension_semantics=("arbitrary", "arbitrary", "arbitrary"), vmem_limit_bytes=VMEM_LIMIT),
        name="kvprep",
    )(proj, proj, proj, proj)


def _cmpwin_kernel(q_ref, kc_ref, vc_ref, kw_ref, vw_ref, ng_ref, ov_ref, part_ref, selb_ref,
                   imp_ref, val_ref, sel_ref, *, n_blocks, win_len, key_chunk):
    tq = q_ref.shape[0]
    n_cmp = kc_ref.shape[-2]
    nblkp = selb_ref.shape[-1]
    q0 = pl.program_id(2) * tq
    gates = _sigmoid(ng_ref[...])
    q_heads = [(q_ref[:, h * HEAD_DIM:(h + 1) * HEAD_DIM] * (HEAD_DIM ** -0.5 * LOG2_E)).astype(BF16)
               for h in range(HEADS_PER_GROUP)]

    def attend(qh, keys, values_ones, vis, all_masked_possible):
        s = jnp.where(vis, _nt_dot(qh, keys), NEG_BIG)
        m = jnp.max(s, axis=-1, keepdims=True)
        if all_masked_possible:
            m = jnp.where(m < 0.5 * NEG_BIG, 0.0, m)
        p = jnp.exp2(s - m)
        o = jnp.dot(p.astype(BF16), values_ones, preferred_element_type=F32)
        inv = 1.0 / jnp.maximum(o[:, HEAD_DIM:], 1e-30)
        return o[:, :HEAD_DIM] * inv, p, inv

    w0 = pl.multiple_of(jnp.maximum(q0 + tq - win_len, 0), tq)
    kwin = kw_ref[0, 0, pl.ds(w0, win_len), :]
    vwin = jnp.concatenate([vw_ref[0, 0, pl.ds(w0, win_len), :], jnp.ones((win_len, HEAD_DIM), BF16)], axis=1)
    t_w = lax.broadcasted_iota(jnp.int32, (tq, win_len), 0) + q0
    kp_w = lax.broadcasted_iota(jnp.int32, (tq, win_len), 1) + w0
    vis_w = (kp_w <= t_w) & (t_w - kp_w < WINDOW)
    for h in range(HEADS_PER_GROUP):
        o_w, _, _ = attend(q_heads[h], kwin, vwin, vis_w, False)
        g_w = gates[:, 2 * HEADS_PER_GROUP + h:2 * HEADS_PER_GROUP + h + 1]
        part_ref[:, h * HEAD_DIM:(h + 1) * HEAD_DIM] = g_w * o_w

    def compressed(n_keys):
        kc = kc_ref[0, 0, 0, :n_keys, :]
        vc = jnp.concatenate([vc_ref[0, 0, 0, :n_keys, :], jnp.ones((n_keys, HEAD_DIM), BF16)], axis=1)
        t_c = lax.broadcasted_iota(jnp.int32, (tq, n_keys), 0) + q0
        n_c = lax.broadcasted_iota(jnp.int32, (tq, n_keys), 1)
        vis_c = n_c * CMP_STRIDE + (CMP_BLOCK - 1) <= t_c
        p_sum = jnp.zeros((tq, n_keys), F32)
        for h in range(HEADS_PER_GROUP):
            sl = slice(h * HEAD_DIM, (h + 1) * HEAD_DIM)
            o_c, p, inv = attend(q_heads[h], kc, vc, vis_c, True)
            p_sum = p_sum + p * jnp.concatenate([inv] * (n_keys // LANES), axis=1)
            g_c = gates[:, 0 * HEADS_PER_GROUP + h:0 * HEADS_PER_GROUP + h + 1]
            part_ref[:, sl] = part_ref[:, sl] + g_c * o_c
        p_hi = p_sum.astype(BF16)
        p_lo = (p_sum - p_hi.astype(F32)).astype(BF16)
        imp_ref[...] = (jnp.dot(p_hi, ov_ref[:n_keys, :], preferred_element_type=F32)
                        + jnp.dot(p_lo, ov_ref[:n_keys, :], preferred_element_type=F32))

    n_visible = (q0 + tq - CMP_BLOCK) // CMP_STRIDE + 1
    n_chunks = jnp.minimum((n_visible + key_chunk - 1) // key_chunk, n_cmp // key_chunk)
    for c in range(1, n_cmp // key_chunk + 1):
        @pl.when(n_chunks == c)
        def _():
            compressed(c * key_chunk)

    imp_t = imp_ref[...].T
    t_b = lax.broadcasted_iota(jnp.int32, (nblkp, tq), 1) + q0
    blk = lax.broadcasted_iota(jnp.int32, (nblkp, tq), 0)
    cur = lax.shift_right_logical(t_b, 6)
    forced = (blk == 0) | (blk == cur) | (blk == cur - 1)
    val = jnp.where(forced, -4.0, jnp.where(blk * SEL_BLOCK <= t_b, imp_t, -1.0))
    val_ref[...] = jnp.where(blk < n_blocks, val, -3.0)
    sel_ref[...] = jnp.where(forced, 1.0, 0.0)
    blk_f = blk.astype(F32)
    n_top = min(N_SELECT, n_blocks)
    quota = n_top - 1 - jnp.minimum(cur[0:1, :], 2)

    def pick(it, carry):
        v = val_ref[...]
        m = jnp.max(v, axis=0, keepdims=True)
        idx = jnp.min(jnp.where(v == m, blk_f, float(nblkp)), axis=0, keepdims=True)
        hit = (blk_f == idx) & (it < quota)
        sel_ref[...] = jnp.where(hit, 1.0, sel_ref[...])
        val_ref[...] = jnp.where(hit, -4.0, v)
        return carry

    lax.fori_loop(0, n_top - 1 - jnp.minimum(lax.shift_right_logical(q0, 6), 2), pick, 0)
    selb_ref[0, 0] = jnp.where(sel_ref[...].T > 0.5, 0.0, NEG_BIG).astype(BF16)


def _cmpwin(proj, cmp_kv, kwb, vwb, ov, batch, seq):
    tq = min(ATT_TQ, seq)
    nq = seq // tq
    n_cmp = seq // CMP_STRIDE
    n_blocks = seq // SEL_BLOCK
    nblkp = ov.shape[1]
    win_len = min(WINDOW + tq, seq)
    q_cols = HEADS_PER_GROUP * HEAD_DIM
    key_chunk = min(CMP_KEY_CHUNK, n_cmp)
    kernel = functools.partial(_cmpwin_kernel, n_blocks=n_blocks, win_len=win_len, key_chunk=key_chunk)
    return pl.pallas_call(
        kernel,
        out_shape=(jax.ShapeDtypeStruct((batch * seq, NSA_HEADS * HEAD_DIM), F32),
                   jax.ShapeDtypeStruct((batch, NSA_KV_GROUPS, seq, nblkp), BF16)),
        grid=(batch, NSA_KV_GROUPS, nq),
        in_specs=[
            pl.BlockSpec((tq, q_cols), lambda b, g, i: (b * nq + i, BLK_NQ * LANES // q_cols + g)),
            pl.BlockSpec((1, 1, 1, n_cmp, HEAD_DIM), lambda b, g, i: (0, b, g, 0, 0)),
            pl.BlockSpec((1, 1, 1, n_cmp, HEAD_DIM), lambda b, g, i: (1, b, g, 0, 0)),
            pl.BlockSpec((1, 1, seq, HEAD_DIM), lambda b, g, i: (b, g, 0, 0)),
            pl.BlockSpec((1, 1, seq, HEAD_DIM), lambda b, g, i: (b, g, 0, 0)),
            pl.BlockSpec((tq, LANES), lambda b, g, i: (b * nq + i, BLK_NG + g)),
            pl.BlockSpec((n_cmp, nblkp), lambda b, g, i: (0, 0)),
        ],
        out_specs=(pl.BlockSpec((tq, q_cols), lambda b, g, i: (b * nq + i, g)),
                   pl.BlockSpec((1, 1, tq, nblkp), lambda b, g, i: (b, g, i, 0))),
        scratch_shapes=[pltpu.VMEM((tq, nblkp), F32), pltpu.VMEM((nblkp, tq), F32),
                        pltpu.VMEM((nblkp, tq), F32)],
        compiler_params=pltpu.CompilerParams(
            dimension_semantics=("arbitrary", "arbitrary", "arbitrary"), vmem_limit_bytes=VMEM_LIMIT),
        name="cmpwin",
    )(proj, cmp_kv, cmp_kv, kwb, vwb, proj, ov)


def _select_kernel(q_ref, selb_ref, kaug_ref, vaug_ref, part_ref, ng_ref, nz_ref, y_ref,
                   qaug_ref, acc_ref, m_ref, p_ref, alpha_ref, *, tk):
    tq = q_ref.shape[0]
    n_halves = selb_ref.shape[-1] // LANES
    q0 = pl.program_id(2) * tq
    scale = HEAD_DIM ** -0.5 * LOG2_E

    for h in range(HEADS_PER_GROUP):
        qh = (q_ref[:, h * HEAD_DIM:(h + 1) * HEAD_DIM] * scale).astype(BF16)
        for half in range(n_halves):
            qaug_ref[half * HEADS_PER_GROUP + h, :, :HEAD_DIM] = qh
            qaug_ref[half * HEADS_PER_GROUP + h, :, HEAD_DIM:] = selb_ref[0, 0, :, half * LANES:(half + 1) * LANES]
    acc_ref[...] = jnp.zeros_like(acc_ref)
    m_ref[...] = jnp.full_like(m_ref, NEG_BIG)

    def tile_step(kt, pending, causal):
        if kt is not None:
            k0 = pl.multiple_of(kt * tk, tk)
            k_t = kaug_ref[0, 0, pl.ds(k0, tk), :]
            half = k0 // (SEL_BLOCK * LANES)
            if causal:
                t_pos = lax.broadcasted_iota(jnp.int32, (tq, tk), 0) + q0
                k_pos = lax.broadcasted_iota(jnp.int32, (tq, tk), 1) + k0
                vis = k_pos <= t_pos
        if pending is not None:
            v_t = vaug_ref[0, 0, pl.ds(pl.multiple_of(pending * tk, tk), tk), :]
        if kt is not None:
            s_next = _nt_dot(qaug_ref[half * HEADS_PER_GROUP], k_t)
        for h in range(HEADS_PER_GROUP):
            if kt is not None:
                s = s_next
                if h + 1 < HEADS_PER_GROUP:
                    s_next = _nt_dot(qaug_ref[half * HEADS_PER_GROUP + h + 1], k_t)
            if pending is not None:
                alpha = alpha_ref[h]
                acc_ref[h] = (acc_ref[h] * jnp.concatenate([alpha, alpha], axis=1)
                              + jnp.dot(p_ref[h], v_t, preferred_element_type=F32))
            if kt is not None:
                if causal:
                    s = jnp.where(vis, s, NEG_BIG)
                m_prev = m_ref[h]
                m_new = jnp.maximum(m_prev, jnp.max(s, axis=-1, keepdims=True))
                p_ref[h] = jnp.exp2(s - jnp.concatenate([m_new] * (tk // LANES), axis=1)).astype(BF16)
                alpha_ref[h] = jnp.exp2(m_prev - m_new)
                m_ref[h] = m_new

    n_full = q0 // tk

    @pl.when(n_full > 0)
    def _():
        tile_step(0, None, False)

    @pl.when(n_full == 0)
    def _():
        p_ref[...] = jnp.zeros_like(p_ref)
        alpha_ref[...] = jnp.ones_like(alpha_ref)

    def step(kt, carry):
        tile_step(kt, kt - 1, False)
        return carry

    lax.fori_loop(1, n_full, step, 0)
    tile_step(n_full, jnp.maximum(n_full - 1, 0), True)
    tile_step(None, n_full, False)

    gates = _sigmoid(ng_ref[...])
    for h in range(HEADS_PER_GROUP):
        sl = slice(h * HEAD_DIM, (h + 1) * HEAD_DIM)
        acc = acc_ref[h]
        o_s = acc[:, :HEAD_DIM] / jnp.maximum(acc[:, HEAD_DIM:], 1e-30)
        g_s = gates[:, 1 * HEADS_PER_GROUP + h:1 * HEADS_PER_GROUP + h + 1]
        z = nz_ref[:, sl]
        y_ref[:, sl] = ((part_ref[:, sl] + g_s * o_s) * (z * _sigmoid(z))).astype(BF16)


def _select(proj, selb, kaug, vaug, part, batch, seq):
    tq = min(SEL_TQ, seq)
    tk = min(SEL_TK, seq)
    nq = seq // tq
    nblkp = selb.shape[-1]
    q_cols = HEADS_PER_GROUP * HEAD_DIM
    kernel = functools.partial(_select_kernel, tk=tk)
    return pl.pallas_call(
        kernel,
        out_shape=jax.ShapeDtypeStruct((batch * seq, NSA_HEADS * HEAD_DIM), BF16),
        grid=(batch, NSA_KV_GROUPS, nq),
        in_specs=[
            pl.BlockSpec((tq, q_cols), lambda b, g, i: (b * nq + i, BLK_NQ * LANES // q_cols + g)),
            pl.BlockSpec((1, 1, tq, nblkp), lambda b, g, i: (b, g, i, 0)),
            pl.BlockSpec((1, 1, seq, 2 * HEAD_DIM), lambda b, g, i: (b, g, 0, 0), pipeline_mode=pl.Buffered(1)),
            pl.BlockSpec((1, 1, seq, 2 * HEAD_DIM), lambda b, g, i: (b, g, 0, 0), pipeline_mode=pl.Buffered(1)),
            pl.BlockSpec((tq, q_cols), lambda b, g, i: (b * nq + i, g)),
            pl.BlockSpec((tq, LANES), lambda b, g, i: (b * nq + i, BLK_NG + g)),
            pl.BlockSpec((tq, q_cols), lambda b, g, i: (b * nq + i, BLK_NZ * LANES // q_cols + g)),
        ],
        out_specs=pl.BlockSpec((tq, q_cols), lambda b, g, i: (b * nq + i, g)),
        scratch_shapes=[
            pltpu.VMEM((nblkp // LANES * HEADS_PER_GROUP, tq, 2 * HEAD_DIM), BF16),
            pltpu.VMEM((HEADS_PER_GROUP, tq, 2 * HEAD_DIM), F32),
            pltpu.VMEM((HEADS_PER_GROUP, tq, LANES), F32),
            pltpu.VMEM((HEADS_PER_GROUP, tq, tk), BF16),
            pltpu.VMEM((HEADS_PER_GROUP, tq, LANES), F32),
        ],
        compiler_params=pltpu.CompilerParams(
            dimension_semantics=("arbitrary", "arbitrary", "arbitrary"), vmem_limit_bytes=VMEM_LIMIT),
        name="select",
    )(proj, selb, kaug, vaug, part, proj, proj)


def _outproj_kernel(yh_ref, yn_ref, w_ref, x_ref, g_ref, o_ref):
    half = yh_ref.shape[1]
    z = (jnp.dot(yh_ref[...], w_ref[:half, :], preferred_element_type=F32)
         + jnp.dot(yn_ref[...], w_ref[half:, :], preferred_element_type=F32))
    ms = jnp.mean(z * z, axis=-1, keepdims=True)
    o_ref[...] = x_ref[...] + z * lax.rsqrt(ms + NORM_EPS) * g_ref[...]


def _outproj(yh, yn, w, x2, g):
    n = x2.shape[0]
    tm = min(OUT_TM, n)
    return pl.pallas_call(
        _outproj_kernel,
        out_shape=jax.ShapeDtypeStruct((n, D_MODEL), F32),
        grid=(n // tm,),
        in_specs=[
            pl.BlockSpec((tm, yh.shape[1]), lambda i: (i, 0)),
            pl.BlockSpec((tm, yn.shape[1]), lambda i: (i, 0)),
            pl.BlockSpec(w.shape, lambda i: (0, 0)),
            pl.BlockSpec((tm, D_MODEL), lambda i: (i, 0)),
            pl.BlockSpec((1, D_MODEL), lambda i: (0, 0)),
        ],
        out_specs=pl.BlockSpec((tm, D_MODEL), lambda i: (i, 0)),
        compiler_params=pltpu.CompilerParams(
            dimension_semantics=("arbitrary",), vmem_limit_bytes=VMEM_LIMIT),
        name="outproj",
    )(yh, yn, w, x2, g)


def _proj_column_map():
    idx = np.full((PROJ_COLS,), -1, np.int64)

    def put(dst_blk, src, width):
        idx[dst_blk * LANES:dst_blk * LANES + width] = np.arange(src, src + width)

    for name, blk, width in (("hq", BLK_HQ, 1024), ("hf", BLK_HF, 1024), ("hi", BLK_HI, 1024),
                             ("hz", BLK_HZ, 1024), ("nq", BLK_NQ, 1024), ("ks", BLK_KS, 256),
                             ("kw", BLK_KW, 256), ("kc", BLK_KC, 256), ("vc", BLK_VC, 256),
                             ("vs", BLK_VS, 256), ("vw", BLK_VW, 256), ("nz", BLK_NZ, 1024)):
        put(blk, _SRC[name], width)
    for g in range(NSA_KV_GROUPS):
        for br in range(N_BRANCHES):
            for h in range(HEADS_PER_GROUP):
                idx[(BLK_NG + g) * LANES + br * HEADS_PER_GROUP + h] = (
                    _SRC["ng"] + (g * HEADS_PER_GROUP + h) * N_BRANCHES + br)
    return idx


def _rope_tables(pos):
    half = ROT_DIM // 2
    inv = ROPE_THETA ** (-2.0 * jnp.arange(half, dtype=F32) / ROT_DIM)
    ang = pos.astype(F32)[:, None] * inv[None, :]
    cos, sin = jnp.cos(ang), jnp.sin(ang)
    n = pos.shape[0]
    pad = jnp.zeros((n, HEAD_DIM - ROT_DIM), F32)
    zero = jnp.zeros((n, half), F32)
    c = jnp.concatenate([cos, cos, pad + 1.0], axis=1)
    s1 = jnp.concatenate([-sin, zero, pad], axis=1)
    s2 = jnp.concatenate([zero, sin, pad], axis=1)
    return c, s1, s2


def _overlap_matrix(n_cmp, n_blocks, nblkp):
    start = np.arange(n_cmp)[:, None] * CMP_STRIDE
    sel = np.arange(nblkp)[None, :] * SEL_BLOCK
    ov = (start < sel + SEL_BLOCK) & (start + CMP_BLOCK > sel)
    ov &= np.arange(nblkp)[None, :] < n_blocks
    ov &= np.arange(n_cmp)[:, None] < (n_cmp - 1)
    return jnp.asarray(ov, BF16)


def kernel(x, pre_norm, post_norm, w_in, hgrn_lb_logits, hgrn_out_norm, cmp_pos_k, cmp_w1_k, cmp_b1_k,
           cmp_w2_k, cmp_pos_v, cmp_w1_v, cmp_b1_v, cmp_w2_v, w_out):
    batch, seq, _ = x.shape
    depth = w_in.shape[0]
    n_cmp = seq // CMP_STRIDE
    n_blocks = seq // SEL_BLOCK
    nblkp = -(-n_blocks // LANES) * LANES

    lb_probs = jax.nn.softmax(hgrn_lb_logits.astype(F32), axis=0)
    lower = jnp.maximum(jnp.cumsum(lb_probs, axis=0) - lb_probs[0:1], 0.0)
    lower = lower.reshape(depth, HGRN_HEADS, 1, HEAD_DIM)
    g_out = jnp.broadcast_to(hgrn_out_norm.astype(F32)[:, None, None, :], lower.shape)
    hgrn_par = jnp.concatenate(
        [jnp.log(lower), jnp.log1p(-lower), 1.0 - lower, g_out, jnp.zeros((depth, HGRN_HEADS, 4, HEAD_DIM), F32)],
        axis=2)

    col_map = _proj_column_map()
    col_src = jnp.asarray(np.maximum(col_map, 0), jnp.int32)
    col_ok = jnp.asarray(col_map >= 0)

    rope_c, rope_s1, rope_s2 = _rope_tables(jnp.arange(seq))
    cmp_tab_k = jnp.stack(_rope_tables(jnp.arange(n_cmp) * CMP_STRIDE + CMP_BLOCK - 1))
    cmp_tab_v = jnp.stack([jnp.ones((n_cmp, HEAD_DIM), F32), jnp.zeros((n_cmp, HEAD_DIM), F32),
                           jnp.zeros((n_cmp, HEAD_DIM), F32)])
    cmp_tabs = jnp.stack([cmp_tab_k, cmp_tab_v])
    ov = _overlap_matrix(n_cmp, n_blocks, nblkp)
    tri = jnp.asarray(np.tril(np.ones((HGRN_CHUNK, HGRN_CHUNK), np.float32)))
    ones = jnp.ones((HEAD_DIM, HEAD_DIM), BF16)

    x2 = x.reshape(batch * seq, D_MODEL)
    for layer in range(depth):
        w_perm = jnp.where(col_ok[None, :], jnp.take(w_in[layer], col_src, axis=1), 0.0).astype(BF16)
        proj = _proj(x2, pre_norm[layer][None, :], w_perm, rope_c, rope_s1, rope_s2, seq)
        y_h = _hgrn(proj, hgrn_par[layer], tri, ones, batch, seq)
        cmp_kv = _compress(
            proj,
            jnp.stack([cmp_w1_k[layer], cmp_w1_v[layer]]).astype(BF16),
            jnp.stack([cmp_b1_k[layer], cmp_b1_v[layer]])[:, None, :],
            jnp.stack([cmp_w2_k[layer], cmp_w2_v[layer]]).astype(BF16),
            jnp.stack([cmp_pos_k[layer], cmp_pos_v[layer]]),
            cmp_tabs, batch, seq)
        kaug, vaug, kwb, vwb = _kvprep(proj, batch, seq)
        part, selb = _cmpwin(proj, cmp_kv, kwb, vwb, ov, batch, seq)
        y_n = _select(proj, selb, kaug, vaug, part, batch, seq)
        x2 = _outproj(y_h, y_n, w_out[layer].astype(BF16), x2, post_norm[layer][None, :])
    return x2.reshape(batch, seq, D_MODEL)
```

```python
import functools

import numpy as np
import jax
import jax.numpy as jnp
from jax import lax
from jax.experimental import pallas as pl
from jax.experimental.pallas import tpu as pltpu

F32 = jnp.float32
BF16 = jnp.bfloat16

D_MODEL = 1024
HEAD_DIM = 128
HGRN_HEADS = 8
NSA_HEADS = 8
NSA_KV_GROUPS = 2
HEADS_PER_GROUP = NSA_HEADS // NSA_KV_GROUPS
N_BRANCHES = 3
HGRN_CHUNK = 64
HGRN_SUB = 8
CMP_BLOCK = 32
CMP_STRIDE = 16
SEL_BLOCK = 64
N_SELECT = 16
WINDOW = 512
ROPE_THETA = 500000.0
ROT_DIM = HEAD_DIM // 4
NORM_EPS = 1e-6
FORCE_SCORE = 1e4
NEG_BIG = -1e30
LOG2_E = 1.4426950408889634

LANES = 128
BLK_HQ, BLK_HI, BLK_HZ = 0, 8, 16
BLK_NQ, BLK_KS, BLK_KW, BLK_KC, BLK_VC, BLK_VS, BLK_VW = 24, 32, 34, 36, 38, 40, 42
BLK_NZ, BLK_NG, BLK_HF = 44, 52, 56
PROJ_BLOCKS = 64
PROJ_COLS = PROJ_BLOCKS * LANES
PROJ_BF16_COLS = BLK_HF * LANES
_SRC = dict(hq=0, hf=1024, hi=2048, hz=3072, nq=4096, kc=5120, vc=5376, ks=5632, vs=5888,
            kw=6144, vw=6400, ng=6656, nz=6680)

PROJ_TM = 1024
PROJ_TN = 1024
ROPE_FULL_TILE = BLK_NQ * LANES // PROJ_TN
ROPE_HEAD_TILE = BLK_KS * LANES // PROJ_TN
ROPE_HEAD_BLOCKS = 4
F32_TILE = BLK_HF * LANES // PROJ_TN
HGRN_T = 512
PREP_T = 1024
ATT_TQ = 256
CMP_KEY_CHUNK = 256
SEL_TQ = 512
SEL_TK = 1024
OUT_TM = 512
VMEM_LIMIT = 56 * 1024 * 1024


def _nt_dot(a, b):
    return lax.dot_general(a, b, (((1,), (1,)), ((), ())), preferred_element_type=F32)


def _sigmoid(x):
    return 1.0 / (1.0 + jnp.exp(-x))


def _proj_kernel(x_ref, g_ref, w_ref, c_ref, s1_ref, s2_ref, o_ref, o32_ref, h_ref, acc_ref):
    j = pl.program_id(1)

    @pl.when(j == 0)
    def _():
        x = x_ref[...]
        ms = jnp.mean(x * x, axis=-1, keepdims=True)
        h_ref[...] = (x * lax.rsqrt(ms + NORM_EPS) * g_ref[...]).astype(BF16)

    @pl.when(j == F32_TILE)
    def _():
        o32_ref[...] = jnp.dot(h_ref[...], w_ref[...], preferred_element_type=F32)

    def store(n_rope_blocks):
        acc_ref[...] = jnp.dot(h_ref[...], w_ref[...], preferred_element_type=F32)
        for blk in range(n_rope_blocks):
            sl = slice(blk * LANES, (blk + 1) * LANES)
            a = acc_ref[:, sl]
            o_ref[:, sl] = (a * c_ref[...] + pltpu.roll(a, LANES - ROT_DIM // 2, 1) * s1_ref[...]
                            + pltpu.roll(a, ROT_DIM // 2, 1) * s2_ref[...]).astype(BF16)
        if n_rope_blocks * LANES < PROJ_TN:
            rest = slice(n_rope_blocks * LANES, PROJ_TN)
            o_ref[:, rest] = acc_ref[:, rest].astype(BF16)

    @pl.when(j == ROPE_FULL_TILE)
    def _():
        store(PROJ_TN // LANES)

    @pl.when(j == ROPE_HEAD_TILE)
    def _():
        store(ROPE_HEAD_BLOCKS)

    @pl.when((j != ROPE_FULL_TILE) & (j != ROPE_HEAD_TILE) & (j != F32_TILE))
    def _():
        o_ref[...] = jnp.dot(h_ref[...], w_ref[...], preferred_element_type=F32).astype(BF16)


def _proj(x2, g, w, rope_c, rope_s1, rope_s2, seq):
    n = x2.shape[0]
    tm = min(PROJ_TM, seq)
    pos_tiles = seq // tm
    return pl.pallas_call(
        _proj_kernel,
        out_shape=(jax.ShapeDtypeStruct((n, PROJ_BF16_COLS), BF16),
                   jax.ShapeDtypeStruct((n, PROJ_COLS - PROJ_BF16_COLS), F32)),
        grid=(n // tm, PROJ_COLS // PROJ_TN),
        in_specs=[
            pl.BlockSpec((tm, D_MODEL), lambda i, j: (i, 0)),
            pl.BlockSpec((1, D_MODEL), lambda i, j: (0, 0)),
            pl.BlockSpec((D_MODEL, PROJ_TN), lambda i, j: (0, j)),
            pl.BlockSpec((tm, LANES), lambda i, j: (i % pos_tiles, 0)),
            pl.BlockSpec((tm, LANES), lambda i, j: (i % pos_tiles, 0)),
            pl.BlockSpec((tm, LANES), lambda i, j: (i % pos_tiles, 0)),
        ],
        out_specs=(pl.BlockSpec((tm, PROJ_TN), lambda i, j: (i, jnp.minimum(j, F32_TILE - 1))),
                   pl.BlockSpec((tm, PROJ_TN), lambda i, j: (i, 0))),
        scratch_shapes=[pltpu.VMEM((tm, D_MODEL), BF16), pltpu.VMEM((tm, PROJ_TN), F32)],
        compiler_params=pltpu.CompilerParams(
            dimension_semantics=("arbitrary", "arbitrary"), vmem_limit_bytes=VMEM_LIMIT),
        name="proj",
    )(x2, g, w, rope_c, rope_s1, rope_s2)


def _hgrn_kernel(q_ref, f_ref, i_ref, z_ref, par_ref, tri_ref, ones_ref, y_ref, st_ref, o_ref):
    @pl.when(pl.program_id(2) == 0)
    def _():
        st_ref[...] = jnp.zeros_like(st_ref)

    t = q_ref.shape[0]
    nc = t // HGRN_CHUNK
    shape3 = (nc, HGRN_CHUNK, HEAD_DIM)
    par = par_ref[0]
    log_lb, log_1m_lb, one_m_lb, g_norm = par[0:1], par[1:2], par[2:3], par[3:4]
    x = f_ref[...]
    e = jnp.exp(-jnp.abs(x))
    c = log_1m_lb + jnp.minimum(x, 0.0) - jnp.log(1.0 + e)
    lf = jnp.maximum(log_lb, c) + jnp.log(1.0 + jnp.exp(-jnp.abs(log_lb - c)))
    k3 = (one_m_lb * jnp.where(x >= 0.0, e, 1.0) / (1.0 + e)).reshape(shape3)
    q3 = q_ref[...].astype(F32).reshape(shape3)
    v3 = i_ref[...].astype(F32).reshape(shape3)
    v_bf = v3.astype(BF16)

    lf3 = (lf * LOG2_E).reshape(shape3)
    b3 = jnp.stack([jnp.dot(tri_ref[...], lf3[ci], precision=lax.Precision.HIGHEST,
                            preferred_element_type=F32) for ci in range(nc)])
    row = lax.broadcasted_iota(jnp.int32, shape3, 1)
    t_idx = lax.broadcasted_iota(jnp.int32, (nc, HGRN_CHUNK, HGRN_CHUNK), 1)
    s_idx = lax.broadcasted_iota(jnp.int32, (nc, HGRN_CHUNK, HGRN_CHUNK), 2)

    attn = jnp.zeros((nc, HGRN_CHUNK, HGRN_CHUNK), F32)
    n = HGRN_CHUNK // 2
    while n >= HGRN_SUB:
        blocks = (nc * HGRN_CHUNK // (2 * n), 2 * n, HEAD_DIM)
        b_blk = b3.reshape(blocks)
        ref = jnp.broadcast_to(b_blk[:, n - 1:n, :], blocks).reshape(shape3)
        upper = (row & n) != 0
        q_n = jnp.where(upper, q3 * jnp.exp2(jnp.where(upper, b3 - ref, 0.0)), 0.0).astype(BF16)
        k_n = jnp.where(upper, 0.0, k3 * jnp.exp2(jnp.where(upper, 0.0, ref - b3))).astype(BF16)
        a_n = jnp.stack([_nt_dot(q_n[ci], k_n[ci]) for ci in range(nc)])
        if 2 * n < HGRN_CHUNK:
            same = (t_idx & -(2 * n)) == (s_idx & -(2 * n))
            a_n = jnp.where(same, a_n, 0.0)
        attn = attn + a_n
        n //= 2
    attn_bf = attn.astype(BF16)
    o_off = jnp.stack([jnp.dot(attn_bf[ci], v_bf[ci], preferred_element_type=F32) for ci in range(nc)])

    subs = (nc * HGRN_CHUNK // HGRN_SUB, HGRN_SUB, HEAD_DIM)
    sub_row = lax.broadcasted_iota(jnp.int32, subs, 1)
    b4, q4, k4, v4 = b3.reshape(subs), q3.reshape(subs), k3.reshape(subs), v3.reshape(subs)
    diag = jnp.zeros(subs, F32)
    for s in range(HGRN_SUB):
        d = jnp.where(sub_row >= s, b4 - b4[:, s:s + 1, :], NEG_BIG)
        p = jnp.exp2(d) * q4 * k4[:, s:s + 1, :]
        r = jnp.dot(p.reshape(t, HEAD_DIM).astype(BF16), ones_ref[...], preferred_element_type=F32)
        diag = diag + r.reshape(subs) * v4[:, s:s + 1, :]
    o_intra = o_off + diag.reshape(shape3)

    b_last = b3[:, HGRN_CHUNK - 1:HGRN_CHUNK, :]
    decay = jnp.exp2(b_last)
    q_dec = (q3 * jnp.exp2(b3)).astype(BF16)
    k_dec = (k3 * jnp.exp2(b_last - b3)).astype(BF16)
    upd = [lax.dot_general(v_bf[ci], k_dec[ci], (((0,), (0,)), ((), ())), preferred_element_type=F32)
           for ci in range(nc)]
    st = st_ref[...]
    for ci in range(nc):
        o_ref[ci * HGRN_CHUNK:(ci + 1) * HGRN_CHUNK, :] = o_intra[ci] + _nt_dot(q_dec[ci], st.astype(BF16))
        st = st * decay[ci] + upd[ci]
    st_ref[...] = st

    o = o_ref[...]
    ms = jnp.mean(o * o, axis=-1, keepdims=True)
    z = z_ref[...].astype(F32)
    y_ref[...] = (o * lax.rsqrt(ms + NORM_EPS) * g_norm * (z * _sigmoid(z))).astype(BF16)


def _hgrn(proj, proj_f32, par, tri, ones, batch, seq):
    t = min(HGRN_T, seq)
    nt = seq // t

    def col(base):
        return pl.BlockSpec((t, HEAD_DIM), lambda b, h, ti: (b * nt + ti, base + h))

    return pl.pallas_call(
        _hgrn_kernel,
        out_shape=jax.ShapeDtypeStruct((batch * seq, HGRN_HEADS * HEAD_DIM), BF16),
        grid=(batch, HGRN_HEADS, nt),
        in_specs=[
            col(BLK_HQ), col(0), col(BLK_HI), col(BLK_HZ),
            pl.BlockSpec((1, 8, HEAD_DIM), lambda b, h, ti: (h, 0, 0)),
            pl.BlockSpec((HGRN_CHUNK, HGRN_CHUNK), lambda b, h, ti: (0, 0)),
            pl.BlockSpec((HEAD_DIM, HEAD_DIM), lambda b, h, ti: (0, 0)),
        ],
        out_specs=pl.BlockSpec((t, HEAD_DIM), lambda b, h, ti: (b * nt + ti, h)),
        scratch_shapes=[pltpu.VMEM((HEAD_DIM, HEAD_DIM), F32), pltpu.VMEM((t, HEAD_DIM), F32)],
        compiler_params=pltpu.CompilerParams(
            dimension_semantics=("arbitrary", "arbitrary", "arbitrary"), vmem_limit_bytes=VMEM_LIMIT),
        name="hgrn",
    )(proj, proj_f32, proj, proj, par, tri, ones)


def _compress_kernel(x_ref, w1_ref, b1_ref, w2_ref, pos_ref, tab_ref, o_ref, x32_ref):
    n_rows = o_ref.shape[-2]
    x32_ref[...] = x_ref[...].astype(F32)
    acc_lo = jnp.zeros((n_rows, HEAD_DIM), F32)
    acc_hi = jnp.zeros((n_rows, HEAD_DIM), F32)
    for r in range(CMP_STRIDE):
        xr = x32_ref[pl.ds(r, n_rows, stride=CMP_STRIDE), :]
        lo, hi = r, CMP_STRIDE + r
        acc_lo = acc_lo + jnp.dot((xr + pos_ref[0, lo:lo + 1, :]).astype(BF16),
                                  w1_ref[0, lo * HEAD_DIM:(lo + 1) * HEAD_DIM, :],
                                  preferred_element_type=F32)
        acc_hi = acc_hi + jnp.dot((xr + pos_ref[0, hi:hi + 1, :]).astype(BF16),
                                  w1_ref[0, hi * HEAD_DIM:(hi + 1) * HEAD_DIM, :],
                                  preferred_element_type=F32)
    pre = acc_lo + pltpu.roll(acc_hi, n_rows - 1, 0) + b1_ref[0]
    mid = pre * _sigmoid(pre)
    out = jnp.dot(mid.astype(BF16), w2_ref[0], preferred_element_type=F32)
    o_ref[0, 0, 0] = (out * tab_ref[0, 0] + pltpu.roll(out, LANES - ROT_DIM // 2, 1) * tab_ref[0, 1]
                      + pltpu.roll(out, ROT_DIM // 2, 1) * tab_ref[0, 2]).astype(BF16)


def _compress(proj, w1, b1, w2, pos, tabs, batch, seq):
    n_rows = seq // CMP_STRIDE
    return pl.pallas_call(
        _compress_kernel,
        out_shape=jax.ShapeDtypeStruct((2, batch, NSA_KV_GROUPS, n_rows, HEAD_DIM), BF16),
        grid=(2, batch, NSA_KV_GROUPS),
        in_specs=[
            pl.BlockSpec((seq, HEAD_DIM), lambda kv, b, g: (b, BLK_KC + 2 * kv + g)),
            pl.BlockSpec((1, CMP_BLOCK * HEAD_DIM, HEAD_DIM), lambda kv, b, g: (kv, 0, 0)),
            pl.BlockSpec((1, 1, HEAD_DIM), lambda kv, b, g: (kv, 0, 0)),
            pl.BlockSpec((1, HEAD_DIM, HEAD_DIM), lambda kv, b, g: (kv, 0, 0)),
            pl.BlockSpec((1, CMP_BLOCK, HEAD_DIM), lambda kv, b, g: (kv, 0, 0)),
            pl.BlockSpec((1, 3, n_rows, HEAD_DIM), lambda kv, b, g: (kv, 0, 0, 0)),
        ],
        out_specs=pl.BlockSpec((1, 1, 1, n_rows, HEAD_DIM), lambda kv, b, g: (kv, b, g, 0, 0)),
        scratch_shapes=[pltpu.VMEM((seq, HEAD_DIM), F32)],
        compiler_params=pltpu.CompilerParams(
            dimension_semantics=("arbitrary", "arbitrary", "arbitrary"), vmem_limit_bytes=VMEM_LIMIT),
        name="compress",
    )(proj, w1, b1, w2, pos, tabs)


def _kvprep_kernel(ks_ref, kw_ref, vs_ref, vw_ref, kaug_ref, vaug_ref, kwb_ref, vwb_ref):
    t = ks_ref.shape[0]
    t0 = pl.program_id(2) * t
    row = lax.broadcasted_iota(jnp.int32, (t, LANES), 0) + t0
    col = lax.broadcasted_iota(jnp.int32, (t, LANES), 1)
    block_lane = lax.shift_right_logical(row, 6) & (LANES - 1)
    kaug_ref[0, 0, :, :HEAD_DIM] = ks_ref[...].astype(BF16)
    kaug_ref[0, 0, :, HEAD_DIM:] = jnp.where(block_lane == col, 1.0, 0.0).astype(BF16)
    vaug_ref[0, 0, :, :HEAD_DIM] = vs_ref[...].astype(BF16)
    vaug_ref[0, 0, :, HEAD_DIM:] = jnp.ones((t, LANES), BF16)
    kwb_ref[0, 0] = kw_ref[...].astype(BF16)
    vwb_ref[0, 0] = vw_ref[...].astype(BF16)


def _kvprep(proj, batch, seq):
    t = min(PREP_T, seq)
    nt = seq // t

    def col(base):
        return pl.BlockSpec((t, HEAD_DIM), lambda b, g, ti: (b * nt + ti, base + g))

    def out(width):
        return pl.BlockSpec((1, 1, t, width), lambda b, g, ti: (b, g, ti, 0))

    def shape(width):
        return jax.ShapeDtypeStruct((batch, NSA_KV_GROUPS, seq, width), BF16)

    return pl.pallas_call(
        _kvprep_kernel,
        out_shape=(shape(2 * HEAD_DIM), shape(2 * HEAD_DIM), shape(HEAD_DIM), shape(HEAD_DIM)),
        grid=(batch, NSA_KV_GROUPS, nt),
        in_specs=[col(BLK_KS), col(BLK_KW), col(BLK_VS), col(BLK_VW)],
        out_specs=(out(2 * HEAD_DIM), out(2 * HEAD_DIM), out(HEAD_DIM), out(HEAD_DIM)),
        compiler_params=pltpu.CompilerParams(
            dimension_semantics=("arbitrary", "arbitrary", "arbitrary"), vmem_limit_bytes=VMEM_LIMIT),
        name="kvprep",
    )(proj, proj, proj, proj)


def _cmpwin_kernel(q_ref, kc_ref, vc_ref, kw_ref, vw_ref, ng_ref, ov_ref, part_ref, selb_ref,
                   imp_ref, val_ref, sel_ref, *, n_blocks, win_len, key_chunk):
    tq = q_ref.shape[0]
    n_cmp = kc_ref.shape[-2]
    nblkp = selb_ref.shape[-1]
    q0 = pl.program_id(2) * tq
    gates = _sigmoid(ng_ref[...].astype(F32))
    q_heads = [q_ref[:, h * HEAD_DIM:(h + 1) * HEAD_DIM] for h in range(HEADS_PER_GROUP)]

    def attend(qh, keys, values_ones, vis, all_masked_possible):
        s = jnp.where(vis, _nt_dot(qh, keys), NEG_BIG)
        m = jnp.max(s, axis=-1, keepdims=True)
        if all_masked_possible:
            m = jnp.where(m < 0.5 * NEG_BIG, 0.0, m)
        p = jnp.exp2(s - m)
        o = jnp.dot(p.astype(BF16), values_ones, preferred_element_type=F32)
        inv = 1.0 / jnp.maximum(o[:, HEAD_DIM:], 1e-30)
        return o[:, :HEAD_DIM] * inv, p, inv

    w0 = pl.multiple_of(jnp.maximum(q0 + tq - win_len, 0), tq)
    kwin = kw_ref[0, 0, pl.ds(w0, win_len), :]
    vwin = jnp.concatenate([vw_ref[0, 0, pl.ds(w0, win_len), :], jnp.ones((win_len, HEAD_DIM), BF16)], axis=1)
    t_w = lax.broadcasted_iota(jnp.int32, (tq, win_len), 0) + q0
    kp_w = lax.broadcasted_iota(jnp.int32, (tq, win_len), 1) + w0
    vis_w = (kp_w <= t_w) & (t_w - kp_w < WINDOW)
    for h in range(HEADS_PER_GROUP):
        o_w, _, _ = attend(q_heads[h], kwin, vwin, vis_w, False)
        g_w = gates[:, 2 * HEADS_PER_GROUP + h:2 * HEADS_PER_GROUP + h + 1]
        part_ref[:, h * HEAD_DIM:(h + 1) * HEAD_DIM] = g_w * o_w

    def compressed(n_keys):
        kc = kc_ref[0, 0, 0, :n_keys, :]
        vc = jnp.concatenate([vc_ref[0, 0, 0, :n_keys, :], jnp.ones((n_keys, HEAD_DIM), BF16)], axis=1)
        t_c = lax.broadcasted_iota(jnp.int32, (tq, n_keys), 0) + q0
        n_c = lax.broadcasted_iota(jnp.int32, (tq, n_keys), 1)
        vis_c = n_c * CMP_STRIDE + (CMP_BLOCK - 1) <= t_c
        p_sum = jnp.zeros((tq, n_keys), F32)
        for h in range(HEADS_PER_GROUP):
            sl = slice(h * HEAD_DIM, (h + 1) * HEAD_DIM)
            o_c, p, inv = attend(q_heads[h], kc, vc, vis_c, True)
            p_sum = p_sum + p * jnp.concatenate([inv] * (n_keys // LANES), axis=1)
            g_c = gates[:, 0 * HEADS_PER_GROUP + h:0 * HEADS_PER_GROUP + h + 1]
            part_ref[:, sl] = part_ref[:, sl] + g_c * o_c
        p_hi = p_sum.astype(BF16)
        p_lo = (p_sum - p_hi.astype(F32)).astype(BF16)
        imp_ref[...] = (jnp.dot(p_hi, ov_ref[:n_keys, :], preferred_element_type=F32)
                        + jnp.dot(p_lo, ov_ref[:n_keys, :], preferred_element_type=F32))

    n_visible = (q0 + tq - CMP_BLOCK) // CMP_STRIDE + 1
    n_chunks = jnp.minimum((n_visible + key_chunk - 1) // key_chunk, n_cmp // key_chunk)
    for c in range(1, n_cmp // key_chunk + 1):
        @pl.when(n_chunks == c)
        def _():
            compressed(c * key_chunk)

    imp_t = imp_ref[...].T
    t_b = lax.broadcasted_iota(jnp.int32, (nblkp, tq), 1) + q0
    blk = lax.broadcasted_iota(jnp.int32, (nblkp, tq), 0)
    cur = lax.shift_right_logical(t_b, 6)
    forced = (blk == 0) | (blk == cur) | (blk == cur - 1)
    val = jnp.where(forced, -4.0, jnp.where(blk * SEL_BLOCK <= t_b, imp_t, -1.0))
    val_ref[...] = jnp.where(blk < n_blocks, val, -3.0)
    sel_ref[...] = jnp.where(forced, 1.0, 0.0)
    blk_f = blk.astype(F32)
    n_top = min(N_SELECT, n_blocks)
    quota = n_top - 1 - jnp.minimum(cur[0:1, :], 2)

    def pick(it, carry):
        v = val_ref[...]
        m = jnp.max(v, axis=0, keepdims=True)
        idx = jnp.min(jnp.where(v == m, blk_f, float(nblkp)), axis=0, keepdims=True)
        hit = (blk_f == idx) & (it < quota)
        sel_ref[...] = jnp.where(hit, 1.0, sel_ref[...])
        val_ref[...] = jnp.where(hit, -4.0, v)
        return carry

    lax.fori_loop(0, n_top - 1 - jnp.minimum(lax.shift_right_logical(q0, 6), 2), pick, 0)
    selb_ref[0, 0] = jnp.where(sel_ref[...].T > 0.5, 0.0, NEG_BIG).astype(BF16)


def _cmpwin(proj, cmp_kv, kwb, vwb, ov, batch, seq):
    tq = min(ATT_TQ, seq)
    nq = seq // tq
    n_cmp = seq // CMP_STRIDE
    n_blocks = seq // SEL_BLOCK
    nblkp = ov.shape[1]
    win_len = min(WINDOW + tq, seq)
    q_cols = HEADS_PER_GROUP * HEAD_DIM
    key_chunk = min(CMP_KEY_CHUNK, n_cmp)
    kernel = functools.partial(_cmpwin_kernel, n_blocks=n_blocks, win_len=win_len, key_chunk=key_chunk)
    return pl.pallas_call(
        kernel,
        out_shape=(jax.ShapeDtypeStruct((batch * seq, NSA_HEADS * HEAD_DIM), F32),
                   jax.ShapeDtypeStruct((batch, NSA_KV_GROUPS, seq, nblkp), BF16)),
        grid=(batch, NSA_KV_GROUPS, nq),
        in_specs=[
            pl.BlockSpec((tq, q_cols), lambda b, g, i: (b * nq + i, BLK_NQ * LANES // q_cols + g)),
            pl.BlockSpec((1, 1, 1, n_cmp, HEAD_DIM), lambda b, g, i: (0, b, g, 0, 0)),
            pl.BlockSpec((1, 1, 1, n_cmp, HEAD_DIM), lambda b, g, i: (1, b, g, 0, 0)),
            pl.BlockSpec((1, 1, seq, HEAD_DIM), lambda b, g, i: (b, g, 0, 0)),
            pl.BlockSpec((1, 1, seq, HEAD_DIM), lambda b, g, i: (b, g, 0, 0)),
            pl.BlockSpec((tq, LANES), lambda b, g, i: (b * nq + i, BLK_NG + g)),
            pl.BlockSpec((n_cmp, nblkp), lambda b, g, i: (0, 0)),
        ],
        out_specs=(pl.BlockSpec((tq, q_cols), lambda b, g, i: (b * nq + i, g)),
                   pl.BlockSpec((1, 1, tq, nblkp), lambda b, g, i: (b, g, i, 0))),
        scratch_shapes=[pltpu.VMEM((tq, nblkp), F32), pltpu.VMEM((nblkp, tq), F32),
                        pltpu.VMEM((nblkp, tq), F32)],
        compiler_params=pltpu.CompilerParams(
            dimension_semantics=("arbitrary", "arbitrary", "arbitrary"), vmem_limit_bytes=VMEM_LIMIT),
        name="cmpwin",
    )(proj, cmp_kv, cmp_kv, kwb, vwb, proj, ov)


def _select_kernel(q_ref, selb_ref, kaug_ref, vaug_ref, part_ref, ng_ref, nz_ref, y_ref,
                   qaug_ref, acc_ref, m_ref, p_ref, alpha_ref, *, tk):
    tq = q_ref.shape[0]
    n_halves = selb_ref.shape[-1] // LANES
    q0 = pl.program_id(2) * tq

    for h in range(HEADS_PER_GROUP):
        qh = q_ref[:, h * HEAD_DIM:(h + 1) * HEAD_DIM]
        for half in range(n_halves):
            qaug_ref[half * HEADS_PER_GROUP + h, :, :HEAD_DIM] = qh
            qaug_ref[half * HEADS_PER_GROUP + h, :, HEAD_DIM:] = selb_ref[0, 0, :, half * LANES:(half + 1) * LANES]
    acc_ref[...] = jnp.zeros_like(acc_ref)
    m_ref[...] = jnp.full_like(m_ref, NEG_BIG)

    def tile_step(kt, pending, causal):
        if kt is not None:
            k0 = pl.multiple_of(kt * tk, tk)
            k_t = kaug_ref[0, 0, pl.ds(k0, tk), :]
            half = k0 // (SEL_BLOCK * LANES)
            if causal:
                t_pos = lax.broadcasted_iota(jnp.int32, (tq, tk), 0) + q0
                k_pos = lax.broadcasted_iota(jnp.int32, (tq, tk), 1) + k0
                vis = k_pos <= t_pos
        if pending is not None:
            v_t = vaug_ref[0, 0, pl.ds(pl.multiple_of(pending * tk, tk), tk), :]
        if kt is not None:
            s_next = _nt_dot(qaug_ref[half * HEADS_PER_GROUP], k_t)
        for h in range(HEADS_PER_GROUP):
            if kt is not None:
                s = s_next
                if h + 1 < HEADS_PER_GROUP:
                    s_next = _nt_dot(qaug_ref[half * HEADS_PER_GROUP + h + 1], k_t)
            if pending is not None:
                alpha = alpha_ref[h]
                acc_ref[h] = (acc_ref[h] * jnp.concatenate([alpha, alpha], axis=1)
                              + jnp.dot(p_ref[h], v_t, preferred_element_type=F32))
            if kt is not None:
                if causal:
                    s = jnp.where(vis, s, NEG_BIG)
                m_prev = m_ref[h]
                m_new = jnp.maximum(m_prev, jnp.max(s, axis=-1, keepdims=True))
                p_ref[h] = jnp.exp2(s - jnp.concatenate([m_new] * (tk // LANES), axis=1)).astype(BF16)
                alpha_ref[h] = jnp.exp2(m_prev - m_new)
                m_ref[h] = m_new

    n_full = q0 // tk

    @pl.when(n_full > 0)
    def _():
        tile_step(0, None, False)

    @pl.when(n_full == 0)
    def _():
        p_ref[...] = jnp.zeros_like(p_ref)
        alpha_ref[...] = jnp.ones_like(alpha_ref)

    def step(kt, carry):
        tile_step(kt, kt - 1, False)
        return carry

    lax.fori_loop(1, n_full, step, 0)
    tile_step(n_full, jnp.maximum(n_full - 1, 0), True)
    tile_step(None, n_full, False)

    gates = _sigmoid(ng_ref[...].astype(F32))
    for h in range(HEADS_PER_GROUP):
        sl = slice(h * HEAD_DIM, (h + 1) * HEAD_DIM)
        acc = acc_ref[h]
        o_s = acc[:, :HEAD_DIM] / jnp.maximum(acc[:, HEAD_DIM:], 1e-30)
        g_s = gates[:, 1 * HEADS_PER_GROUP + h:1 * HEADS_PER_GROUP + h + 1]
        z = nz_ref[:, sl].astype(F32)
        y_ref[:, sl] = ((part_ref[:, sl] + g_s * o_s) * (z * _sigmoid(z))).astype(BF16)


def _select(proj, selb, kaug, vaug, part, batch, seq):
    tq = min(SEL_TQ, seq)
    tk = min(SEL_TK, seq)
    nq = seq // tq
    nblkp = selb.shape[-1]
    q_cols = HEADS_PER_GROUP * HEAD_DIM
    kernel = functools.partial(_select_kernel, tk=tk)
    return pl.pallas_call(
        kernel,
        out_shape=jax.ShapeDtypeStruct((batch * seq, NSA_HEADS * HEAD_DIM), BF16),
        grid=(batch, NSA_KV_GROUPS, nq),
        in_specs=[
            pl.BlockSpec((tq, q_cols), lambda b, g, i: (b * nq + i, BLK_NQ * LANES // q_cols + g)),
            pl.BlockSpec((1, 1, tq, nblkp), lambda b, g, i: (b, g, i, 0)),
            pl.BlockSpec((1, 1, seq, 2 * HEAD_DIM), lambda b, g, i: (b, g, 0, 0), pipeline_mode=pl.Buffered(1)),
            pl.BlockSpec((1, 1, seq, 2 * HEAD_DIM), lambda b, g, i: (b, g, 0, 0), pipeline_mode=pl.Buffered(1)),
            pl.BlockSpec((tq, q_cols), lambda b, g, i: (b * nq + i, g)),
            pl.BlockSpec((tq, LANES), lambda b, g, i: (b * nq + i, BLK_NG + g)),
            pl.BlockSpec((tq, q_cols), lambda b, g, i: (b * nq + i, BLK_NZ * LANES // q_cols + g)),
        ],
        out_specs=pl.BlockSpec((tq, q_cols), lambda b, g, i: (b * nq + i, g)),
        scratch_shapes=[
            pltpu.VMEM((nblkp // LANES * HEADS_PER_GROUP, tq, 2 * HEAD_DIM), BF16),
            pltpu.VMEM((HEADS_PER_GROUP, tq, 2 * HEAD_DIM), F32),
            pltpu.VMEM((HEADS_PER_GROUP, tq, LANES), F32),
            pltpu.VMEM((HEADS_PER_GROUP, tq, tk), BF16),
            pltpu.VMEM((HEADS_PER_GROUP, tq, LANES), F32),
        ],
        compiler_params=pltpu.CompilerParams(
            dimension_semantics=("arbitrary", "arbitrary", "arbitrary"), vmem_limit_bytes=VMEM_LIMIT),
        name="select",
    )(proj, selb, kaug, vaug, part, proj, proj)


def _outproj_kernel(yh_ref, yn_ref, w_ref, x_ref, g_ref, o_ref):
    half = yh_ref.shape[1]
    z = (jnp.dot(yh_ref[...], w_ref[:half, :], preferred_element_type=F32)
         + jnp.dot(yn_ref[...], w_ref[half:, :], preferred_element_type=F32))
    ms = jnp.mean(z * z, axis=-1, keepdims=True)
    o_ref[...] = x_ref[...] + z * lax.rsqrt(ms + NORM_EPS) * g_ref[...]


def _outproj(yh, yn, w, x2, g):
    n = x2.shape[0]
    tm = min(OUT_TM, n)
    return pl.pallas_call(
        _outproj_kernel,
        out_shape=jax.ShapeDtypeStruct((n, D_MODEL), F32),
        grid=(n // tm,),
        in_specs=[
            pl.BlockSpec((tm, yh.shape[1]), lambda i: (i, 0)),
            pl.BlockSpec((tm, yn.shape[1]), lambda i: (i, 0)),
            pl.BlockSpec(w.shape, lambda i: (0, 0)),
            pl.BlockSpec((tm, D_MODEL), lambda i: (i, 0)),
            pl.BlockSpec((1, D_MODEL), lambda i: (0, 0)),
        ],
        out_specs=pl.BlockSpec((tm, D_MODEL), lambda i: (i, 0)),
        compiler_params=pltpu.CompilerParams(
            dimension_semantics=("arbitrary",), vmem_limit_bytes=VMEM_LIMIT),
        name="outproj",
    )(yh, yn, w, x2, g)


def _permute_weights(w):
    def seg(name, width):
        return w[:, _SRC[name]:_SRC[name] + width]

    ng = seg("ng", NSA_HEADS * N_BRANCHES).reshape(-1, NSA_KV_GROUPS, HEADS_PER_GROUP, N_BRANCHES)
    ng = ng.transpose(0, 1, 3, 2).reshape(-1, NSA_KV_GROUPS, N_BRANCHES * HEADS_PER_GROUP)
    ng = jnp.pad(ng, ((0, 0), (0, 0), (0, LANES - N_BRANCHES * HEADS_PER_GROUP))).reshape(-1, NSA_KV_GROUPS * LANES)
    placed = [(BLK_HQ, seg("hq", 1024)), (BLK_HI, seg("hi", 1024)), (BLK_HZ, seg("hz", 1024)),
              (BLK_NQ, seg("nq", 1024) * (HEAD_DIM ** -0.5 * LOG2_E)),
              (BLK_KS, seg("ks", 256)), (BLK_KW, seg("kw", 256)), (BLK_KC, seg("kc", 256)),
              (BLK_VC, seg("vc", 256)), (BLK_VS, seg("vs", 256)), (BLK_VW, seg("vw", 256)),
              (BLK_NZ, seg("nz", 1024)), (BLK_NG, ng), (BLK_HF, seg("hf", 1024))]
    parts, col = [], 0
    for blk, cols in sorted(placed, key=lambda p: p[0]):
        if blk * LANES > col:
            parts.append(jnp.zeros((w.shape[0], blk * LANES - col), w.dtype))
        parts.append(cols)
        col = blk * LANES + cols.shape[1]
    if col < PROJ_COLS:
        parts.append(jnp.zeros((w.shape[0], PROJ_COLS - col), w.dtype))
    return jnp.concatenate(parts, axis=1).astype(BF16)


def _rope_tables(pos):
    half = ROT_DIM // 2
    inv = ROPE_THETA ** (-2.0 * jnp.arange(half, dtype=F32) / ROT_DIM)
    ang = pos.astype(F32)[:, None] * inv[None, :]
    cos, sin = jnp.cos(ang), jnp.sin(ang)
    n = pos.shape[0]
    pad = jnp.zeros((n, HEAD_DIM - ROT_DIM), F32)
    zero = jnp.zeros((n, half), F32)
    c = jnp.concatenate([cos, cos, pad + 1.0], axis=1)
    s1 = jnp.concatenate([-sin, zero, pad], axis=1)
    s2 = jnp.concatenate([zero, sin, pad], axis=1)
    return c, s1, s2


def _overlap_matrix(n_cmp, n_blocks, nblkp):
    start = np.arange(n_cmp)[:, None] * CMP_STRIDE
    sel = np.arange(nblkp)[None, :] * SEL_BLOCK
    ov = (start < sel + SEL_BLOCK) & (start + CMP_BLOCK > sel)
    ov &= np.arange(nblkp)[None, :] < n_blocks
    ov &= np.arange(n_cmp)[:, None] < (n_cmp - 1)
    return jnp.asarray(ov, BF16)


def kernel(x, pre_norm, post_norm, w_in, hgrn_lb_logits, hgrn_out_norm, cmp_pos_k, cmp_w1_k, cmp_b1_k,
           cmp_w2_k, cmp_pos_v, cmp_w1_v, cmp_b1_v, cmp_w2_v, w_out):
    batch, seq, _ = x.shape
    depth = w_in.shape[0]
    n_cmp = seq // CMP_STRIDE
    n_blocks = seq // SEL_BLOCK
    nblkp = -(-n_blocks // LANES) * LANES

    lb_probs = jax.nn.softmax(hgrn_lb_logits.astype(F32), axis=0)
    lower = jnp.maximum(jnp.cumsum(lb_probs, axis=0) - lb_probs[0:1], 0.0)
    lower = lower.reshape(depth, HGRN_HEADS, 1, HEAD_DIM)
    g_out = jnp.broadcast_to(hgrn_out_norm.astype(F32)[:, None, None, :], lower.shape)
    hgrn_par = jnp.concatenate(
        [jnp.log(lower), jnp.log1p(-lower), 1.0 - lower, g_out, jnp.zeros((depth, HGRN_HEADS, 4, HEAD_DIM), F32)],
        axis=2)

    rope_c, rope_s1, rope_s2 = _rope_tables(jnp.arange(seq))
    cmp_tab_k = jnp.stack(_rope_tables(jnp.arange(n_cmp) * CMP_STRIDE + CMP_BLOCK - 1))
    cmp_tab_v = jnp.stack([jnp.ones((n_cmp, HEAD_DIM), F32), jnp.zeros((n_cmp, HEAD_DIM), F32),
                           jnp.zeros((n_cmp, HEAD_DIM), F32)])
    cmp_tabs = jnp.stack([cmp_tab_k, cmp_tab_v])
    ov = _overlap_matrix(n_cmp, n_blocks, nblkp)
    tri = jnp.asarray(np.tril(np.ones((HGRN_CHUNK, HGRN_CHUNK), np.float32)))
    ones = jnp.ones((HEAD_DIM, HEAD_DIM), BF16)

    x2 = x.reshape(batch * seq, D_MODEL)
    for layer in range(depth):
        proj, proj_f32 = _proj(x2, pre_norm[layer][None, :], _permute_weights(w_in[layer]),
                               rope_c, rope_s1, rope_s2, seq)
        y_h = _hgrn(proj, proj_f32, hgrn_par[layer], tri, ones, batch, seq)
        cmp_kv = _compress(
            proj,
            jnp.stack([cmp_w1_k[layer], cmp_w1_v[layer]]).astype(BF16),
            jnp.stack([cmp_b1_k[layer], cmp_b1_v[layer]])[:, None, :],
            jnp.stack([cmp_w2_k[layer], cmp_w2_v[layer]]).astype(BF16),
            jnp.stack([cmp_pos_k[layer], cmp_pos_v[layer]]),
            cmp_tabs, batch, seq)
        kaug, vaug, kwb, vwb = _kvprep(proj, batch, seq)
        part, selb = _cmpwin(proj, cmp_kv, kwb, vwb, ov, batch, seq)
        y_n = _select(proj, selb, kaug, vaug, part, batch, seq)
        x2 = _outproj(y_h, y_n, w_out[layer].astype(BF16), x2, post_norm[layer][None, :])
    return x2.reshape(batch, seq, D_MODEL)
```

```python
import functools

import numpy as np
import jax
import jax.numpy as jnp
from jax import lax
from jax.experimental import pallas as pl
from jax.experimental.pallas import tpu as pltpu

F32 = jnp.float32
BF16 = jnp.bfloat16

D_MODEL = 1024
HEAD_DIM = 128
HGRN_HEADS = 8
NSA_HEADS = 8
NSA_KV_GROUPS = 2
HEADS_PER_GROUP = NSA_HEADS // NSA_KV_GROUPS
N_BRANCHES = 3
HGRN_CHUNK = 64
CMP_BLOCK = 32
CMP_STRIDE = 16
SEL_BLOCK = 64
N_SELECT = 16
WINDOW = 512
ROPE_THETA = 500000.0
ROT_DIM = HEAD_DIM // 4
NORM_EPS = 1e-6
FORCE_SCORE = 1e4
NEG_BIG = -1e30
LOG2_E = 1.4426950408889634

LANES = 128
SUBLANES = 8
BLK_HQ, BLK_HI, BLK_HZ = 0, 8, 16
BLK_NQ, BLK_KS, BLK_KW, BLK_KC, BLK_VC, BLK_VS, BLK_VW = 24, 32, 34, 36, 38, 40, 42
BLK_NZ, BLK_NG, BLK_HF = 44, 52, 56
PROJ_BLOCKS = 64
PROJ_COLS = PROJ_BLOCKS * LANES
PROJ_BF16_COLS = BLK_HF * LANES
_SRC = dict(hq=0, hf=1024, hi=2048, hz=3072, nq=4096, kc=5120, vc=5376, ks=5632, vs=5888,
            kw=6144, vw=6400, ng=6656, nz=6680)

PROJ_TM = 1024
PROJ_TN = 1024
ROPE_FULL_TILE = BLK_NQ * LANES // PROJ_TN
ROPE_HEAD_TILE = BLK_KS * LANES // PROJ_TN
ROPE_HEAD_BLOCKS = 4
F32_TILE = BLK_HF * LANES // PROJ_TN
HGRN_T = 512
HGRN_HEADS_PER_STEP = 2
PREP_T = 1024
ATT_TQ = 256
CMP_KEY_CHUNK = 256
SEL_TQ = 512
SEL_TK = 1024
OUT_TM = 512
VMEM_LIMIT = 56 * 1024 * 1024


def _nt_dot(a, b):
    return lax.dot_general(a, b, (((1,), (1,)), ((), ())), preferred_element_type=F32)


def _sigmoid(x):
    return 1.0 / (1.0 + jnp.exp(-x))


def _proj_kernel(x_ref, g_ref, w_ref, c_ref, s1_ref, s2_ref, o_ref, o32_ref, h_ref, acc_ref):
    j = pl.program_id(1)

    @pl.when(j == 0)
    def _():
        x = x_ref[...]
        ms = jnp.mean(x * x, axis=-1, keepdims=True)
        h_ref[...] = (x * lax.rsqrt(ms + NORM_EPS) * g_ref[...]).astype(BF16)

    @pl.when(j == F32_TILE)
    def _():
        o32_ref[...] = jnp.dot(h_ref[...], w_ref[...], preferred_element_type=F32)

    def store(n_rope_blocks):
        acc_ref[...] = jnp.dot(h_ref[...], w_ref[...], preferred_element_type=F32)
        for blk in range(n_rope_blocks):
            sl = slice(blk * LANES, (blk + 1) * LANES)
            a = acc_ref[:, sl]
            o_ref[:, sl] = (a * c_ref[...] + pltpu.roll(a, LANES - ROT_DIM // 2, 1) * s1_ref[...]
                            + pltpu.roll(a, ROT_DIM // 2, 1) * s2_ref[...]).astype(BF16)
        if n_rope_blocks * LANES < PROJ_TN:
            rest = slice(n_rope_blocks * LANES, PROJ_TN)
            o_ref[:, rest] = acc_ref[:, rest].astype(BF16)

    @pl.when(j == ROPE_FULL_TILE)
    def _():
        store(PROJ_TN // LANES)

    @pl.when(j == ROPE_HEAD_TILE)
    def _():
        store(ROPE_HEAD_BLOCKS)

    @pl.when((j != ROPE_FULL_TILE) & (j != ROPE_HEAD_TILE) & (j != F32_TILE))
    def _():
        o_ref[...] = jnp.dot(h_ref[...], w_ref[...], preferred_element_type=F32).astype(BF16)


def _proj(x2, g, w, rope_c, rope_s1, rope_s2, seq):
    n = x2.shape[0]
    tm = min(PROJ_TM, seq)
    pos_tiles = seq // tm
    return pl.pallas_call(
        _proj_kernel,
        out_shape=(jax.ShapeDtypeStruct((n, PROJ_BF16_COLS), BF16),
                   jax.ShapeDtypeStruct((n, PROJ_COLS - PROJ_BF16_COLS), F32)),
        grid=(n // tm, PROJ_COLS // PROJ_TN),
        in_specs=[
            pl.BlockSpec((tm, D_MODEL), lambda i, j: (i, 0)),
            pl.BlockSpec((1, D_MODEL), lambda i, j: (0, 0)),
            pl.BlockSpec((D_MODEL, PROJ_TN), lambda i, j: (0, j)),
            pl.BlockSpec((tm, LANES), lambda i, j: (i % pos_tiles, 0)),
            pl.BlockSpec((tm, LANES), lambda i, j: (i % pos_tiles, 0)),
            pl.BlockSpec((tm, LANES), lambda i, j: (i % pos_tiles, 0)),
        ],
        out_specs=(pl.BlockSpec((tm, PROJ_TN), lambda i, j: (i, jnp.minimum(j, F32_TILE - 1))),
                   pl.BlockSpec((tm, PROJ_TN), lambda i, j: (i, 0))),
        scratch_shapes=[pltpu.VMEM((tm, D_MODEL), BF16), pltpu.VMEM((tm, PROJ_TN), F32)],
        compiler_params=pltpu.CompilerParams(
            dimension_semantics=("arbitrary", "arbitrary"), vmem_limit_bytes=VMEM_LIMIT),
        name="proj",
    )(x2, g, w, rope_c, rope_s1, rope_s2)


def _hgrn_kernel(q_ref, f_ref, i_ref, z_ref, par_ref, tri_ref, lvl_ref, y_ref, st_ref, o_ref):
    for hh in range(q_ref.shape[1] // HEAD_DIM):
        cols = slice(hh * HEAD_DIM, (hh + 1) * HEAD_DIM)
        _hgrn_head(q_ref.at[:, cols], f_ref.at[:, cols], i_ref.at[:, cols], z_ref.at[:, cols],
                   par_ref.at[hh:hh + 1], tri_ref, lvl_ref, y_ref.at[:, cols], st_ref.at[hh], o_ref.at[hh])


def _hgrn_head(q_ref, f_ref, i_ref, z_ref, par_ref, tri_ref, lvl_ref, y_ref, st_ref, o_ref):
    @pl.when(pl.program_id(2) == 0)
    def _():
        st_ref[...] = jnp.zeros_like(st_ref)

    t = q_ref.shape[0]
    nc = t // HGRN_CHUNK
    shape3 = (nc, HGRN_CHUNK, HEAD_DIM)
    par = par_ref[0]
    log_lb, log_1m_lb, one_m_lb, g_norm = par[0:1], par[1:2], par[2:3], par[3:4]
    x = f_ref[...]
    e = jnp.exp(-jnp.abs(x))
    c = log_1m_lb + jnp.minimum(x, 0.0) - jnp.log(1.0 + e)
    lf = jnp.maximum(log_lb, c) + jnp.log(1.0 + jnp.exp(-jnp.abs(log_lb - c)))
    k3 = (one_m_lb * jnp.where(x >= 0.0, e, 1.0) / (1.0 + e)).reshape(shape3)
    q3 = q_ref[...].astype(F32).reshape(shape3)
    v3 = i_ref[...].astype(F32).reshape(shape3)
    v_bf = v3.astype(BF16)

    lf3 = (lf * LOG2_E).reshape(shape3)
    b3 = jnp.stack([jnp.dot(tri_ref[...], lf3[ci], precision=lax.Precision.HIGHEST,
                            preferred_element_type=F32) for ci in range(nc)])

    tiles = (t // SUBLANES, SUBLANES, HEAD_DIM)
    b_tile = b3.reshape(tiles)
    tile_row = lax.broadcasted_iota(jnp.int32, tiles, 1)
    lvl = lvl_ref[...]
    attn = jnp.where(lvl == 0, jnp.stack([_nt_dot(q3[ci].astype(BF16), k3[ci].astype(BF16)) for ci in range(nc)]), 0.0)
    n = HGRN_CHUNK // 2
    while n >= 1:
        if 2 * n >= SUBLANES:
            blocks = (t // (2 * n), 2 * n, HEAD_DIM)
            ref = jnp.broadcast_to(b3.reshape(blocks)[:, n - 1:n, :], blocks).reshape(shape3)
        else:
            ref = jnp.broadcast_to(b_tile[:, n - 1:n, :], tiles)
            for lo in range(2 * n, SUBLANES, 2 * n):
                ref = jnp.where(tile_row >= lo, jnp.broadcast_to(b_tile[:, lo + n - 1:lo + n, :], tiles), ref)
            ref = ref.reshape(shape3)
        q_n = (q3 * jnp.exp2(jnp.minimum(b3 - ref, 0.0))).astype(BF16)
        k_n = (k3 * jnp.exp2(jnp.minimum(ref - b3, 0.0))).astype(BF16)
        a_n = jnp.stack([_nt_dot(q_n[ci], k_n[ci]) for ci in range(nc)])
        attn = jnp.where(lvl == n, a_n, attn)
        n //= 2
    attn_bf = attn.astype(BF16)
    o_intra = jnp.stack([jnp.dot(attn_bf[ci], v_bf[ci], preferred_element_type=F32) for ci in range(nc)])

    b_last = b3[:, HGRN_CHUNK - 1:HGRN_CHUNK, :]
    decay = jnp.exp2(b_last)
    q_dec = (q3 * jnp.exp2(b3)).astype(BF16)
    k_dec = (k3 * jnp.exp2(b_last - b3)).astype(BF16)
    upd = [lax.dot_general(v_bf[ci], k_dec[ci], (((0,), (0,)), ((), ())), preferred_element_type=F32)
           for ci in range(nc)]
    st = st_ref[...]
    for ci in range(nc):
        o_ref[ci * HGRN_CHUNK:(ci + 1) * HGRN_CHUNK, :] = o_intra[ci] + _nt_dot(q_dec[ci], st.astype(BF16))
        st = st * decay[ci] + upd[ci]
    st_ref[...] = st

    o = o_ref[...]
    ms = jnp.mean(o * o, axis=-1, keepdims=True)
    z = z_ref[...].astype(F32)
    y_ref[...] = (o * lax.rsqrt(ms + NORM_EPS) * g_norm * (z * _sigmoid(z))).astype(BF16)


def _hgrn(proj, proj_f32, par, tri, lvl, batch, seq):
    t = min(HGRN_T, seq)
    nt = seq // t

    hp = HGRN_HEADS_PER_STEP
    width = hp * HEAD_DIM

    def col(base):
        return pl.BlockSpec((t, width), lambda b, h, ti: (b * nt + ti, base // hp + h))

    return pl.pallas_call(
        _hgrn_kernel,
        out_shape=jax.ShapeDtypeStruct((batch * seq, HGRN_HEADS * HEAD_DIM), BF16),
        grid=(batch, HGRN_HEADS // hp, nt),
        in_specs=[
            col(BLK_HQ), col(0), col(BLK_HI), col(BLK_HZ),
            pl.BlockSpec((hp, 8, HEAD_DIM), lambda b, h, ti: (h, 0, 0)),
            pl.BlockSpec((HGRN_CHUNK, HGRN_CHUNK), lambda b, h, ti: (0, 0)),
            pl.BlockSpec((HGRN_CHUNK, HGRN_CHUNK), lambda b, h, ti: (0, 0)),
        ],
        out_specs=pl.BlockSpec((t, width), lambda b, h, ti: (b * nt + ti, h)),
        scratch_shapes=[pltpu.VMEM((hp, HEAD_DIM, HEAD_DIM), F32), pltpu.VMEM((hp, t, HEAD_DIM), F32)],
        compiler_params=pltpu.CompilerParams(
            dimension_semantics=("arbitrary", "arbitrary", "arbitrary"), vmem_limit_bytes=VMEM_LIMIT),
        name="hgrn",
    )(proj, proj_f32, proj, proj, par, tri, lvl)


def _compress_kernel(x_ref, w1_ref, b1_ref, w2_ref, pos_ref, tab_ref, o_ref, x32_ref):
    n_rows = o_ref.shape[-2]
    x32_ref[...] = x_ref[...].astype(F32)
    acc_lo = jnp.zeros((n_rows, HEAD_DIM), F32)
    acc_hi = jnp.zeros((n_rows, HEAD_DIM), F32)
    for r in range(CMP_STRIDE):
        xr = x32_ref[pl.ds(r, n_rows, stride=CMP_STRIDE), :]
        lo, hi = r, CMP_STRIDE + r
        acc_lo = acc_lo + jnp.dot((xr + pos_ref[0, lo:lo + 1, :]).astype(BF16),
                                  w1_ref[0, lo * HEAD_DIM:(lo + 1) * HEAD_DIM, :],
                                  preferred_element_type=F32)
        acc_hi = acc_hi + jnp.dot((xr + pos_ref[0, hi:hi + 1, :]).astype(BF16),
                                  w1_ref[0, hi * HEAD_DIM:(hi + 1) * HEAD_DIM, :],
                                  preferred_element_type=F32)
    pre = acc_lo + pltpu.roll(acc_hi, n_rows - 1, 0) + b1_ref[0]
    mid = pre * _sigmoid(pre)
    out = jnp.dot(mid.astype(BF16), w2_ref[0], preferred_element_type=F32)
    o_ref[0, 0, 0] = (out * tab_ref[0, 0] + pltpu.roll(out, LANES - ROT_DIM // 2, 1) * tab_ref[0, 1]
                      + pltpu.roll(out, ROT_DIM // 2, 1) * tab_ref[0, 2]).astype(BF16)


def _compress(proj, w1, b1, w2, pos, tabs, batch, seq):
    n_rows = seq // CMP_STRIDE
    return pl.pallas_call(
        _compress_kernel,
        out_shape=jax.ShapeDtypeStruct((2, batch, NSA_KV_GROUPS, n_rows, HEAD_DIM), BF16),
        grid=(2, batch, NSA_KV_GROUPS),
        in_specs=[
            pl.BlockSpec((seq, HEAD_DIM), lambda kv, b, g: (b, BLK_KC + 2 * kv + g)),
            pl.BlockSpec((1, CMP_BLOCK * HEAD_DIM, HEAD_DIM), lambda kv, b, g: (kv, 0, 0)),
            pl.BlockSpec((1, 1, HEAD_DIM), lambda kv, b, g: (kv, 0, 0)),
            pl.BlockSpec((1, HEAD_DIM, HEAD_DIM), lambda kv, b, g: (kv, 0, 0)),
            pl.BlockSpec((1, CMP_BLOCK, HEAD_DIM), lambda kv, b, g: (kv, 0, 0)),
            pl.BlockSpec((1, 3, n_rows, HEAD_DIM), lambda kv, b, g: (kv, 0, 0, 0)),
        ],
        out_specs=pl.BlockSpec((1, 1, 1, n_rows, HEAD_DIM), lambda kv, b, g: (kv, b, g, 0, 0)),
        scratch_shapes=[pltpu.VMEM((seq, HEAD_DIM), F32)],
        compiler_params=pltpu.CompilerParams(
            dimension_semantics=("arbitrary", "arbitrary", "arbitrary"), vmem_limit_bytes=VMEM_LIMIT),
        name="compress",
    )(proj, w1, b1, w2, pos, tabs)


def _kvprep_kernel(ks_ref, kw_ref, vs_ref, vw_ref, kaug_ref, vaug_ref, kwb_ref, vwb_ref):
    t = ks_ref.shape[0]
    t0 = pl.program_id(2) * t
    row = lax.broadcasted_iota(jnp.int32, (t, LANES), 0) + t0
    col = lax.broadcasted_iota(jnp.int32, (t, LANES), 1)
    block_lane = lax.shift_right_logical(row, 6) & (LANES - 1)
    kaug_ref[0, 0, :, :HEAD_DIM] = ks_ref[...].astype(BF16)
    kaug_ref[0, 0, :, HEAD_DIM:] = jnp.where(block_lane == col, 1.0, 0.0).astype(BF16)
    vaug_ref[0, 0, :, :HEAD_DIM] = vs_ref[...].astype(BF16)
    vaug_ref[0, 0, :, HEAD_DIM:] = jnp.ones((t, LANES), BF16)
    kwb_ref[0, 0] = kw_ref[...].astype(BF16)
    vwb_ref[0, 0] = vw_ref[...].astype(BF16)


def _kvprep(proj, batch, seq):
    t = min(PREP_T, seq)
    nt = seq // t

    def col(base):
        return pl.BlockSpec((t, HEAD_DIM), lambda b, g, ti: (b * nt + ti, base + g))

    def out(width):
        return pl.BlockSpec((1, 1, t, width), lambda b, g, ti: (b, g, ti, 0))

    def shape(width):
        return jax.ShapeDtypeStruct((batch, NSA_KV_GROUPS, seq, width), BF16)

    return pl.pallas_call(
        _kvprep_kernel,
        out_shape=(shape(2 * HEAD_DIM), shape(2 * HEAD_DIM), shape(HEAD_DIM), shape(HEAD_DIM)),
        grid=(batch, NSA_KV_GROUPS, nt),
        in_specs=[col(BLK_KS), col(BLK_KW), col(BLK_VS), col(BLK_VW)],
        out_specs=(out(2 * HEAD_DIM), out(2 * HEAD_DIM), out(HEAD_DIM), out(HEAD_DIM)),
        compiler_params=pltpu.CompilerParams(
            dimension_semantics=("arbitrary", "arbitrary", "arbitrary"), vmem_limit_bytes=VMEM_LIMIT),
        name="kvprep",
    )(proj, proj, proj, proj)


def _cmpwin_kernel(q_ref, kc_ref, vc_ref, kw_ref, vw_ref, ng_ref, ov_ref, part_ref, selb_ref,
                   imp_ref, val_ref, sel_ref, *, n_blocks, win_len, key_chunk):
    tq = q_ref.shape[0]
    n_cmp = kc_ref.shape[-2]
    nblkp = selb_ref.shape[-1]
    q0 = pl.program_id(2) * tq
    gates = _sigmoid(ng_ref[...].astype(F32))
    q_heads = [q_ref[:, h * HEAD_DIM:(h + 1) * HEAD_DIM] for h in range(HEADS_PER_GROUP)]

    def attend(s, values_ones, vis, all_masked_possible):
        s = jnp.where(vis, s, NEG_BIG)
        m = jnp.max(s, axis=-1, keepdims=True)
        if all_masked_possible:
            m = jnp.where(m < 0.5 * NEG_BIG, 0.0, m)
        p = jnp.exp2(s - m)
        o = jnp.dot(p.astype(BF16), values_ones, preferred_element_type=F32)
        inv = 1.0 / jnp.maximum(o[:, HEAD_DIM:], 1e-30)
        return o[:, :HEAD_DIM] * inv, p, inv

    w0 = pl.multiple_of(jnp.maximum(q0 + tq - win_len, 0), tq)
    kwin = kw_ref[0, 0, pl.ds(w0, win_len), :]
    vwin = jnp.concatenate([vw_ref[0, 0, pl.ds(w0, win_len), :], jnp.ones((win_len, HEAD_DIM), BF16)], axis=1)
    t_w = lax.broadcasted_iota(jnp.int32, (tq, win_len), 0) + q0
    kp_w = lax.broadcasted_iota(jnp.int32, (tq, win_len), 1) + w0
    vis_w = (kp_w <= t_w) & (t_w - kp_w < WINDOW)
    s_w = [_nt_dot(qh, kwin) for qh in q_heads]
    for h in range(HEADS_PER_GROUP):
        o_w, _, _ = attend(s_w[h], vwin, vis_w, False)
        g_w = gates[:, 2 * HEADS_PER_GROUP + h:2 * HEADS_PER_GROUP + h + 1]
        part_ref[:, h * HEAD_DIM:(h + 1) * HEAD_DIM] = g_w * o_w

    def compressed(n_keys):
        kc = kc_ref[0, 0, 0, :n_keys, :]
        vc = jnp.concatenate([vc_ref[0, 0, 0, :n_keys, :], jnp.ones((n_keys, HEAD_DIM), BF16)], axis=1)
        t_c = lax.broadcasted_iota(jnp.int32, (tq, n_keys), 0) + q0
        n_c = lax.broadcasted_iota(jnp.int32, (tq, n_keys), 1)
        vis_c = n_c * CMP_STRIDE + (CMP_BLOCK - 1) <= t_c
        p_sum = jnp.zeros((tq, n_keys), F32)
        s_c = [_nt_dot(qh, kc) for qh in q_heads]
        for h in range(HEADS_PER_GROUP):
            sl = slice(h * HEAD_DIM, (h + 1) * HEAD_DIM)
            o_c, p, inv = attend(s_c[h], vc, vis_c, True)
            p_sum = p_sum + p * jnp.concatenate([inv] * (n_keys // LANES), axis=1)
            g_c = gates[:, 0 * HEADS_PER_GROUP + h:0 * HEADS_PER_GROUP + h + 1]
            part_ref[:, sl] = part_ref[:, sl] + g_c * o_c
        p_hi = p_sum.astype(BF16)
        p_lo = (p_sum - p_hi.astype(F32)).astype(BF16)
        imp_ref[...] = (jnp.dot(p_hi, ov_ref[:n_keys, :], preferred_element_type=F32)
                        + jnp.dot(p_lo, ov_ref[:n_keys, :], preferred_element_type=F32))

    n_visible = (q0 + tq - CMP_BLOCK) // CMP_STRIDE + 1
    n_chunks = jnp.minimum((n_visible + key_chunk - 1) // key_chunk, n_cmp // key_chunk)
    for c in range(1, n_cmp // key_chunk + 1):
        @pl.when(n_chunks == c)
        def _():
            compressed(c * key_chunk)

    imp_t = imp_ref[...].T
    t_b = lax.broadcasted_iota(jnp.int32, (nblkp, tq), 1) + q0
    blk = lax.broadcasted_iota(jnp.int32, (nblkp, tq), 0)
    cur = lax.shift_right_logical(t_b, 6)
    forced = (blk == 0) | (blk == cur) | (blk == cur - 1)
    val = jnp.where(forced, -4.0, jnp.where(blk * SEL_BLOCK <= t_b, imp_t, -1.0))
    val_ref[...] = jnp.where(blk < n_blocks, val, -3.0)
    sel_ref[...] = jnp.where(forced, 1.0, 0.0)
    blk_f = blk.astype(F32)
    n_top = min(N_SELECT, n_blocks)
    quota = n_top - 1 - jnp.minimum(cur[0:1, :], 2)

    def pick(it, carry):
        v = val_ref[...]
        m = jnp.max(v, axis=0, keepdims=True)
        idx = jnp.min(jnp.where(v == m, blk_f, float(nblkp)), axis=0, keepdims=True)
        hit = (blk_f == idx) & (it < quota)
        sel_ref[...] = jnp.where(hit, 1.0, sel_ref[...])
        val_ref[...] = jnp.where(hit, -4.0, v)
        return carry

    lax.fori_loop(0, n_top - 1 - jnp.minimum(lax.shift_right_logical(q0, 6), 2), pick, 0)
    selb_ref[0, 0] = jnp.where(sel_ref[...].T > 0.5, 0.0, NEG_BIG).astype(BF16)


def _cmpwin(proj, cmp_kv, kwb, vwb, ov, batch, seq):
    tq = min(ATT_TQ, seq)
    nq = seq // tq
    n_cmp = seq // CMP_STRIDE
    n_blocks = seq // SEL_BLOCK
    nblkp = ov.shape[1]
    win_len = min(WINDOW + tq, seq)
    q_cols = HEADS_PER_GROUP * HEAD_DIM
    key_chunk = min(CMP_KEY_CHUNK, n_cmp)
    kernel = functools.partial(_cmpwin_kernel, n_blocks=n_blocks, win_len=win_len, key_chunk=key_chunk)
    return pl.pallas_call(
        kernel,
        out_shape=(jax.ShapeDtypeStruct((batch * seq, NSA_HEADS * HEAD_DIM), F32),
                   jax.ShapeDtypeStruct((batch, NSA_KV_GROUPS, seq, nblkp), BF16)),
        grid=(batch, NSA_KV_GROUPS, nq),
        in_specs=[
            pl.BlockSpec((tq, q_cols), lambda b, g, i: (b * nq + i, BLK_NQ * LANES // q_cols + g)),
            pl.BlockSpec((1, 1, 1, n_cmp, HEAD_DIM), lambda b, g, i: (0, b, g, 0, 0)),
            pl.BlockSpec((1, 1, 1, n_cmp, HEAD_DIM), lambda b, g, i: (1, b, g, 0, 0)),
            pl.BlockSpec((1, 1, seq, HEAD_DIM), lambda b, g, i: (b, g, 0, 0)),
            pl.BlockSpec((1, 1, seq, HEAD_DIM), lambda b, g, i: (b, g, 0, 0)),
            pl.BlockSpec((tq, LANES), lambda b, g, i: (b * nq + i, BLK_NG + g)),
            pl.BlockSpec((n_cmp, nblkp), lambda b, g, i: (0, 0)),
        ],
        out_specs=(pl.BlockSpec((tq, q_cols), lambda b, g, i: (b * nq + i, g)),
                   pl.BlockSpec((1, 1, tq, nblkp), lambda b, g, i: (b, g, i, 0))),
        scratch_shapes=[pltpu.VMEM((tq, nblkp), F32), pltpu.VMEM((nblkp, tq), F32),
                        pltpu.VMEM((nblkp, tq), F32)],
        compiler_params=pltpu.CompilerParams(
            dimension_semantics=("arbitrary", "arbitrary", "arbitrary"), vmem_limit_bytes=VMEM_LIMIT),
        name="cmpwin",
    )(proj, cmp_kv, cmp_kv, kwb, vwb, proj, ov)


def _select_kernel(q_ref, selb_ref, kaug_ref, vaug_ref, part_ref, ng_ref, nz_ref, y_ref,
                   qaug_ref, acc_ref, m_ref, p_ref, alpha_ref, *, tk):
    tq = q_ref.shape[0]
    n_halves = selb_ref.shape[-1] // LANES
    q0 = pl.program_id(2) * tq

    for h in range(HEADS_PER_GROUP):
        qh = q_ref[:, h * HEAD_DIM:(h + 1) * HEAD_DIM]
        for half in range(n_halves):
            qaug_ref[half * HEADS_PER_GROUP + h, :, :HEAD_DIM] = qh
            qaug_ref[half * HEADS_PER_GROUP + h, :, HEAD_DIM:] = selb_ref[0, 0, :, half * LANES:(half + 1) * LANES]
    acc_ref[...] = jnp.zeros_like(acc_ref)
    m_ref[...] = jnp.full_like(m_ref, NEG_BIG)

    def tile_step(kt, pending, causal):
        if kt is not None:
            k0 = pl.multiple_of(kt * tk, tk)
            k_t = kaug_ref[0, 0, pl.ds(k0, tk), :]
            half = k0 // (SEL_BLOCK * LANES)
            if causal:
                t_pos = lax.broadcasted_iota(jnp.int32, (tq, tk), 0) + q0
                k_pos = lax.broadcasted_iota(jnp.int32, (tq, tk), 1) + k0
                vis = k_pos <= t_pos
        if pending is not None:
            v_t = vaug_ref[0, 0, pl.ds(pl.multiple_of(pending * tk, tk), tk), :]
        if kt is not None:
            s_next = _nt_dot(qaug_ref[half * HEADS_PER_GROUP], k_t)
        for h in range(HEADS_PER_GROUP):
            if kt is not None:
                s = s_next
                if h + 1 < HEADS_PER_GROUP:
                    s_next = _nt_dot(qaug_ref[half * HEADS_PER_GROUP + h + 1], k_t)
            if pending is not None:
                alpha = alpha_ref[h]
                acc_ref[h] = (acc_ref[h] * jnp.concatenate([alpha, alpha], axis=1)
                              + jnp.dot(p_ref[h], v_t, preferred_element_type=F32))
            if kt is not None:
                if causal:
                    s = jnp.where(vis, s, NEG_BIG)
                m_prev = m_ref[h]
                m_new = jnp.maximum(m_prev, jnp.max(s, axis=-1, keepdims=True))
                p_ref[h] = jnp.exp2(s - jnp.concatenate([m_new] * (tk // LANES), axis=1)).astype(BF16)
                alpha_ref[h] = jnp.exp2(m_prev - m_new)
                m_ref[h] = m_new

    n_full = q0 // tk

    @pl.when(n_full > 0)
    def _():
        tile_step(0, None, False)

    @pl.when(n_full == 0)
    def _():
        p_ref[...] = jnp.zeros_like(p_ref)
        alpha_ref[...] = jnp.ones_like(alpha_ref)

    def step(kt, carry):
        tile_step(kt, kt - 1, False)
        return carry

    lax.fori_loop(1, n_full, step, 0)
    tile_step(n_full, jnp.maximum(n_full - 1, 0), True)
    tile_step(None, n_full, False)

    gates = _sigmoid(ng_ref[...].astype(F32))
    for h in range(HEADS_PER_GROUP):
        sl = slice(h * HEAD_DIM, (h + 1) * HEAD_DIM)
        acc = acc_ref[h]
        o_s = acc[:, :HEAD_DIM] / jnp.maximum(acc[:, HEAD_DIM:], 1e-30)
        g_s = gates[:, 1 * HEADS_PER_GROUP + h:1 * HEADS_PER_GROUP + h + 1]
        z = nz_ref[:, sl].astype(F32)
        y_ref[:, sl] = ((part_ref[:, sl] + g_s * o_s) * (z * _sigmoid(z))).astype(BF16)


def _select(proj, selb, kaug, vaug, part, batch, seq):
    tq = min(SEL_TQ, seq)
    tk = min(SEL_TK, seq)
    nq = seq // tq
    nblkp = selb.shape[-1]
    q_cols = HEADS_PER_GROUP * HEAD_DIM
    kernel = functools.partial(_select_kernel, tk=tk)
    return pl.pallas_call(
        kernel,
        out_shape=jax.ShapeDtypeStruct((batch * seq, NSA_HEADS * HEAD_DIM), BF16),
        grid=(batch, NSA_KV_GROUPS, nq),
        in_specs=[
            pl.BlockSpec((tq, q_cols), lambda b, g, i: (b * nq + i, BLK_NQ * LANES // q_cols + g)),
            pl.BlockSpec((1, 1, tq, nblkp), lambda b, g, i: (b, g, i, 0)),
            pl.BlockSpec((1, 1, seq, 2 * HEAD_DIM), lambda b, g, i: (b, g, 0, 0), pipeline_mode=pl.Buffered(1)),
            pl.BlockSpec((1, 1, seq, 2 * HEAD_DIM), lambda b, g, i: (b, g, 0, 0), pipeline_mode=pl.Buffered(1)),
            pl.BlockSpec((tq, q_cols), lambda b, g, i: (b * nq + i, g)),
            pl.BlockSpec((tq, LANES), lambda b, g, i: (b * nq + i, BLK_NG + g)),
            pl.BlockSpec((tq, q_cols), lambda b, g, i: (b * nq + i, BLK_NZ * LANES // q_cols + g)),
        ],
        out_specs=pl.BlockSpec((tq, q_cols), lambda b, g, i: (b * nq + i, g)),
        scratch_shapes=[
            pltpu.VMEM((nblkp // LANES * HEADS_PER_GROUP, tq, 2 * HEAD_DIM), BF16),
            pltpu.VMEM((HEADS_PER_GROUP, tq, 2 * HEAD_DIM), F32),
            pltpu.VMEM((HEADS_PER_GROUP, tq, LANES), F32),
            pltpu.VMEM((HEADS_PER_GROUP, tq, tk), BF16),
            pltpu.VMEM((HEADS_PER_GROUP, tq, LANES), F32),
        ],
        compiler_params=pltpu.CompilerParams(
            dimension_semantics=("arbitrary", "arbitrary", "arbitrary"), vmem_limit_bytes=VMEM_LIMIT),
        name="select",
    )(proj, selb, kaug, vaug, part, proj, proj)


def _outproj_kernel(yh_ref, yn_ref, w_ref, x_ref, g_ref, o_ref):
    half = yh_ref.shape[1]
    z = (jnp.dot(yh_ref[...], w_ref[:half, :], preferred_element_type=F32)
         + jnp.dot(yn_ref[...], w_ref[half:, :], preferred_element_type=F32))
    ms = jnp.mean(z * z, axis=-1, keepdims=True)
    o_ref[...] = x_ref[...] + z * lax.rsqrt(ms + NORM_EPS) * g_ref[...]


def _outproj(yh, yn, w, x2, g):
    n = x2.shape[0]
    tm = min(OUT_TM, n)
    return pl.pallas_call(
        _outproj_kernel,
        out_shape=jax.ShapeDtypeStruct((n, D_MODEL), F32),
        grid=(n // tm,),
        in_specs=[
            pl.BlockSpec((tm, yh.shape[1]), lambda i: (i, 0)),
            pl.BlockSpec((tm, yn.shape[1]), lambda i: (i, 0)),
            pl.BlockSpec(w.shape, lambda i: (0, 0)),
            pl.BlockSpec((tm, D_MODEL), lambda i: (i, 0)),
            pl.BlockSpec((1, D_MODEL), lambda i: (0, 0)),
        ],
        out_specs=pl.BlockSpec((tm, D_MODEL), lambda i: (i, 0)),
        compiler_params=pltpu.CompilerParams(
            dimension_semantics=("arbitrary",), vmem_limit_bytes=VMEM_LIMIT),
        name="outproj",
    )(yh, yn, w, x2, g)


def _permute_weights(w):
    def seg(name, width):
        return w[:, _SRC[name]:_SRC[name] + width]

    ng = seg("ng", NSA_HEADS * N_BRANCHES).reshape(-1, NSA_KV_GROUPS, HEADS_PER_GROUP, N_BRANCHES)
    ng = ng.transpose(0, 1, 3, 2).reshape(-1, NSA_KV_GROUPS, N_BRANCHES * HEADS_PER_GROUP)
    ng = jnp.pad(ng, ((0, 0), (0, 0), (0, LANES - N_BRANCHES * HEADS_PER_GROUP))).reshape(-1, NSA_KV_GROUPS * LANES)
    placed = [(BLK_HQ, seg("hq", 1024)), (BLK_HI, seg("hi", 1024)), (BLK_HZ, seg("hz", 1024)),
              (BLK_NQ, seg("nq", 1024) * (HEAD_DIM ** -0.5 * LOG2_E)),
              (BLK_KS, seg("ks", 256)), (BLK_KW, seg("kw", 256)), (BLK_KC, seg("kc", 256)),
              (BLK_VC, seg("vc", 256)), (BLK_VS, seg("vs", 256)), (BLK_VW, seg("vw", 256)),
              (BLK_NZ, seg("nz", 1024)), (BLK_NG, ng), (BLK_HF, seg("hf", 1024))]
    parts, col = [], 0
    for blk, cols in sorted(placed, key=lambda p: p[0]):
        if blk * LANES > col:
            parts.append(jnp.zeros((w.shape[0], blk * LANES - col), w.dtype))
        parts.append(cols)
        col = blk * LANES + cols.shape[1]
    if col < PROJ_COLS:
        parts.append(jnp.zeros((w.shape[0], PROJ_COLS - col), w.dtype))
    return jnp.concatenate(parts, axis=1).astype(BF16)


def _rope_tables(pos):
    half = ROT_DIM // 2
    inv = ROPE_THETA ** (-2.0 * jnp.arange(half, dtype=F32) / ROT_DIM)
    ang = pos.astype(F32)[:, None] * inv[None, :]
    cos, sin = jnp.cos(ang), jnp.sin(ang)
    n = pos.shape[0]
    pad = jnp.zeros((n, HEAD_DIM - ROT_DIM), F32)
    zero = jnp.zeros((n, half), F32)
    c = jnp.concatenate([cos, cos, pad + 1.0], axis=1)
    s1 = jnp.concatenate([-sin, zero, pad], axis=1)
    s2 = jnp.concatenate([zero, sin, pad], axis=1)
    return c, s1, s2


def _pair_levels():
    t = np.arange(HGRN_CHUNK)[:, None]
    s = np.arange(HGRN_CHUNK)[None, :]
    high_bit = 2 ** np.floor(np.log2(np.maximum(t ^ s, 1))).astype(np.int64)
    return jnp.asarray(np.where(t > s, high_bit, np.where(t == s, 0, -1)), jnp.int32)


def _overlap_matrix(n_cmp, n_blocks, nblkp):
    start = np.arange(n_cmp)[:, None] * CMP_STRIDE
    sel = np.arange(nblkp)[None, :] * SEL_BLOCK
    ov = (start < sel + SEL_BLOCK) & (start + CMP_BLOCK > sel)
    ov &= np.arange(nblkp)[None, :] < n_blocks
    ov &= np.arange(n_cmp)[:, None] < (n_cmp - 1)
    return jnp.asarray(ov, BF16)


def kernel(x, pre_norm, post_norm, w_in, hgrn_lb_logits, hgrn_out_norm, cmp_pos_k, cmp_w1_k, cmp_b1_k,
           cmp_w2_k, cmp_pos_v, cmp_w1_v, cmp_b1_v, cmp_w2_v, w_out):
    batch, seq, _ = x.shape
    depth = w_in.shape[0]
    n_cmp = seq // CMP_STRIDE
    n_blocks = seq // SEL_BLOCK
    nblkp = -(-n_blocks // LANES) * LANES

    lb_probs = jax.nn.softmax(hgrn_lb_logits.astype(F32), axis=0)
    lower = jnp.maximum(jnp.cumsum(lb_probs, axis=0) - lb_probs[0:1], 0.0)
    lower = lower.reshape(depth, HGRN_HEADS, 1, HEAD_DIM)
    g_out = jnp.broadcast_to(hgrn_out_norm.astype(F32)[:, None, None, :], lower.shape)
    hgrn_par = jnp.concatenate(
        [jnp.log(lower), jnp.log1p(-lower), 1.0 - lower, g_out, jnp.zeros((depth, HGRN_HEADS, 4, HEAD_DIM), F32)],
        axis=2)

    rope_c, rope_s1, rope_s2 = _rope_tables(jnp.arange(seq))
    cmp_tab_k = jnp.stack(_rope_tables(jnp.arange(n_cmp) * CMP_STRIDE + CMP_BLOCK - 1))
    cmp_tab_v = jnp.stack([jnp.ones((n_cmp, HEAD_DIM), F32), jnp.zeros((n_cmp, HEAD_DIM), F32),
                           jnp.zeros((n_cmp, HEAD_DIM), F32)])
    cmp_tabs = jnp.stack([cmp_tab_k, cmp_tab_v])
    ov = _overlap_matrix(n_cmp, n_blocks, nblkp)
    tri = jnp.asarray(np.tril(np.ones((HGRN_CHUNK, HGRN_CHUNK), np.float32)))
    lvl = _pair_levels()

    x2 = x.reshape(batch * seq, D_MODEL)
    for layer in range(depth):
        proj, proj_f32 = _proj(x2, pre_norm[layer][None, :], _permute_weights(w_in[layer]),
                               rope_c, rope_s1, rope_s2, seq)
        y_h = _hgrn(proj, proj_f32, hgrn_par[layer], tri, lvl, batch, seq)
        cmp_kv = _compress(
            proj,
            jnp.stack([cmp_w1_k[layer], cmp_w1_v[layer]]).astype(BF16),
            jnp.stack([cmp_b1_k[layer], cmp_b1_v[layer]])[:, None, :],
            jnp.stack([cmp_w2_k[layer], cmp_w2_v[layer]]).astype(BF16),
            jnp.stack([cmp_pos_k[layer], cmp_pos_v[layer]]),
            cmp_tabs, batch, seq)
        kaug, vaug, kwb, vwb = _kvprep(proj, batch, seq)
        part, selb = _cmpwin(proj, cmp_kv, kwb, vwb, ov, batch, seq)
        y_n = _select(proj, selb, kaug, vaug, part, batch, seq)
        x2 = _outproj(y_h, y_n, w_out[layer].astype(BF16), x2, post_norm[layer][None, :])
    return x2.reshape(batch, seq, D_MODEL)
```

```python
import functools

import numpy as np
import jax
import jax.numpy as jnp
from jax import lax
from jax.experimental import pallas as pl
from jax.experimental.pallas import tpu as pltpu

F32 = jnp.float32
BF16 = jnp.bfloat16

D_MODEL = 1024
HEAD_DIM = 128
HGRN_HEADS = 8
NSA_HEADS = 8
NSA_KV_GROUPS = 2
HEADS_PER_GROUP = NSA_HEADS // NSA_KV_GROUPS
N_BRANCHES = 3
HGRN_CHUNK = 64
CMP_BLOCK = 32
CMP_STRIDE = 16
SEL_BLOCK = 64
N_SELECT = 16
WINDOW = 512
ROPE_THETA = 500000.0
ROT_DIM = HEAD_DIM // 4
NORM_EPS = 1e-6
TAKEN = -4.0
NEG_BIG = -1e30
LOG2_E = 1.4426950408889634

LANES = 128
SUBLANES = 8
BLK_HQ, BLK_HI, BLK_HZ = 0, 8, 16
BLK_NQ, BLK_KS, BLK_KW, BLK_KC, BLK_VC, BLK_VS, BLK_VW = 24, 32, 34, 36, 38, 40, 42
BLK_NZ, BLK_NG, BLK_HF = 44, 52, 56
PROJ_BLOCKS = 64
PROJ_COLS = PROJ_BLOCKS * LANES
PROJ_BF16_COLS = BLK_HF * LANES
_SRC = dict(hq=0, hf=1024, hi=2048, hz=3072, nq=4096, kc=5120, vc=5376, ks=5632, vs=5888,
            kw=6144, vw=6400, ng=6656, nz=6680)

PROJ_TM = 1024
PROJ_TN = 1024
ROPE_FULL_TILE = BLK_NQ * LANES // PROJ_TN
ROPE_HEAD_TILE = BLK_KS * LANES // PROJ_TN
ROPE_HEAD_BLOCKS = 4
F32_TILE = BLK_HF * LANES // PROJ_TN
HGRN_T = 512
HGRN_HEADS_PER_STEP = 2
PREP_T = 1024
ATT_TQ = 256
CMP_KEY_CHUNK = 256
SEL_TQ = 512
SEL_TK = 1024
OUT_TM = 512
VMEM_LIMIT = 56 * 1024 * 1024


def _nt_dot(a, b):
    return lax.dot_general(a, b, (((1,), (1,)), ((), ())), preferred_element_type=F32)


def _sigmoid(x):
    return 1.0 / (1.0 + jnp.exp(-x))


def _proj_kernel(x_ref, g_ref, w_ref, c_ref, s1_ref, s2_ref, o_ref, o32_ref, h_ref, acc_ref):
    j = pl.program_id(1)

    @pl.when(j == 0)
    def _():
        x = x_ref[...]
        ms = jnp.mean(x * x, axis=-1, keepdims=True)
        h_ref[...] = (x * lax.rsqrt(ms + NORM_EPS) * g_ref[...]).astype(BF16)

    @pl.when(j == F32_TILE)
    def _():
        o32_ref[...] = jnp.dot(h_ref[...], w_ref[...], preferred_element_type=F32)

    def store(n_rope_blocks):
        acc_ref[...] = jnp.dot(h_ref[...], w_ref[...], preferred_element_type=F32)
        for blk in range(n_rope_blocks):
            sl = slice(blk * LANES, (blk + 1) * LANES)
            a = acc_ref[:, sl]
            o_ref[:, sl] = (a * c_ref[...] + pltpu.roll(a, LANES - ROT_DIM // 2, 1) * s1_ref[...]
                            + pltpu.roll(a, ROT_DIM // 2, 1) * s2_ref[...]).astype(BF16)
        if n_rope_blocks * LANES < PROJ_TN:
            rest = slice(n_rope_blocks * LANES, PROJ_TN)
            o_ref[:, rest] = acc_ref[:, rest].astype(BF16)

    @pl.when(j == ROPE_FULL_TILE)
    def _():
        store(PROJ_TN // LANES)

    @pl.when(j == ROPE_HEAD_TILE)
    def _():
        store(ROPE_HEAD_BLOCKS)

    @pl.when((j != ROPE_FULL_TILE) & (j != ROPE_HEAD_TILE) & (j != F32_TILE))
    def _():
        o_ref[...] = jnp.dot(h_ref[...], w_ref[...], preferred_element_type=F32).astype(BF16)


def _proj(x2, g, w, rope_c, rope_s1, rope_s2, seq):
    n = x2.shape[0]
    tm = min(PROJ_TM, seq)
    pos_tiles = seq // tm
    return pl.pallas_call(
        _proj_kernel,
        out_shape=(jax.ShapeDtypeStruct((n, PROJ_BF16_COLS), BF16),
                   jax.ShapeDtypeStruct((n, PROJ_COLS - PROJ_BF16_COLS), F32)),
        grid=(n // tm, PROJ_COLS // PROJ_TN),
        in_specs=[
            pl.BlockSpec((tm, D_MODEL), lambda i, j: (i, 0)),
            pl.BlockSpec((1, D_MODEL), lambda i, j: (0, 0)),
            pl.BlockSpec((D_MODEL, PROJ_TN), lambda i, j: (0, j)),
            pl.BlockSpec((tm, LANES), lambda i, j: (i % pos_tiles, 0)),
            pl.BlockSpec((tm, LANES), lambda i, j: (i % pos_tiles, 0)),
            pl.BlockSpec((tm, LANES), lambda i, j: (i % pos_tiles, 0)),
        ],
        out_specs=(pl.BlockSpec((tm, PROJ_TN), lambda i, j: (i, jnp.minimum(j, F32_TILE - 1))),
                   pl.BlockSpec((tm, PROJ_TN), lambda i, j: (i, 0))),
        scratch_shapes=[pltpu.VMEM((tm, D_MODEL), BF16), pltpu.VMEM((tm, PROJ_TN), F32)],
        compiler_params=pltpu.CompilerParams(
            dimension_semantics=("arbitrary", "arbitrary"), vmem_limit_bytes=VMEM_LIMIT),
        name="proj",
    )(x2, g, w, rope_c, rope_s1, rope_s2)


def _hgrn_kernel(q_ref, f_ref, i_ref, z_ref, par_ref, tri_ref, lvl_ref, y_ref, st_ref, o_ref):
    for hh in range(q_ref.shape[1] // HEAD_DIM):
        cols = slice(hh * HEAD_DIM, (hh + 1) * HEAD_DIM)
        _hgrn_head(q_ref.at[:, cols], f_ref.at[:, cols], i_ref.at[:, cols], z_ref.at[:, cols],
                   par_ref.at[hh:hh + 1], tri_ref, lvl_ref, y_ref.at[:, cols], st_ref.at[hh], o_ref.at[hh])


def _hgrn_head(q_ref, f_ref, i_ref, z_ref, par_ref, tri_ref, lvl_ref, y_ref, st_ref, o_ref):
    @pl.when(pl.program_id(2) == 0)
    def _():
        st_ref[...] = jnp.zeros_like(st_ref)

    t = q_ref.shape[0]
    nc = t // HGRN_CHUNK
    shape3 = (nc, HGRN_CHUNK, HEAD_DIM)
    par = par_ref[0]
    log_lb, log_1m_lb, one_m_lb, g_norm = par[0:1], par[1:2], par[2:3], par[3:4]
    x = f_ref[...]
    e = jnp.exp(-jnp.abs(x))
    c = log_1m_lb + jnp.minimum(x, 0.0) - jnp.log(1.0 + e)
    lf = jnp.maximum(log_lb, c) + jnp.log(1.0 + jnp.exp(-jnp.abs(log_lb - c)))
    k3 = (one_m_lb * jnp.where(x >= 0.0, e, 1.0) / (1.0 + e)).reshape(shape3)
    q3 = q_ref[...].astype(F32).reshape(shape3)
    v3 = i_ref[...].astype(F32).reshape(shape3)
    v_bf = v3.astype(BF16)

    lf3 = (lf * LOG2_E).reshape(shape3)
    b3 = jnp.stack([jnp.dot(tri_ref[...], lf3[ci], precision=lax.Precision.HIGHEST,
                            preferred_element_type=F32) for ci in range(nc)])

    tiles = (t // SUBLANES, SUBLANES, HEAD_DIM)
    b_tile = b3.reshape(tiles)
    tile_row = lax.broadcasted_iota(jnp.int32, tiles, 1)
    lvl = lvl_ref[...]
    attn = jnp.where(lvl == 0, jnp.stack([_nt_dot(q3[ci].astype(BF16), k3[ci].astype(BF16)) for ci in range(nc)]), 0.0)
    n = HGRN_CHUNK // 2
    while n >= 1:
        if 2 * n >= SUBLANES:
            blocks = (t // (2 * n), 2 * n, HEAD_DIM)
            ref = jnp.broadcast_to(b3.reshape(blocks)[:, n - 1:n, :], blocks).reshape(shape3)
        else:
            ref = jnp.broadcast_to(b_tile[:, n - 1:n, :], tiles)
            for lo in range(2 * n, SUBLANES, 2 * n):
                ref = jnp.where(tile_row >= lo, jnp.broadcast_to(b_tile[:, lo + n - 1:lo + n, :], tiles), ref)
            ref = ref.reshape(shape3)
        q_n = (q3 * jnp.exp2(jnp.minimum(b3 - ref, 0.0))).astype(BF16)
        k_n = (k3 * jnp.exp2(jnp.minimum(ref - b3, 0.0))).astype(BF16)
        a_n = jnp.stack([_nt_dot(q_n[ci], k_n[ci]) for ci in range(nc)])
        attn = jnp.where(lvl == n, a_n, attn)
        n //= 2
    attn_bf = attn.astype(BF16)
    o_intra = jnp.stack([jnp.dot(attn_bf[ci], v_bf[ci], preferred_element_type=F32) for ci in range(nc)])

    b_last = b3[:, HGRN_CHUNK - 1:HGRN_CHUNK, :]
    decay = jnp.exp2(b_last)
    q_dec = (q3 * jnp.exp2(b3)).astype(BF16)
    k_dec = (k3 * jnp.exp2(b_last - b3)).astype(BF16)
    upd = [lax.dot_general(v_bf[ci], k_dec[ci], (((0,), (0,)), ((), ())), preferred_element_type=F32)
           for ci in range(nc)]
    st = st_ref[...]
    for ci in range(nc):
        o_ref[ci * HGRN_CHUNK:(ci + 1) * HGRN_CHUNK, :] = o_intra[ci] + _nt_dot(q_dec[ci], st.astype(BF16))
        st = st * decay[ci] + upd[ci]
    st_ref[...] = st

    o = o_ref[...]
    ms = jnp.mean(o * o, axis=-1, keepdims=True)
    z = z_ref[...].astype(F32)
    y_ref[...] = (o * lax.rsqrt(ms + NORM_EPS) * g_norm * (z * _sigmoid(z))).astype(BF16)


def _hgrn(proj, proj_f32, par, tri, lvl, batch, seq):
    t = min(HGRN_T, seq)
    nt = seq // t

    hp = HGRN_HEADS_PER_STEP
    width = hp * HEAD_DIM

    def col(base):
        return pl.BlockSpec((t, width), lambda b, h, ti: (b * nt + ti, base // hp + h))

    return pl.pallas_call(
        _hgrn_kernel,
        out_shape=jax.ShapeDtypeStruct((batch * seq, HGRN_HEADS * HEAD_DIM), BF16),
        grid=(batch, HGRN_HEADS // hp, nt),
        in_specs=[
            col(BLK_HQ), col(0), col(BLK_HI), col(BLK_HZ),
            pl.BlockSpec((hp, 8, HEAD_DIM), lambda b, h, ti: (h, 0, 0)),
            pl.BlockSpec((HGRN_CHUNK, HGRN_CHUNK), lambda b, h, ti: (0, 0)),
            pl.BlockSpec((HGRN_CHUNK, HGRN_CHUNK), lambda b, h, ti: (0, 0)),
        ],
        out_specs=pl.BlockSpec((t, width), lambda b, h, ti: (b * nt + ti, h)),
        scratch_shapes=[pltpu.VMEM((hp, HEAD_DIM, HEAD_DIM), F32), pltpu.VMEM((hp, t, HEAD_DIM), F32)],
        compiler_params=pltpu.CompilerParams(
            dimension_semantics=("arbitrary", "arbitrary", "arbitrary"), vmem_limit_bytes=VMEM_LIMIT),
        name="hgrn",
    )(proj, proj_f32, proj, proj, par, tri, lvl)


def _compress_kernel(x_ref, w1_ref, b1_ref, w2_ref, pos_ref, tab_ref, o_ref, x32_ref):
    n_rows = o_ref.shape[-2]
    x32_ref[...] = x_ref[...].astype(F32)
    acc_lo = jnp.zeros((n_rows, HEAD_DIM), F32)
    acc_hi = jnp.zeros((n_rows, HEAD_DIM), F32)
    for r in range(CMP_STRIDE):
        xr = x32_ref[pl.ds(r, n_rows, stride=CMP_STRIDE), :]
        lo, hi = r, CMP_STRIDE + r
        acc_lo = acc_lo + jnp.dot((xr + pos_ref[0, lo:lo + 1, :]).astype(BF16),
                                  w1_ref[0, lo * HEAD_DIM:(lo + 1) * HEAD_DIM, :],
                                  preferred_element_type=F32)
        acc_hi = acc_hi + jnp.dot((xr + pos_ref[0, hi:hi + 1, :]).astype(BF16),
                                  w1_ref[0, hi * HEAD_DIM:(hi + 1) * HEAD_DIM, :],
                                  preferred_element_type=F32)
    pre = acc_lo + pltpu.roll(acc_hi, n_rows - 1, 0) + b1_ref[0]
    mid = pre * _sigmoid(pre)
    out = jnp.dot(mid.astype(BF16), w2_ref[0], preferred_element_type=F32)
    o_ref[0, 0, 0] = (out * tab_ref[0, 0] + pltpu.roll(out, LANES - ROT_DIM // 2, 1) * tab_ref[0, 1]
                      + pltpu.roll(out, ROT_DIM // 2, 1) * tab_ref[0, 2]).astype(BF16)


def _compress(proj, w1, b1, w2, pos, tabs, batch, seq):
    n_rows = seq // CMP_STRIDE
    return pl.pallas_call(
        _compress_kernel,
        out_shape=jax.ShapeDtypeStruct((2, batch, NSA_KV_GROUPS, n_rows, HEAD_DIM), BF16),
        grid=(2, batch, NSA_KV_GROUPS),
        in_specs=[
            pl.BlockSpec((seq, HEAD_DIM), lambda kv, b, g: (b, BLK_KC + 2 * kv + g)),
            pl.BlockSpec((1, CMP_BLOCK * HEAD_DIM, HEAD_DIM), lambda kv, b, g: (kv, 0, 0)),
            pl.BlockSpec((1, 1, HEAD_DIM), lambda kv, b, g: (kv, 0, 0)),
            pl.BlockSpec((1, HEAD_DIM, HEAD_DIM), lambda kv, b, g: (kv, 0, 0)),
            pl.BlockSpec((1, CMP_BLOCK, HEAD_DIM), lambda kv, b, g: (kv, 0, 0)),
            pl.BlockSpec((1, 3, n_rows, HEAD_DIM), lambda kv, b, g: (kv, 0, 0, 0)),
        ],
        out_specs=pl.BlockSpec((1, 1, 1, n_rows, HEAD_DIM), lambda kv, b, g: (kv, b, g, 0, 0)),
        scratch_shapes=[pltpu.VMEM((seq, HEAD_DIM), F32)],
        compiler_params=pltpu.CompilerParams(
            dimension_semantics=("arbitrary", "arbitrary", "arbitrary"), vmem_limit_bytes=VMEM_LIMIT),
        name="compress",
    )(proj, w1, b1, w2, pos, tabs)


def _kvprep_kernel(ks_ref, kw_ref, vs_ref, vw_ref, kaug_ref, vaug_ref, kwb_ref, vwb_ref):
    t = ks_ref.shape[0]
    t0 = pl.program_id(2) * t
    row = lax.broadcasted_iota(jnp.int32, (t, LANES), 0) + t0
    col = lax.broadcasted_iota(jnp.int32, (t, LANES), 1)
    block_lane = lax.shift_right_logical(row, 6) & (LANES - 1)
    kaug_ref[0, 0, :, :HEAD_DIM] = ks_ref[...].astype(BF16)
    kaug_ref[0, 0, :, HEAD_DIM:] = jnp.where(block_lane == col, 1.0, 0.0).astype(BF16)
    vaug_ref[0, 0, :, :HEAD_DIM] = vs_ref[...].astype(BF16)
    vaug_ref[0, 0, :, HEAD_DIM:] = jnp.ones((t, LANES), BF16)
    kwb_ref[0, 0] = kw_ref[...].astype(BF16)
    vwb_ref[0, 0] = vw_ref[...].astype(BF16)


def _kvprep(proj, batch, seq):
    t = min(PREP_T, seq)
    nt = seq // t

    def col(base):
        return pl.BlockSpec((t, HEAD_DIM), lambda b, g, ti: (b * nt + ti, base + g))

    def out(width):
        return pl.BlockSpec((1, 1, t, width), lambda b, g, ti: (b, g, ti, 0))

    def shape(width):
        return jax.ShapeDtypeStruct((batch, NSA_KV_GROUPS, seq, width), BF16)

    return pl.pallas_call(
        _kvprep_kernel,
        out_shape=(shape(2 * HEAD_DIM), shape(2 * HEAD_DIM), shape(HEAD_DIM), shape(HEAD_DIM)),
        grid=(batch, NSA_KV_GROUPS, nt),
        in_specs=[col(BLK_KS), col(BLK_KW), col(BLK_VS), col(BLK_VW)],
        out_specs=(out(2 * HEAD_DIM), out(2 * HEAD_DIM), out(HEAD_DIM), out(HEAD_DIM)),
        compiler_params=pltpu.CompilerParams(
            dimension_semantics=("arbitrary", "arbitrary", "arbitrary"), vmem_limit_bytes=VMEM_LIMIT),
        name="kvprep",
    )(proj, proj, proj, proj)


def _cmpwin_kernel(q_ref, kc_ref, vc_ref, kw_ref, vw_ref, ng_ref, ov_ref, part_ref, selb_ref,
                   imp_ref, val_ref, *, n_blocks, win_len, key_chunk):
    tq = q_ref.shape[0]
    n_cmp = kc_ref.shape[-2]
    nblkp = selb_ref.shape[-1]
    q0 = pl.program_id(2) * tq
    gates = _sigmoid(ng_ref[...].astype(F32))
    q_heads = [q_ref[:, h * HEAD_DIM:(h + 1) * HEAD_DIM] for h in range(HEADS_PER_GROUP)]

    def attend(s, values_ones, vis, all_masked_possible):
        s = jnp.where(vis, s, NEG_BIG)
        m = jnp.max(s, axis=-1, keepdims=True)
        if all_masked_possible:
            m = jnp.where(m < 0.5 * NEG_BIG, 0.0, m)
        p = jnp.exp2(s - m)
        o = jnp.dot(p.astype(BF16), values_ones, preferred_element_type=F32)
        inv = 1.0 / jnp.maximum(o[:, HEAD_DIM:], 1e-30)
        return o[:, :HEAD_DIM] * inv, p, inv

    w0 = pl.multiple_of(jnp.maximum(q0 + tq - win_len, 0), tq)
    kwin = kw_ref[0, 0, pl.ds(w0, win_len), :]
    vwin = jnp.concatenate([vw_ref[0, 0, pl.ds(w0, win_len), :], jnp.ones((win_len, HEAD_DIM), BF16)], axis=1)
    t_w = lax.broadcasted_iota(jnp.int32, (tq, win_len), 0) + q0
    kp_w = lax.broadcasted_iota(jnp.int32, (tq, win_len), 1) + w0
    vis_w = lax.bitcast_convert_type(t_w - kp_w, jnp.uint32) < WINDOW
    s_w = [_nt_dot(qh, kwin) for qh in q_heads]
    for h in range(HEADS_PER_GROUP):
        o_w, _, _ = attend(s_w[h], vwin, vis_w, False)
        g_w = gates[:, 2 * HEADS_PER_GROUP + h:2 * HEADS_PER_GROUP + h + 1]
        part_ref[:, h * HEAD_DIM:(h + 1) * HEAD_DIM] = g_w * o_w

    def compressed(n_keys):
        kc = kc_ref[0, 0, 0, :n_keys, :]
        vc = jnp.concatenate([vc_ref[0, 0, 0, :n_keys, :], jnp.ones((n_keys, HEAD_DIM), BF16)], axis=1)
        t_c = lax.broadcasted_iota(jnp.int32, (tq, n_keys), 0) + q0
        n_c = lax.broadcasted_iota(jnp.int32, (tq, n_keys), 1)
        vis_c = n_c * CMP_STRIDE + (CMP_BLOCK - 1) <= t_c
        p_sum = jnp.zeros((tq, n_keys), F32)
        s_c = [_nt_dot(qh, kc) for qh in q_heads]
        for h in range(HEADS_PER_GROUP):
            sl = slice(h * HEAD_DIM, (h + 1) * HEAD_DIM)
            o_c, p, inv = attend(s_c[h], vc, vis_c, True)
            p_sum = p_sum + p * jnp.concatenate([inv] * (n_keys // LANES), axis=1)
            g_c = gates[:, 0 * HEADS_PER_GROUP + h:0 * HEADS_PER_GROUP + h + 1]
            part_ref[:, sl] = part_ref[:, sl] + g_c * o_c
        p_hi = p_sum.astype(BF16)
        p_lo = (p_sum - p_hi.astype(F32)).astype(BF16)
        imp_ref[...] = (jnp.dot(p_hi, ov_ref[:n_keys, :], preferred_element_type=F32)
                        + jnp.dot(p_lo, ov_ref[:n_keys, :], preferred_element_type=F32))

    n_visible = (q0 + tq - CMP_BLOCK) // CMP_STRIDE + 1
    n_chunks = jnp.minimum((n_visible + key_chunk - 1) // key_chunk, n_cmp // key_chunk)
    for c in range(1, n_cmp // key_chunk + 1):
        @pl.when(n_chunks == c)
        def _():
            compressed(c * key_chunk)

    imp_t = imp_ref[...].T
    t_b = lax.broadcasted_iota(jnp.int32, (nblkp, tq), 1) + q0
    blk = lax.broadcasted_iota(jnp.int32, (nblkp, tq), 0)
    cur = lax.shift_right_logical(t_b, 6)
    near = lax.bitcast_convert_type(cur - blk, jnp.uint32) < 2
    val = jnp.where(near, TAKEN, jnp.where(blk * SEL_BLOCK <= t_b, imp_t, -1.0))
    val = jnp.where(blk == 0, TAKEN, val)
    val_ref[...] = jnp.where(blk < n_blocks, val, -3.0)
    blk_f = blk.astype(F32)
    n_top = min(N_SELECT, n_blocks)
    quota = n_top - 1 - jnp.minimum(cur[0:1, :], 2)

    def pick(it, carry):
        v = val_ref[...]
        m = jnp.max(v, axis=0, keepdims=True)
        idx = jnp.min(jnp.where(v == m, blk_f, float(nblkp)), axis=0, keepdims=True)
        idx = jnp.where(it < quota, idx, -1.0)
        val_ref[...] = jnp.where(blk_f == idx, TAKEN, v)
        return carry

    lax.fori_loop(0, n_top - 1 - jnp.minimum(lax.shift_right_logical(q0, 6), 2), pick, 0)
    selb_ref[0, 0] = jnp.where(val_ref[...].T == TAKEN, 0.0, NEG_BIG).astype(BF16)


def _cmpwin(proj, cmp_kv, kwb, vwb, ov, batch, seq):
    tq = min(ATT_TQ, seq)
    nq = seq // tq
    n_cmp = seq // CMP_STRIDE
    n_blocks = seq // SEL_BLOCK
    nblkp = ov.shape[1]
    win_len = min(WINDOW + tq, seq)
    q_cols = HEADS_PER_GROUP * HEAD_DIM
    key_chunk = min(CMP_KEY_CHUNK, n_cmp)
    kernel = functools.partial(_cmpwin_kernel, n_blocks=n_blocks, win_len=win_len, key_chunk=key_chunk)
    return pl.pallas_call(
        kernel,
        out_shape=(jax.ShapeDtypeStruct((batch * seq, NSA_HEADS * HEAD_DIM), F32),
                   jax.ShapeDtypeStruct((batch, NSA_KV_GROUPS, seq, nblkp), BF16)),
        grid=(batch, NSA_KV_GROUPS, nq),
        in_specs=[
            pl.BlockSpec((tq, q_cols), lambda b, g, i: (b * nq + i, BLK_NQ * LANES // q_cols + g)),
            pl.BlockSpec((1, 1, 1, n_cmp, HEAD_DIM), lambda b, g, i: (0, b, g, 0, 0)),
            pl.BlockSpec((1, 1, 1, n_cmp, HEAD_DIM), lambda b, g, i: (1, b, g, 0, 0)),
            pl.BlockSpec((1, 1, seq, HEAD_DIM), lambda b, g, i: (b, g, 0, 0)),
            pl.BlockSpec((1, 1, seq, HEAD_DIM), lambda b, g, i: (b, g, 0, 0)),
            pl.BlockSpec((tq, LANES), lambda b, g, i: (b * nq + i, BLK_NG + g)),
            pl.BlockSpec((n_cmp, nblkp), lambda b, g, i: (0, 0)),
        ],
        out_specs=(pl.BlockSpec((tq, q_cols), lambda b, g, i: (b * nq + i, g)),
                   pl.BlockSpec((1, 1, tq, nblkp), lambda b, g, i: (b, g, i, 0))),
        scratch_shapes=[pltpu.VMEM((tq, nblkp), F32), pltpu.VMEM((nblkp, tq), F32)],
        compiler_params=pltpu.CompilerParams(
            dimension_semantics=("arbitrary", "arbitrary", "arbitrary"), vmem_limit_bytes=VMEM_LIMIT),
        name="cmpwin",
    )(proj, cmp_kv, cmp_kv, kwb, vwb, proj, ov)


def _select_kernel(q_ref, selb_ref, kaug_ref, vaug_ref, part_ref, ng_ref, nz_ref, y_ref,
                   qaug_ref, acc_ref, m_ref, p_ref, alpha_ref, *, tk):
    tq = q_ref.shape[0]
    n_halves = selb_ref.shape[-1] // LANES
    q0 = pl.program_id(2) * tq

    for h in range(HEADS_PER_GROUP):
        qh = q_ref[:, h * HEAD_DIM:(h + 1) * HEAD_DIM]
        for half in range(n_halves):
            qaug_ref[half * HEADS_PER_GROUP + h, :, :HEAD_DIM] = qh
            qaug_ref[half * HEADS_PER_GROUP + h, :, HEAD_DIM:] = selb_ref[0, 0, :, half * LANES:(half + 1) * LANES]
    acc_ref[...] = jnp.zeros_like(acc_ref)
    m_ref[...] = jnp.full_like(m_ref, NEG_BIG)

    def tile_step(kt, pending, causal):
        if kt is not None:
            k0 = pl.multiple_of(kt * tk, tk)
            k_t = kaug_ref[0, 0, pl.ds(k0, tk), :]
            half = k0 // (SEL_BLOCK * LANES)
            if causal:
                t_pos = lax.broadcasted_iota(jnp.int32, (tq, tk), 0) + q0
                k_pos = lax.broadcasted_iota(jnp.int32, (tq, tk), 1) + k0
                vis = k_pos <= t_pos
        if pending is not None:
            v_t = vaug_ref[0, 0, pl.ds(pl.multiple_of(pending * tk, tk), tk), :]
        if kt is not None:
            s_next = _nt_dot(qaug_ref[half * HEADS_PER_GROUP], k_t)
        for h in range(HEADS_PER_GROUP):
            if kt is not None:
                s = s_next
                if h + 1 < HEADS_PER_GROUP:
                    s_next = _nt_dot(qaug_ref[half * HEADS_PER_GROUP + h + 1], k_t)
            if pending is not None:
                alpha = alpha_ref[h]
                acc_ref[h] = (acc_ref[h] * jnp.concatenate([alpha, alpha], axis=1)
                              + jnp.dot(p_ref[h], v_t, preferred_element_type=F32))
            if kt is not None:
                if causal:
                    s = jnp.where(vis, s, NEG_BIG)
                m_prev = m_ref[h]
                m_new = jnp.maximum(m_prev, jnp.max(s, axis=-1, keepdims=True))
                p_ref[h] = jnp.exp2(s - jnp.concatenate([m_new] * (tk // LANES), axis=1)).astype(BF16)
                alpha_ref[h] = jnp.exp2(m_prev - m_new)
                m_ref[h] = m_new

    n_full = q0 // tk

    @pl.when(n_full > 0)
    def _():
        tile_step(0, None, False)

    @pl.when(n_full == 0)
    def _():
        p_ref[...] = jnp.zeros_like(p_ref)
        alpha_ref[...] = jnp.ones_like(alpha_ref)

    def step(kt, carry):
        tile_step(kt, kt - 1, False)
        return carry

    lax.fori_loop(1, n_full, step, 0)
    tile_step(n_full, jnp.maximum(n_full - 1, 0), True)
    tile_step(None, n_full, False)

    gates = _sigmoid(ng_ref[...].astype(F32))
    for h in range(HEADS_PER_GROUP):
        sl = slice(h * HEAD_DIM, (h + 1) * HEAD_DIM)
        acc = acc_ref[h]
        o_s = acc[:, :HEAD_DIM] / jnp.maximum(acc[:, HEAD_DIM:], 1e-30)
        g_s = gates[:, 1 * HEADS_PER_GROUP + h:1 * HEADS_PER_GROUP + h + 1]
        z = nz_ref[:, sl].astype(F32)
        y_ref[:, sl] = ((part_ref[:, sl] + g_s * o_s) * (z * _sigmoid(z))).astype(BF16)


def _select(proj, selb, kaug, vaug, part, batch, seq):
    tq = min(SEL_TQ, seq)
    tk = min(SEL_TK, seq)
    nq = seq // tq
    nblkp = selb.shape[-1]
    q_cols = HEADS_PER_GROUP * HEAD_DIM
    kernel = functools.partial(_select_kernel, tk=tk)
    return pl.pallas_call(
        kernel,
        out_shape=jax.ShapeDtypeStruct((batch * seq, NSA_HEADS * HEAD_DIM), BF16),
        grid=(batch, NSA_KV_GROUPS, nq),
        in_specs=[
            pl.BlockSpec((tq, q_cols), lambda b, g, i: (b * nq + i, BLK_NQ * LANES // q_cols + g)),
            pl.BlockSpec((1, 1, tq, nblkp), lambda b, g, i: (b, g, i, 0)),
            pl.BlockSpec((1, 1, seq, 2 * HEAD_DIM), lambda b, g, i: (b, g, 0, 0), pipeline_mode=pl.Buffered(1)),
            pl.BlockSpec((1, 1, seq, 2 * HEAD_DIM), lambda b, g, i: (b, g, 0, 0), pipeline_mode=pl.Buffered(1)),
            pl.BlockSpec((tq, q_cols), lambda b, g, i: (b * nq + i, g)),
            pl.BlockSpec((tq, LANES), lambda b, g, i: (b * nq + i, BLK_NG + g)),
            pl.BlockSpec((tq, q_cols), lambda b, g, i: (b * nq + i, BLK_NZ * LANES // q_cols + g)),
        ],
        out_specs=pl.BlockSpec((tq, q_cols), lambda b, g, i: (b * nq + i, g)),
        scratch_shapes=[
            pltpu.VMEM((nblkp // LANES * HEADS_PER_GROUP, tq, 2 * HEAD_DIM), BF16),
            pltpu.VMEM((HEADS_PER_GROUP, tq, 2 * HEAD_DIM), F32),
            pltpu.VMEM((HEADS_PER_GROUP, tq, LANES), F32),
            pltpu.VMEM((HEADS_PER_GROUP, tq, tk), BF16),
            pltpu.VMEM((HEADS_PER_GROUP, tq, LANES), F32),
        ],
        compiler_params=pltpu.CompilerParams(
            dimension_semantics=("arbitrary", "arbitrary", "arbitrary"), vmem_limit_bytes=VMEM_LIMIT),
        name="select",
    )(proj, selb, kaug, vaug, part, proj, proj)


def _outproj_kernel(yh_ref, yn_ref, w_ref, x_ref, g_ref, o_ref):
    half = yh_ref.shape[1]
    z = (jnp.dot(yh_ref[...], w_ref[:half, :], preferred_element_type=F32)
         + jnp.dot(yn_ref[...], w_ref[half:, :], preferred_element_type=F32))
    ms = jnp.mean(z * z, axis=-1, keepdims=True)
    o_ref[...] = x_ref[...] + z * lax.rsqrt(ms + NORM_EPS) * g_ref[...]


def _outproj(yh, yn, w, x2, g):
    n = x2.shape[0]
    tm = min(OUT_TM, n)
    return pl.pallas_call(
        _outproj_kernel,
        out_shape=jax.ShapeDtypeStruct((n, D_MODEL), F32),
        grid=(n // tm,),
        in_specs=[
            pl.BlockSpec((tm, yh.shape[1]), lambda i: (i, 0)),
            pl.BlockSpec((tm, yn.shape[1]), lambda i: (i, 0)),
            pl.BlockSpec(w.shape, lambda i: (0, 0)),
            pl.BlockSpec((tm, D_MODEL), lambda i: (i, 0)),
            pl.BlockSpec((1, D_MODEL), lambda i: (0, 0)),
        ],
        out_specs=pl.BlockSpec((tm, D_MODEL), lambda i: (i, 0)),
        compiler_params=pltpu.CompilerParams(
            dimension_semantics=("arbitrary",), vmem_limit_bytes=VMEM_LIMIT),
        name="outproj",
    )(yh, yn, w, x2, g)


def _permute_weights(w):
    def seg(name, width):
        return w[:, _SRC[name]:_SRC[name] + width]

    ng = seg("ng", NSA_HEADS * N_BRANCHES).reshape(-1, NSA_KV_GROUPS, HEADS_PER_GROUP, N_BRANCHES)
    ng = ng.transpose(0, 1, 3, 2).reshape(-1, NSA_KV_GROUPS, N_BRANCHES * HEADS_PER_GROUP)
    ng = jnp.pad(ng, ((0, 0), (0, 0), (0, LANES - N_BRANCHES * HEADS_PER_GROUP))).reshape(-1, NSA_KV_GROUPS * LANES)
    placed = [(BLK_HQ, seg("hq", 1024)), (BLK_HI, seg("hi", 1024)), (BLK_HZ, seg("hz", 1024)),
              (BLK_NQ, seg("nq", 1024) * (HEAD_DIM ** -0.5 * LOG2_E)),
              (BLK_KS, seg("ks", 256)), (BLK_KW, seg("kw", 256)), (BLK_KC, seg("kc", 256)),
              (BLK_VC, seg("vc", 256)), (BLK_VS, seg("vs", 256)), (BLK_VW, seg("vw", 256)),
              (BLK_NZ, seg("nz", 1024)), (BLK_NG, ng), (BLK_HF, seg("hf", 1024))]
    parts, col = [], 0
    for blk, cols in sorted(placed, key=lambda p: p[0]):
        if blk * LANES > col:
            parts.append(jnp.zeros((w.shape[0], blk * LANES - col), w.dtype))
        parts.append(cols)
        col = blk * LANES + cols.shape[1]
    if col < PROJ_COLS:
        parts.append(jnp.zeros((w.shape[0], PROJ_COLS - col), w.dtype))
    return jnp.concatenate(parts, axis=1).astype(BF16)


def _rope_tables(pos):
    half = ROT_DIM // 2
    inv = ROPE_THETA ** (-2.0 * jnp.arange(half, dtype=F32) / ROT_DIM)
    ang = pos.astype(F32)[:, None] * inv[None, :]
    cos, sin = jnp.cos(ang), jnp.sin(ang)
    n = pos.shape[0]
    pad = jnp.zeros((n, HEAD_DIM - ROT_DIM), F32)
    zero = jnp.zeros((n, half), F32)
    c = jnp.concatenate([cos, cos, pad + 1.0], axis=1)
    s1 = jnp.concatenate([-sin, zero, pad], axis=1)
    s2 = jnp.concatenate([zero, sin, pad], axis=1)
    return c, s1, s2


def _pair_levels():
    t = np.arange(HGRN_CHUNK)[:, None]
    s = np.arange(HGRN_CHUNK)[None, :]
    high_bit = 2 ** np.floor(np.log2(np.maximum(t ^ s, 1))).astype(np.int64)
    return jnp.asarray(np.where(t > s, high_bit, np.where(t == s, 0, -1)), jnp.int32)


def _overlap_matrix(n_cmp, n_blocks, nblkp):
    start = np.arange(n_cmp)[:, None] * CMP_STRIDE
    sel = np.arange(nblkp)[None, :] * SEL_BLOCK
    ov = (start < sel + SEL_BLOCK) & (start + CMP_BLOCK > sel)
    ov &= np.arange(nblkp)[None, :] < n_blocks
    ov &= np.arange(n_cmp)[:, None] < (n_cmp - 1)
    return jnp.asarray(ov, BF16)


def kernel(x, pre_norm, post_norm, w_in, hgrn_lb_logits, hgrn_out_norm, cmp_pos_k, cmp_w1_k, cmp_b1_k,
           cmp_w2_k, cmp_pos_v, cmp_w1_v, cmp_b1_v, cmp_w2_v, w_out):
    batch, seq, _ = x.shape
    depth = w_in.shape[0]
    n_cmp = seq // CMP_STRIDE
    n_blocks = seq // SEL_BLOCK
    nblkp = -(-n_blocks // LANES) * LANES

    lb_probs = jax.nn.softmax(hgrn_lb_logits.astype(F32), axis=0)
    lower = jnp.maximum(jnp.cumsum(lb_probs, axis=0) - lb_probs[0:1], 0.0)
    lower = lower.reshape(depth, HGRN_HEADS, 1, HEAD_DIM)
    g_out = jnp.broadcast_to(hgrn_out_norm.astype(F32)[:, None, None, :], lower.shape)
    hgrn_par = jnp.concatenate(
        [jnp.log(lower), jnp.log1p(-lower), 1.0 - lower, g_out, jnp.zeros((depth, HGRN_HEADS, 4, HEAD_DIM), F32)],
        axis=2)

    rope_c, rope_s1, rope_s2 = _rope_tables(jnp.arange(seq))
    cmp_tab_k = jnp.stack(_rope_tables(jnp.arange(n_cmp) * CMP_STRIDE + CMP_BLOCK - 1))
    cmp_tab_v = jnp.stack([jnp.ones((n_cmp, HEAD_DIM), F32), jnp.zeros((n_cmp, HEAD_DIM), F32),
                           jnp.zeros((n_cmp, HEAD_DIM), F32)])
    cmp_tabs = jnp.stack([cmp_tab_k, cmp_tab_v])
    ov = _overlap_matrix(n_cmp, n_blocks, nblkp)
    tri = jnp.asarray(np.tril(np.ones((HGRN_CHUNK, HGRN_CHUNK), np.float32)))
    lvl = _pair_levels()

    x2 = x.reshape(batch * seq, D_MODEL)
    for layer in range(depth):
        proj, proj_f32 = _proj(x2, pre_norm[layer][None, :], _permute_weights(w_in[layer]),
                               rope_c, rope_s1, rope_s2, seq)
        y_h = _hgrn(proj, proj_f32, hgrn_par[layer], tri, lvl, batch, seq)
        cmp_kv = _compress(
            proj,
            jnp.stack([cmp_w1_k[layer], cmp_w1_v[layer]]).astype(BF16),
            jnp.stack([cmp_b1_k[layer], cmp_b1_v[layer]])[:, None, :],
            jnp.stack([cmp_w2_k[layer], cmp_w2_v[layer]]).astype(BF16),
            jnp.stack([cmp_pos_k[layer], cmp_pos_v[layer]]),
            cmp_tabs, batch, seq)
        kaug, vaug, kwb, vwb = _kvprep(proj, batch, seq)
        part, selb = _cmpwin(proj, cmp_kv, kwb, vwb, ov, batch, seq)
        y_n = _select(proj, selb, kaug, vaug, part, batch, seq)
        x2 = _outproj(y_h, y_n, w_out[layer].astype(BF16), x2, post_norm[layer][None, :])
    return x2.reshape(batch, seq, D_MODEL)
```

```python
import functools

import numpy as np
import jax
import jax.numpy as jnp
from jax import lax
from jax.experimental import pallas as pl
from jax.experimental.pallas import tpu as pltpu

F32 = jnp.float32
BF16 = jnp.bfloat16

D_MODEL = 1024
HEAD_DIM = 128
HGRN_HEADS = 8
NSA_HEADS = 8
NSA_KV_GROUPS = 2
HEADS_PER_GROUP = NSA_HEADS // NSA_KV_GROUPS
N_BRANCHES = 3
HGRN_CHUNK = 64
CMP_BLOCK = 32
CMP_STRIDE = 16
SEL_BLOCK = 64
N_SELECT = 16
WINDOW = 512
ROPE_THETA = 500000.0
ROT_DIM = HEAD_DIM // 4
NORM_EPS = 1e-6
TAKEN = -4.0
NEG_BIG = -1e30
LOG2_E = 1.4426950408889634

LANES = 128
SUBLANES = 8
BLK_HQ, BLK_HI, BLK_HZ = 0, 8, 16
BLK_NQ, BLK_KS, BLK_KW, BLK_KC, BLK_VC, BLK_VS, BLK_VW = 24, 32, 34, 36, 38, 40, 42
BLK_NZ, BLK_NG, BLK_HF = 44, 52, 56
PROJ_BLOCKS = 64
PROJ_COLS = PROJ_BLOCKS * LANES
PROJ_BF16_COLS = BLK_HF * LANES
_SRC = dict(hq=0, hf=1024, hi=2048, hz=3072, nq=4096, kc=5120, vc=5376, ks=5632, vs=5888,
            kw=6144, vw=6400, ng=6656, nz=6680)

PROJ_TM = 1024
PROJ_TN = 1024
ROPE_FULL_TILE = BLK_NQ * LANES // PROJ_TN
ROPE_HEAD_TILE = BLK_KS * LANES // PROJ_TN
ROPE_HEAD_BLOCKS = 4
F32_TILE = BLK_HF * LANES // PROJ_TN
HGRN_T = 512
HGRN_HEADS_PER_STEP = 2
ATT_TQ = 256
CMP_KEY_CHUNK = 256
SEL_TQ = 512
SEL_TK = 1024
OUT_TM = 512
VMEM_LIMIT = 56 * 1024 * 1024


def _nt_dot(a, b):
    return lax.dot_general(a, b, (((1,), (1,)), ((), ())), preferred_element_type=F32)


def _sigmoid(x):
    return 1.0 / (1.0 + jnp.exp(-x))


def _proj_kernel(x_ref, g_ref, w_ref, c_ref, s1_ref, s2_ref, o_ref, o32_ref, h_ref, acc_ref):
    j = pl.program_id(1)

    @pl.when(j == 0)
    def _():
        x = x_ref[...]
        ms = jnp.mean(x * x, axis=-1, keepdims=True)
        h_ref[...] = (x * lax.rsqrt(ms + NORM_EPS) * g_ref[...]).astype(BF16)

    @pl.when(j == F32_TILE)
    def _():
        o32_ref[...] = jnp.dot(h_ref[...], w_ref[...], preferred_element_type=F32)

    def store(n_rope_blocks):
        acc_ref[...] = jnp.dot(h_ref[...], w_ref[...], preferred_element_type=F32)
        for blk in range(n_rope_blocks):
            sl = slice(blk * LANES, (blk + 1) * LANES)
            a = acc_ref[:, sl]
            o_ref[:, sl] = (a * c_ref[...] + pltpu.roll(a, LANES - ROT_DIM // 2, 1) * s1_ref[...]
                            + pltpu.roll(a, ROT_DIM // 2, 1) * s2_ref[...]).astype(BF16)
        if n_rope_blocks * LANES < PROJ_TN:
            rest = slice(n_rope_blocks * LANES, PROJ_TN)
            o_ref[:, rest] = acc_ref[:, rest].astype(BF16)

    @pl.when(j == ROPE_FULL_TILE)
    def _():
        store(PROJ_TN // LANES)

    @pl.when(j == ROPE_HEAD_TILE)
    def _():
        store(ROPE_HEAD_BLOCKS)

    @pl.when((j != ROPE_FULL_TILE) & (j != ROPE_HEAD_TILE) & (j != F32_TILE))
    def _():
        o_ref[...] = jnp.dot(h_ref[...], w_ref[...], preferred_element_type=F32).astype(BF16)


def _proj(x2, g, w, rope_c, rope_s1, rope_s2, seq):
    n = x2.shape[0]
    tm = min(PROJ_TM, seq)
    pos_tiles = seq // tm
    return pl.pallas_call(
        _proj_kernel,
        out_shape=(jax.ShapeDtypeStruct((n, PROJ_BF16_COLS), BF16),
                   jax.ShapeDtypeStruct((n, PROJ_COLS - PROJ_BF16_COLS), F32)),
        grid=(n // tm, PROJ_COLS // PROJ_TN),
        in_specs=[
            pl.BlockSpec((tm, D_MODEL), lambda i, j: (i, 0)),
            pl.BlockSpec((1, D_MODEL), lambda i, j: (0, 0)),
            pl.BlockSpec((D_MODEL, PROJ_TN), lambda i, j: (0, j)),
            pl.BlockSpec((tm, LANES), lambda i, j: (i % pos_tiles, 0)),
            pl.BlockSpec((tm, LANES), lambda i, j: (i % pos_tiles, 0)),
            pl.BlockSpec((tm, LANES), lambda i, j: (i % pos_tiles, 0)),
        ],
        out_specs=(pl.BlockSpec((tm, PROJ_TN), lambda i, j: (i, jnp.minimum(j, F32_TILE - 1))),
                   pl.BlockSpec((tm, PROJ_TN), lambda i, j: (i, 0))),
        scratch_shapes=[pltpu.VMEM((tm, D_MODEL), BF16), pltpu.VMEM((tm, PROJ_TN), F32)],
        compiler_params=pltpu.CompilerParams(
            dimension_semantics=("arbitrary", "arbitrary"), vmem_limit_bytes=VMEM_LIMIT),
        name="proj",
    )(x2, g, w, rope_c, rope_s1, rope_s2)


def _hgrn_kernel(q_ref, f_ref, i_ref, z_ref, par_ref, tri_ref, lvl_ref, y_ref, st_ref, o_ref):
    for hh in range(q_ref.shape[1] // HEAD_DIM):
        cols = slice(hh * HEAD_DIM, (hh + 1) * HEAD_DIM)
        _hgrn_head(q_ref.at[:, cols], f_ref.at[:, cols], i_ref.at[:, cols], z_ref.at[:, cols],
                   par_ref.at[hh:hh + 1], tri_ref, lvl_ref, y_ref.at[:, cols], st_ref.at[hh], o_ref.at[hh])


def _hgrn_head(q_ref, f_ref, i_ref, z_ref, par_ref, tri_ref, lvl_ref, y_ref, st_ref, o_ref):
    @pl.when(pl.program_id(2) == 0)
    def _():
        st_ref[...] = jnp.zeros_like(st_ref)

    t = q_ref.shape[0]
    nc = t // HGRN_CHUNK
    shape3 = (nc, HGRN_CHUNK, HEAD_DIM)
    par = par_ref[0]
    log_lb, log_1m_lb, one_m_lb, g_norm = par[0:1], par[1:2], par[2:3], par[3:4]
    x = f_ref[...]
    e = jnp.exp(-jnp.abs(x))
    c = log_1m_lb + jnp.minimum(x, 0.0) - jnp.log(1.0 + e)
    lf = jnp.maximum(log_lb, c) + jnp.log(1.0 + jnp.exp(-jnp.abs(log_lb - c)))
    k3 = (one_m_lb * jnp.where(x >= 0.0, e, 1.0) / (1.0 + e)).reshape(shape3)
    q3 = q_ref[...].astype(F32).reshape(shape3)
    v3 = i_ref[...].astype(F32).reshape(shape3)
    v_bf = v3.astype(BF16)

    lf3 = (lf * LOG2_E).reshape(shape3)
    b3 = jnp.stack([jnp.dot(tri_ref[...], lf3[ci], precision=lax.Precision.HIGHEST,
                            preferred_element_type=F32) for ci in range(nc)])

    tiles = (t // SUBLANES, SUBLANES, HEAD_DIM)
    b_tile = b3.reshape(tiles)
    tile_row = lax.broadcasted_iota(jnp.int32, tiles, 1)
    lvl = lvl_ref[...]
    attn = jnp.where(lvl == 0, jnp.stack([_nt_dot(q3[ci].astype(BF16), k3[ci].astype(BF16)) for ci in range(nc)]), 0.0)
    n = HGRN_CHUNK // 2
    while n >= 1:
        if 2 * n >= SUBLANES:
            blocks = (t // (2 * n), 2 * n, HEAD_DIM)
            ref = jnp.broadcast_to(b3.reshape(blocks)[:, n - 1:n, :], blocks).reshape(shape3)
            side = jnp.concatenate([k3.reshape(blocks)[:, :n], q3.reshape(blocks)[:, n:]], axis=1).reshape(shape3)
        else:
            ref = jnp.broadcast_to(b_tile[:, n - 1:n, :], tiles)
            for lo in range(2 * n, SUBLANES, 2 * n):
                ref = jnp.where(tile_row >= lo, jnp.broadcast_to(b_tile[:, lo + n - 1:lo + n, :], tiles), ref)
            ref = ref.reshape(shape3)
            side = jnp.where((tile_row & n) != 0, q3.reshape(tiles), k3.reshape(tiles)).reshape(shape3)
        d = b3 - ref
        r_n = (side * jnp.exp2(jnp.minimum(d, -d))).astype(BF16)
        a_n = jnp.stack([_nt_dot(r_n[ci], r_n[ci]) for ci in range(nc)])
        attn = jnp.where(lvl == n, a_n, attn)
        n //= 2
    attn_bf = attn.astype(BF16)
    o_intra = jnp.stack([jnp.dot(attn_bf[ci], v_bf[ci], preferred_element_type=F32) for ci in range(nc)])

    b_last = b3[:, HGRN_CHUNK - 1:HGRN_CHUNK, :]
    decay = jnp.exp2(b_last)
    q_dec = (q3 * jnp.exp2(b3)).astype(BF16)
    k_dec = (k3 * jnp.exp2(b_last - b3)).astype(BF16)
    upd = [lax.dot_general(v_bf[ci], k_dec[ci], (((0,), (0,)), ((), ())), preferred_element_type=F32)
           for ci in range(nc)]
    st = st_ref[...]
    for ci in range(nc):
        o_ref[ci * HGRN_CHUNK:(ci + 1) * HGRN_CHUNK, :] = o_intra[ci] + _nt_dot(q_dec[ci], st.astype(BF16))
        st = st * decay[ci] + upd[ci]
    st_ref[...] = st

    o = o_ref[...]
    ms = jnp.mean(o * o, axis=-1, keepdims=True)
    z = z_ref[...].astype(F32)
    y_ref[...] = (o * lax.rsqrt(ms + NORM_EPS) * g_norm * (z * _sigmoid(z))).astype(BF16)


def _hgrn(proj, proj_f32, par, tri, lvl, batch, seq):
    t = min(HGRN_T, seq)
    nt = seq // t

    hp = HGRN_HEADS_PER_STEP
    width = hp * HEAD_DIM

    def col(base):
        return pl.BlockSpec((t, width), lambda b, h, ti: (b * nt + ti, base // hp + h))

    return pl.pallas_call(
        _hgrn_kernel,
        out_shape=jax.ShapeDtypeStruct((batch * seq, HGRN_HEADS * HEAD_DIM), BF16),
        grid=(batch, HGRN_HEADS // hp, nt),
        in_specs=[
            col(BLK_HQ), col(0), col(BLK_HI), col(BLK_HZ),
            pl.BlockSpec((hp, 8, HEAD_DIM), lambda b, h, ti: (h, 0, 0)),
            pl.BlockSpec((HGRN_CHUNK, HGRN_CHUNK), lambda b, h, ti: (0, 0)),
            pl.BlockSpec((HGRN_CHUNK, HGRN_CHUNK), lambda b, h, ti: (0, 0)),
        ],
        out_specs=pl.BlockSpec((t, width), lambda b, h, ti: (b * nt + ti, h)),
        scratch_shapes=[pltpu.VMEM((hp, HEAD_DIM, HEAD_DIM), F32), pltpu.VMEM((hp, t, HEAD_DIM), F32)],
        compiler_params=pltpu.CompilerParams(
            dimension_semantics=("arbitrary", "arbitrary", "arbitrary"), vmem_limit_bytes=VMEM_LIMIT),
        name="hgrn",
    )(proj, proj_f32, proj, proj, par, tri, lvl)


def _compress_kernel(x_ref, w1_ref, b1_ref, w2_ref, pos_ref, tab_ref, o_ref, x32_ref):
    n_rows = o_ref.shape[-2]
    x32_ref[...] = x_ref[...].astype(F32)
    acc_lo = jnp.zeros((n_rows, HEAD_DIM), F32)
    acc_hi = jnp.zeros((n_rows, HEAD_DIM), F32)
    for r in range(CMP_STRIDE):
        xr = x32_ref[pl.ds(r, n_rows, stride=CMP_STRIDE), :]
        lo, hi = r, CMP_STRIDE + r
        acc_lo = acc_lo + jnp.dot((xr + pos_ref[0, lo:lo + 1, :]).astype(BF16),
                                  w1_ref[0, lo * HEAD_DIM:(lo + 1) * HEAD_DIM, :],
                                  preferred_element_type=F32)
        acc_hi = acc_hi + jnp.dot((xr + pos_ref[0, hi:hi + 1, :]).astype(BF16),
                                  w1_ref[0, hi * HEAD_DIM:(hi + 1) * HEAD_DIM, :],
                                  preferred_element_type=F32)
    pre = acc_lo + pltpu.roll(acc_hi, n_rows - 1, 0) + b1_ref[0]
    mid = pre * _sigmoid(pre)
    out = jnp.dot(mid.astype(BF16), w2_ref[0], preferred_element_type=F32)
    o_ref[0, 0, 0] = (out * tab_ref[0, 0] + pltpu.roll(out, LANES - ROT_DIM // 2, 1) * tab_ref[0, 1]
                      + pltpu.roll(out, ROT_DIM // 2, 1) * tab_ref[0, 2]).astype(BF16)


def _compress(proj, w1, b1, w2, pos, tabs, batch, seq):
    n_rows = seq // CMP_STRIDE
    return pl.pallas_call(
        _compress_kernel,
        out_shape=jax.ShapeDtypeStruct((2, batch, NSA_KV_GROUPS, n_rows, HEAD_DIM), BF16),
        grid=(2, batch, NSA_KV_GROUPS),
        in_specs=[
            pl.BlockSpec((seq, HEAD_DIM), lambda kv, b, g: (b, BLK_KC + 2 * kv + g)),
            pl.BlockSpec((1, CMP_BLOCK * HEAD_DIM, HEAD_DIM), lambda kv, b, g: (kv, 0, 0)),
            pl.BlockSpec((1, 1, HEAD_DIM), lambda kv, b, g: (kv, 0, 0)),
            pl.BlockSpec((1, HEAD_DIM, HEAD_DIM), lambda kv, b, g: (kv, 0, 0)),
            pl.BlockSpec((1, CMP_BLOCK, HEAD_DIM), lambda kv, b, g: (kv, 0, 0)),
            pl.BlockSpec((1, 3, n_rows, HEAD_DIM), lambda kv, b, g: (kv, 0, 0, 0)),
        ],
        out_specs=pl.BlockSpec((1, 1, 1, n_rows, HEAD_DIM), lambda kv, b, g: (kv, b, g, 0, 0)),
        scratch_shapes=[pltpu.VMEM((seq, HEAD_DIM), F32)],
        compiler_params=pltpu.CompilerParams(
            dimension_semantics=("arbitrary", "arbitrary", "arbitrary"), vmem_limit_bytes=VMEM_LIMIT),
        name="compress",
    )(proj, w1, b1, w2, pos, tabs)


def _cmpwin_kernel(q_ref, kc_ref, vc_ref, kw_ref, vw_ref, ng_ref, ov_ref, part_ref, selb_ref,
                   imp_ref, val_ref, *, n_blocks, win_len, key_chunk):
    tq = q_ref.shape[0]
    n_cmp = kc_ref.shape[-2]
    nblkp = selb_ref.shape[-1]
    q0 = pl.program_id(2) * tq
    gates = _sigmoid(ng_ref[...].astype(F32))
    q_heads = [q_ref[:, h * HEAD_DIM:(h + 1) * HEAD_DIM] for h in range(HEADS_PER_GROUP)]

    def attend(s, values_ones, vis, all_masked_possible):
        s = jnp.where(vis, s, NEG_BIG)
        m = jnp.max(s, axis=-1, keepdims=True)
        if all_masked_possible:
            m = jnp.where(m < 0.5 * NEG_BIG, 0.0, m)
        p = jnp.exp2(s - m)
        o = jnp.dot(p.astype(BF16), values_ones, preferred_element_type=F32)
        inv = 1.0 / jnp.maximum(o[:, HEAD_DIM:], 1e-30)
        return o[:, :HEAD_DIM] * inv, p, inv

    w0 = pl.multiple_of(jnp.maximum(q0 + tq - win_len, 0), tq)
    kwin = kw_ref[pl.ds(w0, win_len), :]
    vwin = jnp.concatenate([vw_ref[pl.ds(w0, win_len), :], jnp.ones((win_len, HEAD_DIM), BF16)], axis=1)
    t_w = lax.broadcasted_iota(jnp.int32, (tq, win_len), 0) + q0
    kp_w = lax.broadcasted_iota(jnp.int32, (tq, win_len), 1) + w0
    vis_w = lax.bitcast_convert_type(t_w - kp_w, jnp.uint32) < WINDOW
    s_w = [_nt_dot(qh, kwin) for qh in q_heads]
    for h in range(HEADS_PER_GROUP):
        o_w, _, _ = attend(s_w[h], vwin, vis_w, False)
        g_w = gates[:, 2 * HEADS_PER_GROUP + h:2 * HEADS_PER_GROUP + h + 1]
        part_ref[:, h * HEAD_DIM:(h + 1) * HEAD_DIM] = g_w * o_w

    def compressed(n_keys):
        kc = kc_ref[0, 0, 0, :n_keys, :]
        vc = jnp.concatenate([vc_ref[0, 0, 0, :n_keys, :], jnp.ones((n_keys, HEAD_DIM), BF16)], axis=1)
        t_c = lax.broadcasted_iota(jnp.int32, (tq, n_keys), 0) + q0
        n_c = lax.broadcasted_iota(jnp.int32, (tq, n_keys), 1)
        vis_c = n_c * CMP_STRIDE + (CMP_BLOCK - 1) <= t_c
        p_sum = jnp.zeros((tq, n_keys), F32)
        s_c = [_nt_dot(qh, kc) for qh in q_heads]
        for h in range(HEADS_PER_GROUP):
            sl = slice(h * HEAD_DIM, (h + 1) * HEAD_DIM)
            o_c, p, inv = attend(s_c[h], vc, vis_c, True)
            p_sum = p_sum + p * jnp.concatenate([inv] * (n_keys // LANES), axis=1)
            g_c = gates[:, 0 * HEADS_PER_GROUP + h:0 * HEADS_PER_GROUP + h + 1]
            part_ref[:, sl] = part_ref[:, sl] + g_c * o_c
        p_hi = p_sum.astype(BF16)
        p_lo = (p_sum - p_hi.astype(F32)).astype(BF16)
        imp_ref[...] = (jnp.dot(p_hi, ov_ref[:n_keys, :], preferred_element_type=F32)
                        + jnp.dot(p_lo, ov_ref[:n_keys, :], preferred_element_type=F32))

    n_visible = (q0 + tq - CMP_BLOCK) // CMP_STRIDE + 1
    n_chunks = jnp.minimum((n_visible + key_chunk - 1) // key_chunk, n_cmp // key_chunk)
    for c in range(1, n_cmp // key_chunk + 1):
        @pl.when(n_chunks == c)
        def _():
            compressed(c * key_chunk)

    imp_t = imp_ref[...].T
    t_b = lax.broadcasted_iota(jnp.int32, (nblkp, tq), 1) + q0
    blk = lax.broadcasted_iota(jnp.int32, (nblkp, tq), 0)
    cur = lax.shift_right_logical(t_b, 6)
    near = lax.bitcast_convert_type(cur - blk, jnp.uint32) < 2
    val = jnp.where(near, TAKEN, jnp.where(blk * SEL_BLOCK <= t_b, imp_t, -1.0))
    val = jnp.where(blk == 0, TAKEN, val)
    val_ref[...] = jnp.where(blk < n_blocks, val, -3.0)
    blk_f = blk.astype(F32)
    n_top = min(N_SELECT, n_blocks)
    quota = n_top - 1 - jnp.minimum(cur[0:1, :], 2)

    def pick(it, carry):
        v = val_ref[...]
        m = jnp.max(v, axis=0, keepdims=True)
        idx = jnp.min(jnp.where(v == m, blk_f, float(nblkp)), axis=0, keepdims=True)
        idx = jnp.where(it < quota, idx, -1.0)
        val_ref[...] = jnp.where(blk_f == idx, TAKEN, v)
        return carry

    lax.fori_loop(0, n_top - 1 - jnp.minimum(lax.shift_right_logical(q0, 6), 2), pick, 0)
    selb_ref[0, 0] = jnp.where(val_ref[...].T == TAKEN, 0.0, NEG_BIG).astype(BF16)


def _cmpwin(proj, cmp_kv, ov, batch, seq):
    tq = min(ATT_TQ, seq)
    nq = seq // tq
    n_cmp = seq // CMP_STRIDE
    n_blocks = seq // SEL_BLOCK
    nblkp = ov.shape[1]
    win_len = min(WINDOW + tq, seq)
    q_cols = HEADS_PER_GROUP * HEAD_DIM
    key_chunk = min(CMP_KEY_CHUNK, n_cmp)
    kernel = functools.partial(_cmpwin_kernel, n_blocks=n_blocks, win_len=win_len, key_chunk=key_chunk)
    return pl.pallas_call(
        kernel,
        out_shape=(jax.ShapeDtypeStruct((batch * seq, NSA_HEADS * HEAD_DIM), F32),
                   jax.ShapeDtypeStruct((batch, NSA_KV_GROUPS, seq, nblkp), BF16)),
        grid=(batch, NSA_KV_GROUPS, nq),
        in_specs=[
            pl.BlockSpec((tq, q_cols), lambda b, g, i: (b * nq + i, BLK_NQ * LANES // q_cols + g)),
            pl.BlockSpec((1, 1, 1, n_cmp, HEAD_DIM), lambda b, g, i: (0, b, g, 0, 0)),
            pl.BlockSpec((1, 1, 1, n_cmp, HEAD_DIM), lambda b, g, i: (1, b, g, 0, 0)),
            pl.BlockSpec((seq, HEAD_DIM), lambda b, g, i: (b, BLK_KW + g)),
            pl.BlockSpec((seq, HEAD_DIM), lambda b, g, i: (b, BLK_VW + g)),
            pl.BlockSpec((tq, LANES), lambda b, g, i: (b * nq + i, BLK_NG + g)),
            pl.BlockSpec((n_cmp, nblkp), lambda b, g, i: (0, 0)),
        ],
        out_specs=(pl.BlockSpec((tq, q_cols), lambda b, g, i: (b * nq + i, g)),
                   pl.BlockSpec((1, 1, tq, nblkp), lambda b, g, i: (b, g, i, 0))),
        scratch_shapes=[pltpu.VMEM((tq, nblkp), F32), pltpu.VMEM((nblkp, tq), F32)],
        compiler_params=pltpu.CompilerParams(
            dimension_semantics=("arbitrary", "arbitrary", "arbitrary"), vmem_limit_bytes=VMEM_LIMIT),
        name="cmpwin",
    )(proj, cmp_kv, cmp_kv, proj, proj, proj, ov)


def _select_kernel(q_ref, selb_ref, ks_ref, vs_ref, hot_ref, part_ref, ng_ref, nz_ref, y_ref,
                   qaug_ref, acc_ref, m_ref, p_ref, alpha_ref, *, tk):
    tq = q_ref.shape[0]
    n_halves = selb_ref.shape[-1] // LANES
    q0 = pl.program_id(2) * tq

    for h in range(HEADS_PER_GROUP):
        qh = q_ref[:, h * HEAD_DIM:(h + 1) * HEAD_DIM]
        for half in range(n_halves):
            qaug_ref[half * HEADS_PER_GROUP + h, :, :HEAD_DIM] = qh
            qaug_ref[half * HEADS_PER_GROUP + h, :, HEAD_DIM:] = selb_ref[0, 0, :, half * LANES:(half + 1) * LANES]
    acc_ref[...] = jnp.zeros_like(acc_ref)
    m_ref[...] = jnp.full_like(m_ref, NEG_BIG)

    def tile_step(kt, pending, causal):
        if kt is not None:
            k0 = pl.multiple_of(kt * tk, tk)
            hot0 = pl.multiple_of(k0 % hot_ref.shape[0], tk)
            k_t = jnp.concatenate([ks_ref[pl.ds(k0, tk), :], hot_ref[pl.ds(hot0, tk), :]], axis=1)
            half = k0 // (SEL_BLOCK * LANES)
            if causal:
                t_pos = lax.broadcasted_iota(jnp.int32, (tq, tk), 0) + q0
                k_pos = lax.broadcasted_iota(jnp.int32, (tq, tk), 1) + k0
                vis = k_pos <= t_pos
        if pending is not None:
            v_t = jnp.concatenate([vs_ref[pl.ds(pl.multiple_of(pending * tk, tk), tk), :],
                                   jnp.ones((tk, HEAD_DIM), BF16)], axis=1)
        if kt is not None:
            s_next = _nt_dot(qaug_ref[half * HEADS_PER_GROUP], k_t)
        for h in range(HEADS_PER_GROUP):
            if kt is not None:
                s = s_next
                if h + 1 < HEADS_PER_GROUP:
                    s_next = _nt_dot(qaug_ref[half * HEADS_PER_GROUP + h + 1], k_t)
            if pending is not None:
                alpha = alpha_ref[h]
                acc_ref[h] = (acc_ref[h] * jnp.concatenate([alpha, alpha], axis=1)
                              + jnp.dot(p_ref[h], v_t, preferred_element_type=F32))
            if kt is not None:
                if causal:
                    s = jnp.where(vis, s, NEG_BIG)
                m_prev = m_ref[h]
                m_new = jnp.maximum(m_prev, jnp.max(s, axis=-1, keepdims=True))
                p_ref[h] = jnp.exp2(s - jnp.concatenate([m_new] * (tk // LANES), axis=1)).astype(BF16)
                alpha_ref[h] = jnp.exp2(m_prev - m_new)
                m_ref[h] = m_new

    n_full = q0 // tk

    @pl.when(n_full > 0)
    def _():
        tile_step(0, None, False)

    @pl.when(n_full == 0)
    def _():
        p_ref[...] = jnp.zeros_like(p_ref)
        alpha_ref[...] = jnp.ones_like(alpha_ref)

    def step(kt, carry):
        tile_step(kt, kt - 1, False)
        return carry

    lax.fori_loop(1, n_full, step, 0)
    tile_step(n_full, jnp.maximum(n_full - 1, 0), True)
    tile_step(None, n_full, False)

    gates = _sigmoid(ng_ref[...].astype(F32))
    for h in range(HEADS_PER_GROUP):
        sl = slice(h * HEAD_DIM, (h + 1) * HEAD_DIM)
        acc = acc_ref[h]
        o_s = acc[:, :HEAD_DIM] / jnp.maximum(acc[:, HEAD_DIM:], 1e-30)
        g_s = gates[:, 1 * HEADS_PER_GROUP + h:1 * HEADS_PER_GROUP + h + 1]
        z = nz_ref[:, sl].astype(F32)
        y_ref[:, sl] = ((part_ref[:, sl] + g_s * o_s) * (z * _sigmoid(z))).astype(BF16)


def _select(proj, selb, block_hot, part, batch, seq):
    tq = min(SEL_TQ, seq)
    tk = min(SEL_TK, seq)
    nq = seq // tq
    nblkp = selb.shape[-1]
    q_cols = HEADS_PER_GROUP * HEAD_DIM
    kernel = functools.partial(_select_kernel, tk=tk)
    return pl.pallas_call(
        kernel,
        out_shape=jax.ShapeDtypeStruct((batch * seq, NSA_HEADS * HEAD_DIM), BF16),
        grid=(batch, NSA_KV_GROUPS, nq),
        in_specs=[
            pl.BlockSpec((tq, q_cols), lambda b, g, i: (b * nq + i, BLK_NQ * LANES // q_cols + g)),
            pl.BlockSpec((1, 1, tq, nblkp), lambda b, g, i: (b, g, i, 0)),
            pl.BlockSpec((seq, HEAD_DIM), lambda b, g, i: (b, BLK_KS + g), pipeline_mode=pl.Buffered(1)),
            pl.BlockSpec((seq, HEAD_DIM), lambda b, g, i: (b, BLK_VS + g), pipeline_mode=pl.Buffered(1)),
            pl.BlockSpec(block_hot.shape, lambda b, g, i: (0, 0), pipeline_mode=pl.Buffered(1)),
            pl.BlockSpec((tq, q_cols), lambda b, g, i: (b * nq + i, g)),
            pl.BlockSpec((tq, LANES), lambda b, g, i: (b * nq + i, BLK_NG + g)),
            pl.BlockSpec((tq, q_cols), lambda b, g, i: (b * nq + i, BLK_NZ * LANES // q_cols + g)),
        ],
        out_specs=pl.BlockSpec((tq, q_cols), lambda b, g, i: (b * nq + i, g)),
        scratch_shapes=[
            pltpu.VMEM((nblkp // LANES * HEADS_PER_GROUP, tq, 2 * HEAD_DIM), BF16),
            pltpu.VMEM((HEADS_PER_GROUP, tq, 2 * HEAD_DIM), F32),
            pltpu.VMEM((HEADS_PER_GROUP, tq, LANES), F32),
            pltpu.VMEM((HEADS_PER_GROUP, tq, tk), BF16),
            pltpu.VMEM((HEADS_PER_GROUP, tq, LANES), F32),
        ],
        compiler_params=pltpu.CompilerParams(
            dimension_semantics=("arbitrary", "arbitrary", "arbitrary"), vmem_limit_bytes=VMEM_LIMIT),
        name="select",
    )(proj, selb, proj, proj, block_hot, part, proj, proj)


def _outproj_kernel(yh_ref, yn_ref, w_ref, x_ref, g_ref, o_ref):
    half = yh_ref.shape[1]
    z = (jnp.dot(yh_ref[...], w_ref[:half, :], preferred_element_type=F32)
         + jnp.dot(yn_ref[...], w_ref[half:, :], preferred_element_type=F32))
    ms = jnp.mean(z * z, axis=-1, keepdims=True)
    o_ref[...] = x_ref[...] + z * lax.rsqrt(ms + NORM_EPS) * g_ref[...]


def _outproj(yh, yn, w, x2, g):
    n = x2.shape[0]
    tm = min(OUT_TM, n)
    return pl.pallas_call(
        _outproj_kernel,
        out_shape=jax.ShapeDtypeStruct((n, D_MODEL), F32),
        grid=(n // tm,),
        in_specs=[
            pl.BlockSpec((tm, yh.shape[1]), lambda i: (i, 0)),
            pl.BlockSpec((tm, yn.shape[1]), lambda i: (i, 0)),
            pl.BlockSpec(w.shape, lambda i: (0, 0)),
            pl.BlockSpec((tm, D_MODEL), lambda i: (i, 0)),
            pl.BlockSpec((1, D_MODEL), lambda i: (0, 0)),
        ],
        out_specs=pl.BlockSpec((tm, D_MODEL), lambda i: (i, 0)),
        compiler_params=pltpu.CompilerParams(
            dimension_semantics=("arbitrary",), vmem_limit_bytes=VMEM_LIMIT),
        name="outproj",
    )(yh, yn, w, x2, g)


def _permute_weights(w):
    def seg(name, width):
        return w[:, _SRC[name]:_SRC[name] + width]

    ng = seg("ng", NSA_HEADS * N_BRANCHES).reshape(-1, NSA_KV_GROUPS, HEADS_PER_GROUP, N_BRANCHES)
    ng = ng.transpose(0, 1, 3, 2).reshape(-1, NSA_KV_GROUPS, N_BRANCHES * HEADS_PER_GROUP)
    ng = jnp.pad(ng, ((0, 0), (0, 0), (0, LANES - N_BRANCHES * HEADS_PER_GROUP))).reshape(-1, NSA_KV_GROUPS * LANES)
    placed = [(BLK_HQ, seg("hq", 1024)), (BLK_HI, seg("hi", 1024)), (BLK_HZ, seg("hz", 1024)),
              (BLK_NQ, seg("nq", 1024) * (HEAD_DIM ** -0.5 * LOG2_E)),
              (BLK_KS, seg("ks", 256)), (BLK_KW, seg("kw", 256)), (BLK_KC, seg("kc", 256)),
              (BLK_VC, seg("vc", 256)), (BLK_VS, seg("vs", 256)), (BLK_VW, seg("vw", 256)),
              (BLK_NZ, seg("nz", 1024)), (BLK_NG, ng), (BLK_HF, seg("hf", 1024))]
    parts, col = [], 0
    for blk, cols in sorted(placed, key=lambda p: p[0]):
        if blk * LANES > col:
            parts.append(jnp.zeros((w.shape[0], blk * LANES - col), w.dtype))
        parts.append(cols)
        col = blk * LANES + cols.shape[1]
    if col < PROJ_COLS:
        parts.append(jnp.zeros((w.shape[0], PROJ_COLS - col), w.dtype))
    return jnp.concatenate(parts, axis=1).astype(BF16)


def _rope_tables(pos):
    half = ROT_DIM // 2
    inv = ROPE_THETA ** (-2.0 * jnp.arange(half, dtype=F32) / ROT_DIM)
    ang = pos.astype(F32)[:, None] * inv[None, :]
    cos, sin = jnp.cos(ang), jnp.sin(ang)
    n = pos.shape[0]
    pad = jnp.zeros((n, HEAD_DIM - ROT_DIM), F32)
    zero = jnp.zeros((n, half), F32)
    c = jnp.concatenate([cos, cos, pad + 1.0], axis=1)
    s1 = jnp.concatenate([-sin, zero, pad], axis=1)
    s2 = jnp.concatenate([zero, sin, pad], axis=1)
    return c, s1, s2


def _pair_levels():
    t = np.arange(HGRN_CHUNK)[:, None]
    s = np.arange(HGRN_CHUNK)[None, :]
    high_bit = 2 ** np.floor(np.log2(np.maximum(t ^ s, 1))).astype(np.int64)
    return jnp.asarray(np.where(t > s, high_bit, np.where(t == s, 0, -1)), jnp.int32)


def _block_one_hot(seq):
    rows = min(seq, SEL_BLOCK * LANES)
    lane = (np.arange(rows) // SEL_BLOCK) % LANES
    return jnp.asarray(lane[:, None] == np.arange(LANES)[None, :], BF16)


def _overlap_matrix(n_cmp, n_blocks, nblkp):
    start = np.arange(n_cmp)[:, None] * CMP_STRIDE
    sel = np.arange(nblkp)[None, :] * SEL_BLOCK
    ov = (start < sel + SEL_BLOCK) & (start + CMP_BLOCK > sel)
    ov &= np.arange(nblkp)[None, :] < n_blocks
    ov &= np.arange(n_cmp)[:, None] < (n_cmp - 1)
    return jnp.asarray(ov, BF16)


def kernel(x, pre_norm, post_norm, w_in, hgrn_lb_logits, hgrn_out_norm, cmp_pos_k, cmp_w1_k, cmp_b1_k,
           cmp_w2_k, cmp_pos_v, cmp_w1_v, cmp_b1_v, cmp_w2_v, w_out):
    batch, seq, _ = x.shape
    depth = w_in.shape[0]
    n_cmp = seq // CMP_STRIDE
    n_blocks = seq // SEL_BLOCK
    nblkp = -(-n_blocks // LANES) * LANES

    lb_probs = jax.nn.softmax(hgrn_lb_logits.astype(F32), axis=0)
    lower = jnp.maximum(jnp.cumsum(lb_probs, axis=0) - lb_probs[0:1], 0.0)
    lower = lower.reshape(depth, HGRN_HEADS, 1, HEAD_DIM)
    g_out = jnp.broadcast_to(hgrn_out_norm.astype(F32)[:, None, None, :], lower.shape)
    hgrn_par = jnp.concatenate(
        [jnp.log(lower), jnp.log1p(-lower), 1.0 - lower, g_out, jnp.zeros((depth, HGRN_HEADS, 4, HEAD_DIM), F32)],
        axis=2)

    rope_c, rope_s1, rope_s2 = _rope_tables(jnp.arange(seq))
    cmp_tab_k = jnp.stack(_rope_tables(jnp.arange(n_cmp) * CMP_STRIDE + CMP_BLOCK - 1))
    cmp_tab_v = jnp.stack([jnp.ones((n_cmp, HEAD_DIM), F32), jnp.zeros((n_cmp, HEAD_DIM), F32),
                           jnp.zeros((n_cmp, HEAD_DIM), F32)])
    cmp_tabs = jnp.stack([cmp_tab_k, cmp_tab_v])
    ov = _overlap_matrix(n_cmp, n_blocks, nblkp)
    tri = jnp.asarray(np.tril(np.ones((HGRN_CHUNK, HGRN_CHUNK), np.float32)))
    lvl = _pair_levels()
    block_hot = _block_one_hot(seq)

    x2 = x.reshape(batch * seq, D_MODEL)
    for layer in range(depth):
        proj, proj_f32 = _proj(x2, pre_norm[layer][None, :], _permute_weights(w_in[layer]),
                               rope_c, rope_s1, rope_s2, seq)
        y_h = _hgrn(proj, proj_f32, hgrn_par[layer], tri, lvl, batch, seq)
        cmp_kv = _compress(
            proj,
            jnp.stack([cmp_w1_k[layer], cmp_w1_v[layer]]).astype(BF16),
            jnp.stack([cmp_b1_k[layer], cmp_b1_v[layer]])[:, None, :],
            jnp.stack([cmp_w2_k[layer], cmp_w2_v[layer]]).astype(BF16),
            jnp.stack([cmp_pos_k[layer], cmp_pos_v[layer]]),
            cmp_tabs, batch, seq)
        part, selb = _cmpwin(proj, cmp_kv, ov, batch, seq)
        y_n = _select(proj, selb, block_hot, part, batch, seq)
        x2 = _outproj(y_h, y_n, w_out[layer].astype(BF16), x2, post_norm[layer][None, :])
    return x2.reshape(batch, seq, D_MODEL)
```

```python
import functools

import numpy as np
import jax
import jax.numpy as jnp
from jax import lax
from jax.experimental import pallas as pl
from jax.experimental.pallas import tpu as pltpu

F32 = jnp.float32
BF16 = jnp.bfloat16

D_MODEL = 1024
HEAD_DIM = 128
HGRN_HEADS = 8
NSA_HEADS = 8
NSA_KV_GROUPS = 2
HEADS_PER_GROUP = NSA_HEADS // NSA_KV_GROUPS
N_BRANCHES = 3
HGRN_CHUNK = 64
CMP_BLOCK = 32
CMP_STRIDE = 16
SEL_BLOCK = 64
N_SELECT = 16
WINDOW = 512
ROPE_THETA = 500000.0
ROT_DIM = HEAD_DIM // 4
NORM_EPS = 1e-6
TAKEN = -4.0
NEG_BIG = -1e30
LOG2_E = 1.4426950408889634

LANES = 128
SUBLANES = 8
BLK_HQ, BLK_HI, BLK_HZ = 0, 8, 16
BLK_NQ, BLK_KS, BLK_KW, BLK_KC, BLK_VC, BLK_VS, BLK_VW = 24, 32, 34, 36, 38, 40, 42
BLK_NZ, BLK_NG, BLK_HF = 44, 52, 56
PROJ_BLOCKS = 64
PROJ_COLS = PROJ_BLOCKS * LANES
PROJ_BF16_COLS = BLK_HF * LANES
_SRC = dict(hq=0, hf=1024, hi=2048, hz=3072, nq=4096, kc=5120, vc=5376, ks=5632, vs=5888,
            kw=6144, vw=6400, ng=6656, nz=6680)

PROJ_TM = 1024
PROJ_TN = 1024
ROPE_FULL_TILE = BLK_NQ * LANES // PROJ_TN
ROPE_HEAD_TILE = BLK_KS * LANES // PROJ_TN
ROPE_HEAD_BLOCKS = 4
F32_TILE = BLK_HF * LANES // PROJ_TN
HGRN_T = 512
HGRN_HEADS_PER_STEP = 4
ATT_TQ = 256
CMP_KEY_CHUNK = 256
SEL_TQ = 512
SEL_TK = 1024
OUT_TM = 512
VMEM_LIMIT = 56 * 1024 * 1024


def _nt_dot(a, b):
    return lax.dot_general(a, b, (((1,), (1,)), ((), ())), preferred_element_type=F32)


def _sigmoid(x):
    return 1.0 / (1.0 + jnp.exp(-x))


def _proj_kernel(x_ref, g_ref, w_ref, c_ref, s1_ref, s2_ref, o_ref, o32_ref, h_ref, acc_ref):
    j = pl.program_id(1)

    @pl.when(j == 0)
    def _():
        x = x_ref[...]
        ms = jnp.mean(x * x, axis=-1, keepdims=True)
        h_ref[...] = (x * lax.rsqrt(ms + NORM_EPS) * g_ref[...]).astype(BF16)

    @pl.when(j == F32_TILE)
    def _():
        o32_ref[...] = jnp.dot(h_ref[...], w_ref[...], preferred_element_type=F32)

    def store(n_rope_blocks):
        acc_ref[...] = jnp.dot(h_ref[...], w_ref[...], preferred_element_type=F32)
        for blk in range(n_rope_blocks):
            sl = slice(blk * LANES, (blk + 1) * LANES)
            a = acc_ref[:, sl]
            o_ref[:, sl] = (a * c_ref[...] + pltpu.roll(a, LANES - ROT_DIM // 2, 1) * s1_ref[...]
                            + pltpu.roll(a, ROT_DIM // 2, 1) * s2_ref[...]).astype(BF16)
        if n_rope_blocks * LANES < PROJ_TN:
            rest = slice(n_rope_blocks * LANES, PROJ_TN)
            o_ref[:, rest] = acc_ref[:, rest].astype(BF16)

    @pl.when(j == ROPE_FULL_TILE)
    def _():
        store(PROJ_TN // LANES)

    @pl.when(j == ROPE_HEAD_TILE)
    def _():
        store(ROPE_HEAD_BLOCKS)

    @pl.when((j != ROPE_FULL_TILE) & (j != ROPE_HEAD_TILE) & (j != F32_TILE))
    def _():
        o_ref[...] = jnp.dot(h_ref[...], w_ref[...], preferred_element_type=F32).astype(BF16)


def _proj(x2, g, w, rope_c, rope_s1, rope_s2, seq):
    n = x2.shape[0]
    tm = min(PROJ_TM, seq)
    pos_tiles = seq // tm
    return pl.pallas_call(
        _proj_kernel,
        out_shape=(jax.ShapeDtypeStruct((n, PROJ_BF16_COLS), BF16),
                   jax.ShapeDtypeStruct((n, PROJ_COLS - PROJ_BF16_COLS), F32)),
        grid=(n // tm, PROJ_COLS // PROJ_TN),
        in_specs=[
            pl.BlockSpec((tm, D_MODEL), lambda i, j: (i, 0)),
            pl.BlockSpec((1, D_MODEL), lambda i, j: (0, 0)),
            pl.BlockSpec((D_MODEL, PROJ_TN), lambda i, j: (0, j)),
            pl.BlockSpec((tm, LANES), lambda i, j: (i % pos_tiles, 0)),
            pl.BlockSpec((tm, LANES), lambda i, j: (i % pos_tiles, 0)),
            pl.BlockSpec((tm, LANES), lambda i, j: (i % pos_tiles, 0)),
        ],
        out_specs=(pl.BlockSpec((tm, PROJ_TN), lambda i, j: (i, jnp.minimum(j, F32_TILE - 1))),
                   pl.BlockSpec((tm, PROJ_TN), lambda i, j: (i, 0))),
        scratch_shapes=[pltpu.VMEM((tm, D_MODEL), BF16), pltpu.VMEM((tm, PROJ_TN), F32)],
        compiler_params=pltpu.CompilerParams(
            dimension_semantics=("arbitrary", "arbitrary"), vmem_limit_bytes=VMEM_LIMIT),
        name="proj",
    )(x2, g, w, rope_c, rope_s1, rope_s2)


def _hgrn_kernel(q_ref, f_ref, i_ref, z_ref, par_ref, tri_ref, lvl_ref, y_ref, st_ref, o_ref):
    @pl.when(pl.program_id(2) == 0)
    def _():
        st_ref[...] = jnp.zeros_like(st_ref)

    t = q_ref.shape[0]
    n_heads = q_ref.shape[1] // HEAD_DIM
    nc = t // HGRN_CHUNK
    units = n_heads * nc
    shape3 = (units, HGRN_CHUNK, HEAD_DIM)

    def stacked(ref):
        return jnp.concatenate([ref[:, h * HEAD_DIM:(h + 1) * HEAD_DIM] for h in range(n_heads)], axis=0)

    lf_heads, k_heads = [], []
    for h in range(n_heads):
        log_lb, log_1m_lb, one_m_lb = par_ref[h, 0:1, :], par_ref[h, 1:2, :], par_ref[h, 2:3, :]
        x = f_ref[:, h * HEAD_DIM:(h + 1) * HEAD_DIM]
        e = jnp.exp(-jnp.abs(x))
        c = log_1m_lb + jnp.minimum(x, 0.0) - jnp.log(1.0 + e)
        lf_heads.append(jnp.maximum(log_lb, c) + jnp.log(1.0 + jnp.exp(-jnp.abs(log_lb - c))))
        k_heads.append(one_m_lb * jnp.where(x >= 0.0, e, 1.0) / (1.0 + e))
    k3 = jnp.concatenate(k_heads, axis=0).reshape(shape3)
    q3 = stacked(q_ref).astype(F32).reshape(shape3)
    v3 = stacked(i_ref).astype(F32).reshape(shape3)
    v_bf = v3.astype(BF16)

    lf3 = (jnp.concatenate(lf_heads, axis=0) * LOG2_E).reshape(shape3)
    b3 = jnp.stack([jnp.dot(tri_ref[...], lf3[u], precision=lax.Precision.HIGHEST,
                            preferred_element_type=F32) for u in range(units)])

    rows = units * HGRN_CHUNK
    tiles = (rows // SUBLANES, SUBLANES, HEAD_DIM)
    b_tile = b3.reshape(tiles)
    tile_row = lax.broadcasted_iota(jnp.int32, tiles, 1)
    lvl = lvl_ref[...]
    attn = jnp.where(lvl == 0, jnp.stack([_nt_dot(q3[u].astype(BF16), k3[u].astype(BF16)) for u in range(units)]), 0.0)
    n = HGRN_CHUNK // 2
    while n >= 1:
        if 2 * n >= SUBLANES:
            blocks = (rows // (2 * n), 2 * n, HEAD_DIM)
            ref = jnp.broadcast_to(b3.reshape(blocks)[:, n - 1:n, :], blocks).reshape(shape3)
            side = jnp.concatenate([k3.reshape(blocks)[:, :n], q3.reshape(blocks)[:, n:]], axis=1).reshape(shape3)
        else:
            ref = jnp.broadcast_to(b_tile[:, n - 1:n, :], tiles)
            for lo in range(2 * n, SUBLANES, 2 * n):
                ref = jnp.where(tile_row >= lo, jnp.broadcast_to(b_tile[:, lo + n - 1:lo + n, :], tiles), ref)
            ref = ref.reshape(shape3)
            side = jnp.where((tile_row & n) != 0, q3.reshape(tiles), k3.reshape(tiles)).reshape(shape3)
        d = b3 - ref
        r_n = (side * jnp.exp2(jnp.minimum(d, -d))).astype(BF16)
        a_n = jnp.stack([_nt_dot(r_n[u], r_n[u]) for u in range(units)])
        attn = jnp.where(lvl == n, a_n, attn)
        n //= 2
    attn_bf = attn.astype(BF16)
    o_intra = jnp.stack([jnp.dot(attn_bf[u], v_bf[u], preferred_element_type=F32) for u in range(units)])

    b_last = b3[:, HGRN_CHUNK - 1:HGRN_CHUNK, :]
    decay = jnp.exp2(b_last)
    q_dec = (q3 * jnp.exp2(b3)).astype(BF16)
    k_dec = (k3 * jnp.exp2(b_last - b3)).astype(BF16)
    upd = [lax.dot_general(v_bf[u], k_dec[u], (((0,), (0,)), ((), ())), preferred_element_type=F32)
           for u in range(units)]
    st = [st_ref[h] for h in range(n_heads)]
    for ci in range(nc):
        for h in range(n_heads):
            u = h * nc + ci
            o_ref[h, ci * HGRN_CHUNK:(ci + 1) * HGRN_CHUNK, :] = (
                o_intra[u] + _nt_dot(q_dec[u], st[h].astype(BF16)))
            st[h] = st[h] * decay[u] + upd[u]
    for h in range(n_heads):
        st_ref[h] = st[h]
        cols = slice(h * HEAD_DIM, (h + 1) * HEAD_DIM)
        o = o_ref[h]
        ms = jnp.mean(o * o, axis=-1, keepdims=True)
        z = z_ref[:, cols].astype(F32)
        y_ref[:, cols] = (o * lax.rsqrt(ms + NORM_EPS) * par_ref[h, 3:4, :] * (z * _sigmoid(z))).astype(BF16)


def _hgrn(proj, proj_f32, par, tri, lvl, batch, seq):
    t = min(HGRN_T, seq)
    nt = seq // t

    hp = HGRN_HEADS_PER_STEP
    width = hp * HEAD_DIM

    def col(base):
        return pl.BlockSpec((t, width), lambda b, h, ti: (b * nt + ti, base // hp + h))

    return pl.pallas_call(
        _hgrn_kernel,
        out_shape=jax.ShapeDtypeStruct((batch * seq, HGRN_HEADS * HEAD_DIM), BF16),
        grid=(batch, HGRN_HEADS // hp, nt),
        in_specs=[
            col(BLK_HQ), col(0), col(BLK_HI), col(BLK_HZ),
            pl.BlockSpec((hp, 8, HEAD_DIM), lambda b, h, ti: (h, 0, 0)),
            pl.BlockSpec((HGRN_CHUNK, HGRN_CHUNK), lambda b, h, ti: (0, 0)),
            pl.BlockSpec((HGRN_CHUNK, HGRN_CHUNK), lambda b, h, ti: (0, 0)),
        ],
        out_specs=pl.BlockSpec((t, width), lambda b, h, ti: (b * nt + ti, h)),
        scratch_shapes=[pltpu.VMEM((hp, HEAD_DIM, HEAD_DIM), F32), pltpu.VMEM((hp, t, HEAD_DIM), F32)],
        compiler_params=pltpu.CompilerParams(
            dimension_semantics=("arbitrary", "arbitrary", "arbitrary"), vmem_limit_bytes=VMEM_LIMIT),
        name="hgrn",
    )(proj, proj_f32, proj, proj, par, tri, lvl)


def _compress_kernel(x_ref, w1_ref, b1_ref, w2_ref, pos_ref, tab_ref, o_ref, x32_ref):
    n_rows = o_ref.shape[-2]
    x32_ref[...] = x_ref[...].astype(F32)
    acc_lo = jnp.zeros((n_rows, HEAD_DIM), F32)
    acc_hi = jnp.zeros((n_rows, HEAD_DIM), F32)
    for r in range(CMP_STRIDE):
        xr = x32_ref[pl.ds(r, n_rows, stride=CMP_STRIDE), :]
        lo, hi = r, CMP_STRIDE + r
        acc_lo = acc_lo + jnp.dot((xr + pos_ref[0, lo:lo + 1, :]).astype(BF16),
                                  w1_ref[0, lo * HEAD_DIM:(lo + 1) * HEAD_DIM, :],
                                  preferred_element_type=F32)
        acc_hi = acc_hi + jnp.dot((xr + pos_ref[0, hi:hi + 1, :]).astype(BF16),
                                  w1_ref[0, hi * HEAD_DIM:(hi + 1) * HEAD_DIM, :],
                                  preferred_element_type=F32)
    pre = acc_lo + pltpu.roll(acc_hi, n_rows - 1, 0) + b1_ref[0]
    mid = pre * _sigmoid(pre)
    out = jnp.dot(mid.astype(BF16), w2_ref[0], preferred_element_type=F32)
    o_ref[0, 0, 0] = (out * tab_ref[0, 0] + pltpu.roll(out, LANES - ROT_DIM // 2, 1) * tab_ref[0, 1]
                      + pltpu.roll(out, ROT_DIM // 2, 1) * tab_ref[0, 2]).astype(BF16)


def _compress(proj, w1, b1, w2, pos, tabs, batch, seq):
    n_rows = seq // CMP_STRIDE
    return pl.pallas_call(
        _compress_kernel,
        out_shape=jax.ShapeDtypeStruct((2, batch, NSA_KV_GROUPS, n_rows, HEAD_DIM), BF16),
        grid=(2, batch, NSA_KV_GROUPS),
        in_specs=[
            pl.BlockSpec((seq, HEAD_DIM), lambda kv, b, g: (b, BLK_KC + 2 * kv + g)),
            pl.BlockSpec((1, CMP_BLOCK * HEAD_DIM, HEAD_DIM), lambda kv, b, g: (kv, 0, 0)),
            pl.BlockSpec((1, 1, HEAD_DIM), lambda kv, b, g: (kv, 0, 0)),
            pl.BlockSpec((1, HEAD_DIM, HEAD_DIM), lambda kv, b, g: (kv, 0, 0)),
            pl.BlockSpec((1, CMP_BLOCK, HEAD_DIM), lambda kv, b, g: (kv, 0, 0)),
            pl.BlockSpec((1, 3, n_rows, HEAD_DIM), lambda kv, b, g: (kv, 0, 0, 0)),
        ],
        out_specs=pl.BlockSpec((1, 1, 1, n_rows, HEAD_DIM), lambda kv, b, g: (kv, b, g, 0, 0)),
        scratch_shapes=[pltpu.VMEM((seq, HEAD_DIM), F32)],
        compiler_params=pltpu.CompilerParams(
            dimension_semantics=("arbitrary", "arbitrary", "arbitrary"), vmem_limit_bytes=VMEM_LIMIT),
        name="compress",
    )(proj, w1, b1, w2, pos, tabs)


def _cmpwin_kernel(q_ref, kc_ref, vc_ref, kw_ref, vw_ref, ng_ref, ov_ref, part_ref, selb_ref,
                   imp_ref, val_ref, *, n_blocks, win_len, key_chunk):
    tq = q_ref.shape[0]
    n_cmp = kc_ref.shape[-2]
    nblkp = selb_ref.shape[-1]
    q0 = pl.program_id(2) * tq
    gates = _sigmoid(ng_ref[...].astype(F32))
    q_heads = [q_ref[:, h * HEAD_DIM:(h + 1) * HEAD_DIM] for h in range(HEADS_PER_GROUP)]

    def attend(s, values_ones, vis, all_masked_possible):
        s = jnp.where(vis, s, NEG_BIG)
        m = jnp.max(s, axis=-1, keepdims=True)
        if all_masked_possible:
            m = jnp.where(m < 0.5 * NEG_BIG, 0.0, m)
        p = jnp.exp2(s - m)
        o = jnp.dot(p.astype(BF16), values_ones, preferred_element_type=F32)
        inv = 1.0 / jnp.maximum(o[:, HEAD_DIM:], 1e-30)
        return o[:, :HEAD_DIM] * inv, p, inv

    w0 = pl.multiple_of(jnp.maximum(q0 + tq - win_len, 0), tq)
    kwin = kw_ref[pl.ds(w0, win_len), :]
    vwin = jnp.concatenate([vw_ref[pl.ds(w0, win_len), :], jnp.ones((win_len, HEAD_DIM), BF16)], axis=1)
    t_w = lax.broadcasted_iota(jnp.int32, (tq, win_len), 0) + q0
    kp_w = lax.broadcasted_iota(jnp.int32, (tq, win_len), 1) + w0
    vis_w = lax.bitcast_convert_type(t_w - kp_w, jnp.uint32) < WINDOW
    s_w = [_nt_dot(qh, kwin) for qh in q_heads]
    for h in range(HEADS_PER_GROUP):
        o_w, _, _ = attend(s_w[h], vwin, vis_w, False)
        g_w = gates[:, 2 * HEADS_PER_GROUP + h:2 * HEADS_PER_GROUP + h + 1]
        part_ref[:, h * HEAD_DIM:(h + 1) * HEAD_DIM] = g_w * o_w

    def compressed(n_keys):
        kc = kc_ref[0, 0, 0, :n_keys, :]
        vc = jnp.concatenate([vc_ref[0, 0, 0, :n_keys, :], jnp.ones((n_keys, HEAD_DIM), BF16)], axis=1)
        t_c = lax.broadcasted_iota(jnp.int32, (tq, n_keys), 0) + q0
        n_c = lax.broadcasted_iota(jnp.int32, (tq, n_keys), 1)
        vis_c = n_c * CMP_STRIDE + (CMP_BLOCK - 1) <= t_c
        p_sum = jnp.zeros((tq, n_keys), F32)
        s_c = [_nt_dot(qh, kc) for qh in q_heads]
        for h in range(HEADS_PER_GROUP):
            sl = slice(h * HEAD_DIM, (h + 1) * HEAD_DIM)
            o_c, p, inv = attend(s_c[h], vc, vis_c, True)
            p_sum = p_sum + p * jnp.concatenate([inv] * (n_keys // LANES), axis=1)
            g_c = gates[:, 0 * HEADS_PER_GROUP + h:0 * HEADS_PER_GROUP + h + 1]
            part_ref[:, sl] = part_ref[:, sl] + g_c * o_c
        p_hi = p_sum.astype(BF16)
        p_lo = (p_sum - p_hi.astype(F32)).astype(BF16)
        imp_ref[...] = (jnp.dot(p_hi, ov_ref[:n_keys, :], preferred_element_type=F32)
                        + jnp.dot(p_lo, ov_ref[:n_keys, :], preferred_element_type=F32))

    n_visible = (q0 + tq - CMP_BLOCK) // CMP_STRIDE + 1
    n_chunks = jnp.minimum((n_visible + key_chunk - 1) // key_chunk, n_cmp // key_chunk)
    for c in range(1, n_cmp // key_chunk + 1):
        @pl.when(n_chunks == c)
        def _():
            compressed(c * key_chunk)

    imp_t = imp_ref[...].T
    t_b = lax.broadcasted_iota(jnp.int32, (nblkp, tq), 1) + q0
    blk = lax.broadcasted_iota(jnp.int32, (nblkp, tq), 0)
    cur = lax.shift_right_logical(t_b, 6)
    near = lax.bitcast_convert_type(cur - blk, jnp.uint32) < 2
    val = jnp.where(near, TAKEN, jnp.where(blk * SEL_BLOCK <= t_b, imp_t, -1.0))
    val = jnp.where(blk == 0, TAKEN, val)
    val_ref[...] = jnp.where(blk < n_blocks, val, -3.0)
    blk_f = blk.astype(F32)
    n_top = min(N_SELECT, n_blocks)
    quota = n_top - 1 - jnp.minimum(cur[0:1, :], 2)

    def pick(it, carry):
        v = val_ref[...]
        m = jnp.max(v, axis=0, keepdims=True)
        idx = jnp.min(jnp.where(v == m, blk_f, float(nblkp)), axis=0, keepdims=True)
        idx = jnp.where(it < quota, idx, -1.0)
        val_ref[...] = jnp.where(blk_f == idx, TAKEN, v)
        return carry

    lax.fori_loop(0, n_top - 1 - jnp.minimum(lax.shift_right_logical(q0, 6), 2), pick, 0)
    selb_ref[0, 0] = jnp.where(val_ref[...].T == TAKEN, 0.0, NEG_BIG).astype(BF16)


def _cmpwin(proj, cmp_kv, ov, batch, seq):
    tq = min(ATT_TQ, seq)
    nq = seq // tq
    n_cmp = seq // CMP_STRIDE
    n_blocks = seq // SEL_BLOCK
    nblkp = ov.shape[1]
    win_len = min(WINDOW + tq, seq)
    q_cols = HEADS_PER_GROUP * HEAD_DIM
    key_chunk = min(CMP_KEY_CHUNK, n_cmp)
    kernel = functools.partial(_cmpwin_kernel, n_blocks=n_blocks, win_len=win_len, key_chunk=key_chunk)
    return pl.pallas_call(
        kernel,
        out_shape=(jax.ShapeDtypeStruct((batch * seq, NSA_HEADS * HEAD_DIM), F32),
                   jax.ShapeDtypeStruct((batch, NSA_KV_GROUPS, seq, nblkp), BF16)),
        grid=(batch, NSA_KV_GROUPS, nq),
        in_specs=[
            pl.BlockSpec((tq, q_cols), lambda b, g, i: (b * nq + i, BLK_NQ * LANES // q_cols + g)),
            pl.BlockSpec((1, 1, 1, n_cmp, HEAD_DIM), lambda b, g, i: (0, b, g, 0, 0)),
            pl.BlockSpec((1, 1, 1, n_cmp, HEAD_DIM), lambda b, g, i: (1, b, g, 0, 0)),
            pl.BlockSpec((seq, HEAD_DIM), lambda b, g, i: (b, BLK_KW + g)),
            pl.BlockSpec((seq, HEAD_DIM), lambda b, g, i: (b, BLK_VW + g)),
            pl.BlockSpec((tq, LANES), lambda b, g, i: (b * nq + i, BLK_NG + g)),
            pl.BlockSpec((n_cmp, nblkp), lambda b, g, i: (0, 0)),
        ],
        out_specs=(pl.BlockSpec((tq, q_cols), lambda b, g, i: (b * nq + i, g)),
                   pl.BlockSpec((1, 1, tq, nblkp), lambda b, g, i: (b, g, i, 0))),
        scratch_shapes=[pltpu.VMEM((tq, nblkp), F32), pltpu.VMEM((nblkp, tq), F32)],
        compiler_params=pltpu.CompilerParams(
            dimension_semantics=("arbitrary", "arbitrary", "arbitrary"), vmem_limit_bytes=VMEM_LIMIT),
        name="cmpwin",
    )(proj, cmp_kv, cmp_kv, proj, proj, proj, ov)


def _select_kernel(q_ref, selb_ref, ks_ref, vs_ref, hot_ref, part_ref, ng_ref, nz_ref, y_ref,
                   qaug_ref, acc_ref, m_ref, p_ref, alpha_ref, *, tk):
    tq = q_ref.shape[0]
    n_halves = selb_ref.shape[-1] // LANES
    q0 = pl.program_id(2) * tq

    for h in range(HEADS_PER_GROUP):
        qh = q_ref[:, h * HEAD_DIM:(h + 1) * HEAD_DIM]
        for half in range(n_halves):
            qaug_ref[half * HEADS_PER_GROUP + h, :, :HEAD_DIM] = qh
            qaug_ref[half * HEADS_PER_GROUP + h, :, HEAD_DIM:] = selb_ref[0, 0, :, half * LANES:(half + 1) * LANES]
    acc_ref[...] = jnp.zeros_like(acc_ref)
    m_ref[...] = jnp.full_like(m_ref, NEG_BIG)

    def tile_step(kt, pending, causal):
        if kt is not None:
            k0 = pl.multiple_of(kt * tk, tk)
            hot0 = pl.multiple_of(k0 % hot_ref.shape[0], tk)
            k_t = jnp.concatenate([ks_ref[pl.ds(k0, tk), :], hot_ref[pl.ds(hot0, tk), :]], axis=1)
            half = k0 // (SEL_BLOCK * LANES)
            if causal:
                t_pos = lax.broadcasted_iota(jnp.int32, (tq, tk), 0) + q0
                k_pos = lax.broadcasted_iota(jnp.int32, (tq, tk), 1) + k0
                vis = k_pos <= t_pos
        if pending is not None:
            v_t = jnp.concatenate([vs_ref[pl.ds(pl.multiple_of(pending * tk, tk), tk), :],
                                   jnp.ones((tk, HEAD_DIM), BF16)], axis=1)
        if kt is not None:
            s_next = _nt_dot(qaug_ref[half * HEADS_PER_GROUP], k_t)
        for h in range(HEADS_PER_GROUP):
            if kt is not None:
                s = s_next
                if h + 1 < HEADS_PER_GROUP:
                    s_next = _nt_dot(qaug_ref[half * HEADS_PER_GROUP + h + 1], k_t)
            if pending is not None:
                alpha = alpha_ref[h]
                acc_ref[h] = (acc_ref[h] * jnp.concatenate([alpha, alpha], axis=1)
                              + jnp.dot(p_ref[h], v_t, preferred_element_type=F32))
            if kt is not None:
                if causal:
                    s = jnp.where(vis, s, NEG_BIG)
                m_prev = m_ref[h]
                m_new = jnp.maximum(m_prev, jnp.max(s, axis=-1, keepdims=True))
                p_ref[h] = jnp.exp2(s - jnp.concatenate([m_new] * (tk // LANES), axis=1)).astype(BF16)
                alpha_ref[h] = jnp.exp2(m_prev - m_new)
                m_ref[h] = m_new

    n_full = q0 // tk

    @pl.when(n_full > 0)
    def _():
        tile_step(0, None, False)

    @pl.when(n_full == 0)
    def _():
        p_ref[...] = jnp.zeros_like(p_ref)
        alpha_ref[...] = jnp.ones_like(alpha_ref)

    def step(kt, carry):
        tile_step(kt, kt - 1, False)
        return carry

    lax.fori_loop(1, n_full, step, 0)
    tile_step(n_full, jnp.maximum(n_full - 1, 0), True)
    tile_step(None, n_full, False)

    gates = _sigmoid(ng_ref[...].astype(F32))
    for h in range(HEADS_PER_GROUP):
        sl = slice(h * HEAD_DIM, (h + 1) * HEAD_DIM)
        acc = acc_ref[h]
        o_s = acc[:, :HEAD_DIM] / jnp.maximum(acc[:, HEAD_DIM:], 1e-30)
        g_s = gates[:, 1 * HEADS_PER_GROUP + h:1 * HEADS_PER_GROUP + h + 1]
        z = nz_ref[:, sl].astype(F32)
        y_ref[:, sl] = ((part_ref[:, sl] + g_s * o_s) * (z * _sigmoid(z))).astype(BF16)


def _select(proj, selb, block_hot, part, batch, seq):
    tq = min(SEL_TQ, seq)
    tk = min(SEL_TK, seq)
    nq = seq // tq
    nblkp = selb.shape[-1]
    q_cols = HEADS_PER_GROUP * HEAD_DIM
    kernel = functools.partial(_select_kernel, tk=tk)
    return pl.pallas_call(
        kernel,
        out_shape=jax.ShapeDtypeStruct((batch * seq, NSA_HEADS * HEAD_DIM), BF16),
        grid=(batch, NSA_KV_GROUPS, nq),
        in_specs=[
            pl.BlockSpec((tq, q_cols), lambda b, g, i: (b * nq + i, BLK_NQ * LANES // q_cols + g)),
            pl.BlockSpec((1, 1, tq, nblkp), lambda b, g, i: (b, g, i, 0)),
            pl.BlockSpec((seq, HEAD_DIM), lambda b, g, i: (b, BLK_KS + g), pipeline_mode=pl.Buffered(1)),
            pl.BlockSpec((seq, HEAD_DIM), lambda b, g, i: (b, BLK_VS + g), pipeline_mode=pl.Buffered(1)),
            pl.BlockSpec(block_hot.shape, lambda b, g, i: (0, 0), pipeline_mode=pl.Buffered(1)),
            pl.BlockSpec((tq, q_cols), lambda b, g, i: (b * nq + i, g)),
            pl.BlockSpec((tq, LANES), lambda b, g, i: (b * nq + i, BLK_NG + g)),
            pl.BlockSpec((tq, q_cols), lambda b, g, i: (b * nq + i, BLK_NZ * LANES // q_cols + g)),
        ],
        out_specs=pl.BlockSpec((tq, q_cols), lambda b, g, i: (b * nq + i, g)),
        scratch_shapes=[
            pltpu.VMEM((nblkp // LANES * HEADS_PER_GROUP, tq, 2 * HEAD_DIM), BF16),
            pltpu.VMEM((HEADS_PER_GROUP, tq, 2 * HEAD_DIM), F32),
            pltpu.VMEM((HEADS_PER_GROUP, tq, LANES), F32),
            pltpu.VMEM((HEADS_PER_GROUP, tq, tk), BF16),
            pltpu.VMEM((HEADS_PER_GROUP, tq, LANES), F32),
        ],
        compiler_params=pltpu.CompilerParams(
            dimension_semantics=("arbitrary", "arbitrary", "arbitrary"), vmem_limit_bytes=VMEM_LIMIT),
        name="select",
    )(proj, selb, proj, proj, block_hot, part, proj, proj)


def _outproj_kernel(yh_ref, yn_ref, w_ref, x_ref, g_ref, o_ref):
    half = yh_ref.shape[1]
    z = (jnp.dot(yh_ref[...], w_ref[:half, :], preferred_element_type=F32)
         + jnp.dot(yn_ref[...], w_ref[half:, :], preferred_element_type=F32))
    ms = jnp.mean(z * z, axis=-1, keepdims=True)
    o_ref[...] = x_ref[...] + z * lax.rsqrt(ms + NORM_EPS) * g_ref[...]


def _outproj(yh, yn, w, x2, g):
    n = x2.shape[0]
    tm = min(OUT_TM, n)
    return pl.pallas_call(
        _outproj_kernel,
        out_shape=jax.ShapeDtypeStruct((n, D_MODEL), F32),
        grid=(n // tm,),
        in_specs=[
            pl.BlockSpec((tm, yh.shape[1]), lambda i: (i, 0)),
            pl.BlockSpec((tm, yn.shape[1]), lambda i: (i, 0)),
            pl.BlockSpec(w.shape, lambda i: (0, 0)),
            pl.BlockSpec((tm, D_MODEL), lambda i: (i, 0)),
            pl.BlockSpec((1, D_MODEL), lambda i: (0, 0)),
        ],
        out_specs=pl.BlockSpec((tm, D_MODEL), lambda i: (i, 0)),
        compiler_params=pltpu.CompilerParams(
            dimension_semantics=("arbitrary",), vmem_limit_bytes=VMEM_LIMIT),
        name="outproj",
    )(yh, yn, w, x2, g)


def _permute_weights(w):
    def seg(name, width):
        return w[:, _SRC[name]:_SRC[name] + width]

    ng = seg("ng", NSA_HEADS * N_BRANCHES).reshape(-1, NSA_KV_GROUPS, HEADS_PER_GROUP, N_BRANCHES)
    ng = ng.transpose(0, 1, 3, 2).reshape(-1, NSA_KV_GROUPS, N_BRANCHES * HEADS_PER_GROUP)
    ng = jnp.pad(ng, ((0, 0), (0, 0), (0, LANES - N_BRANCHES * HEADS_PER_GROUP))).reshape(-1, NSA_KV_GROUPS * LANES)
    placed = [(BLK_HQ, seg("hq", 1024)), (BLK_HI, seg("hi", 1024)), (BLK_HZ, seg("hz", 1024)),
              (BLK_NQ, seg("nq", 1024) * (HEAD_DIM ** -0.5 * LOG2_E)),
              (BLK_KS, seg("ks", 256)), (BLK_KW, seg("kw", 256)), (BLK_KC, seg("kc", 256)),
              (BLK_VC, seg("vc", 256)), (BLK_VS, seg("vs", 256)), (BLK_VW, seg("vw", 256)),
              (BLK_NZ, seg("nz", 1024)), (BLK_NG, ng), (BLK_HF, seg("hf", 1024))]
    parts, col = [], 0
    for blk, cols in sorted(placed, key=lambda p: p[0]):
        if blk * LANES > col:
            parts.append(jnp.zeros((w.shape[0], blk * LANES - col), w.dtype))
        parts.append(cols)
        col = blk * LANES + cols.shape[1]
    if col < PROJ_COLS:
        parts.append(jnp.zeros((w.shape[0], PROJ_COLS - col), w.dtype))
    return jnp.concatenate(parts, axis=1).astype(BF16)


def _rope_tables(pos):
    half = ROT_DIM // 2
    inv = ROPE_THETA ** (-2.0 * jnp.arange(half, dtype=F32) / ROT_DIM)
    ang = pos.astype(F32)[:, None] * inv[None, :]
    cos, sin = jnp.cos(ang), jnp.sin(ang)
    n = pos.shape[0]
    pad = jnp.zeros((n, HEAD_DIM - ROT_DIM), F32)
    zero = jnp.zeros((n, half), F32)
    c = jnp.concatenate([cos, cos, pad + 1.0], axis=1)
    s1 = jnp.concatenate([-sin, zero, pad], axis=1)
    s2 = jnp.concatenate([zero, sin, pad], axis=1)
    return c, s1, s2


def _pair_levels():
    t = np.arange(HGRN_CHUNK)[:, None]
    s = np.arange(HGRN_CHUNK)[None, :]
    high_bit = 2 ** np.floor(np.log2(np.maximum(t ^ s, 1))).astype(np.int64)
    return jnp.asarray(np.where(t > s, high_bit, np.where(t == s, 0, -1)), jnp.int32)


def _block_one_hot(seq):
    rows = min(seq, SEL_BLOCK * LANES)
    lane = (np.arange(rows) // SEL_BLOCK) % LANES
    return jnp.asarray(lane[:, None] == np.arange(LANES)[None, :], BF16)


def _overlap_matrix(n_cmp, n_blocks, nblkp):
    start = np.arange(n_cmp)[:, None] * CMP_STRIDE
    sel = np.arange(nblkp)[None, :] * SEL_BLOCK
    ov = (start < sel + SEL_BLOCK) & (start + CMP_BLOCK > sel)
    ov &= np.arange(nblkp)[None, :] < n_blocks
    ov &= np.arange(n_cmp)[:, None] < (n_cmp - 1)
    return jnp.asarray(ov, BF16)


def kernel(x, pre_norm, post_norm, w_in, hgrn_lb_logits, hgrn_out_norm, cmp_pos_k, cmp_w1_k, cmp_b1_k,
           cmp_w2_k, cmp_pos_v, cmp_w1_v, cmp_b1_v, cmp_w2_v, w_out):
    batch, seq, _ = x.shape
    depth = w_in.shape[0]
    n_cmp = seq // CMP_STRIDE
    n_blocks = seq // SEL_BLOCK
    nblkp = -(-n_blocks // LANES) * LANES

    lb_probs = jax.nn.softmax(hgrn_lb_logits.astype(F32), axis=0)
    lower = jnp.maximum(jnp.cumsum(lb_probs, axis=0) - lb_probs[0:1], 0.0)
    lower = lower.reshape(depth, HGRN_HEADS, 1, HEAD_DIM)
    g_out = jnp.broadcast_to(hgrn_out_norm.astype(F32)[:, None, None, :], lower.shape)
    hgrn_par = jnp.concatenate(
        [jnp.log(lower), jnp.log1p(-lower), 1.0 - lower, g_out, jnp.zeros((depth, HGRN_HEADS, 4, HEAD_DIM), F32)],
        axis=2)

    rope_c, rope_s1, rope_s2 = _rope_tables(jnp.arange(seq))
    cmp_tab_k = jnp.stack(_rope_tables(jnp.arange(n_cmp) * CMP_STRIDE + CMP_BLOCK - 1))
    cmp_tab_v = jnp.stack([jnp.ones((n_cmp, HEAD_DIM), F32), jnp.zeros((n_cmp, HEAD_DIM), F32),
                           jnp.zeros((n_cmp, HEAD_DIM), F32)])
    cmp_tabs = jnp.stack([cmp_tab_k, cmp_tab_v])
    ov = _overlap_matrix(n_cmp, n_blocks, nblkp)
    tri = jnp.asarray(np.tril(np.ones((HGRN_CHUNK, HGRN_CHUNK), np.float32)))
    lvl = _pair_levels()
    block_hot = _block_one_hot(seq)

    x2 = x.reshape(batch * seq, D_MODEL)
    for layer in range(depth):
        proj, proj_f32 = _proj(x2, pre_norm[layer][None, :], _permute_weights(w_in[layer]),
                               rope_c, rope_s1, rope_s2, seq)
        y_h = _hgrn(proj, proj_f32, hgrn_par[layer], tri, lvl, batch, seq)
        cmp_kv = _compress(
            proj,
            jnp.stack([cmp_w1_k[layer], cmp_w1_v[layer]]).astype(BF16),
            jnp.stack([cmp_b1_k[layer], cmp_b1_v[layer]])[:, None, :],
            jnp.stack([cmp_w2_k[layer], cmp_w2_v[layer]]).astype(BF16),
            jnp.stack([cmp_pos_k[layer], cmp_pos_v[layer]]),
            cmp_tabs, batch, seq)
        part, selb = _cmpwin(proj, cmp_kv, ov, batch, seq)
        y_n = _select(proj, selb, block_hot, part, batch, seq)
        x2 = _outproj(y_h, y_n, w_out[layer].astype(BF16), x2, post_norm[layer][None, :])
    return x2.reshape(batch, seq, D_MODEL)
```

```python
import functools

import numpy as np
import jax
import jax.numpy as jnp
from jax import lax
from jax.experimental import pallas as pl
from jax.experimental.pallas import tpu as pltpu

F32 = jnp.float32
BF16 = jnp.bfloat16

D_MODEL = 1024
HEAD_DIM = 128
HGRN_HEADS = 8
NSA_HEADS = 8
NSA_KV_GROUPS = 2
HEADS_PER_GROUP = NSA_HEADS // NSA_KV_GROUPS
N_BRANCHES = 3
HGRN_CHUNK = 64
CMP_BLOCK = 32
CMP_STRIDE = 16
SEL_BLOCK = 64
N_SELECT = 16
WINDOW = 512
ROPE_THETA = 500000.0
ROT_DIM = HEAD_DIM // 4
NORM_EPS = 1e-6
TAKEN = -4.0
NEG_BIG = -1e30
LOG2_E = 1.4426950408889634

LANES = 128
SUBLANES = 8
BLK_HQ, BLK_HI, BLK_HZ = 0, 8, 16
BLK_NQ, BLK_KS, BLK_KW, BLK_KC, BLK_VC, BLK_VS, BLK_VW = 24, 32, 34, 36, 38, 40, 42
BLK_NZ, BLK_NG, BLK_HF = 44, 52, 56
PROJ_BLOCKS = 64
PROJ_COLS = PROJ_BLOCKS * LANES
PROJ_BF16_COLS = BLK_HF * LANES
_SRC = dict(hq=0, hf=1024, hi=2048, hz=3072, nq=4096, kc=5120, vc=5376, ks=5632, vs=5888,
            kw=6144, vw=6400, ng=6656, nz=6680)

PROJ_TM = 1024
PROJ_TN = 1024
ROPE_FULL_TILE = BLK_NQ * LANES // PROJ_TN
ROPE_HEAD_TILE = BLK_KS * LANES // PROJ_TN
ROPE_HEAD_BLOCKS = 4
F32_TILE = BLK_HF * LANES // PROJ_TN
HGRN_T = 512
HGRN_HEADS_PER_STEP = 4
ATT_TQ = 256
CMP_KEY_CHUNK = 256
SEL_TQ = 512
SEL_TK = 1024
OUT_TM = 512
VMEM_LIMIT = 56 * 1024 * 1024


def _nt_dot(a, b):
    return lax.dot_general(a, b, (((1,), (1,)), ((), ())), preferred_element_type=F32)


def _sigmoid(x):
    return 1.0 / (1.0 + jnp.exp(-x))


def _proj_kernel(x_ref, g_ref, w_ref, c_ref, s1_ref, s2_ref, o_ref, o32_ref, h_ref, acc_ref):
    j = pl.program_id(1)

    @pl.when(j == 0)
    def _():
        x = x_ref[...]
        ms = jnp.mean(x * x, axis=-1, keepdims=True)
        h_ref[...] = (x * lax.rsqrt(ms + NORM_EPS) * g_ref[...]).astype(BF16)

    @pl.when(j == F32_TILE)
    def _():
        o32_ref[...] = jnp.dot(h_ref[...], w_ref[...], preferred_element_type=F32)

    def store(n_rope_blocks):
        acc_ref[...] = jnp.dot(h_ref[...], w_ref[...], preferred_element_type=F32)
        for blk in range(n_rope_blocks):
            sl = slice(blk * LANES, (blk + 1) * LANES)
            a = acc_ref[:, sl]
            o_ref[:, sl] = (a * c_ref[...] + pltpu.roll(a, LANES - ROT_DIM // 2, 1) * s1_ref[...]
                            + pltpu.roll(a, ROT_DIM // 2, 1) * s2_ref[...]).astype(BF16)
        if n_rope_blocks * LANES < PROJ_TN:
            rest = slice(n_rope_blocks * LANES, PROJ_TN)
            o_ref[:, rest] = acc_ref[:, rest].astype(BF16)

    @pl.when(j == ROPE_FULL_TILE)
    def _():
        store(PROJ_TN // LANES)

    @pl.when(j == ROPE_HEAD_TILE)
    def _():
        store(ROPE_HEAD_BLOCKS)

    @pl.when((j != ROPE_FULL_TILE) & (j != ROPE_HEAD_TILE) & (j != F32_TILE))
    def _():
        o_ref[...] = jnp.dot(h_ref[...], w_ref[...], preferred_element_type=F32).astype(BF16)


def _proj(x2, g, w, rope_c, rope_s1, rope_s2, seq):
    n = x2.shape[0]
    tm = min(PROJ_TM, seq)
    pos_tiles = seq // tm
    return pl.pallas_call(
        _proj_kernel,
        out_shape=(jax.ShapeDtypeStruct((n, PROJ_BF16_COLS), BF16),
                   jax.ShapeDtypeStruct((n, PROJ_COLS - PROJ_BF16_COLS), F32)),
        grid=(n // tm, PROJ_COLS // PROJ_TN),
        in_specs=[
            pl.BlockSpec((tm, D_MODEL), lambda i, j: (i, 0)),
            pl.BlockSpec((1, D_MODEL), lambda i, j: (0, 0)),
            pl.BlockSpec((D_MODEL, PROJ_TN), lambda i, j: (0, j)),
            pl.BlockSpec((tm, LANES), lambda i, j: (i % pos_tiles, 0)),
            pl.BlockSpec((tm, LANES), lambda i, j: (i % pos_tiles, 0)),
            pl.BlockSpec((tm, LANES), lambda i, j: (i % pos_tiles, 0)),
        ],
        out_specs=(pl.BlockSpec((tm, PROJ_TN), lambda i, j: (i, jnp.minimum(j, F32_TILE - 1))),
                   pl.BlockSpec((tm, PROJ_TN), lambda i, j: (i, 0))),
        scratch_shapes=[pltpu.VMEM((tm, D_MODEL), BF16), pltpu.VMEM((tm, PROJ_TN), F32)],
        compiler_params=pltpu.CompilerParams(
            dimension_semantics=("arbitrary", "arbitrary"), vmem_limit_bytes=VMEM_LIMIT),
        name="proj",
    )(x2, g, w, rope_c, rope_s1, rope_s2)


def _hgrn_kernel(q_ref, f_ref, i_ref, z_ref, par_ref, tri_ref, lvl_ref, y_ref, st_ref, o_ref):
    @pl.when(pl.program_id(2) == 0)
    def _():
        st_ref[...] = jnp.zeros_like(st_ref)

    t = q_ref.shape[0]
    n_heads = q_ref.shape[1] // HEAD_DIM
    nc = t // HGRN_CHUNK
    units = n_heads * nc
    shape3 = (units, HGRN_CHUNK, HEAD_DIM)

    def stacked(ref):
        return jnp.concatenate([ref[:, h * HEAD_DIM:(h + 1) * HEAD_DIM] for h in range(n_heads)], axis=0)

    lf_heads, k_heads = [], []
    for h in range(n_heads):
        log_lb, log_1m_lb, one_m_lb = par_ref[h, 0:1, :], par_ref[h, 1:2, :], par_ref[h, 2:3, :]
        x = f_ref[:, h * HEAD_DIM:(h + 1) * HEAD_DIM]
        e = jnp.exp(-jnp.abs(x))
        c = log_1m_lb + jnp.minimum(x, 0.0) - jnp.log(1.0 + e)
        lf_heads.append(jnp.maximum(log_lb, c) + jnp.log(1.0 + jnp.exp(-jnp.abs(log_lb - c))))
        k_heads.append(one_m_lb * jnp.where(x >= 0.0, e, 1.0) / (1.0 + e))
    k3 = jnp.concatenate(k_heads, axis=0).reshape(shape3)
    q3 = stacked(q_ref).astype(F32).reshape(shape3)
    v3 = stacked(i_ref).astype(F32).reshape(shape3)
    v_bf = v3.astype(BF16)

    lf3 = (jnp.concatenate(lf_heads, axis=0) * LOG2_E).reshape(shape3)
    b3 = jnp.stack([jnp.dot(tri_ref[...], lf3[u], precision=lax.Precision.HIGHEST,
                            preferred_element_type=F32) for u in range(units)])

    rows = units * HGRN_CHUNK
    tiles = (rows // SUBLANES, SUBLANES, HEAD_DIM)
    b_tile = b3.reshape(tiles)
    tile_row = lax.broadcasted_iota(jnp.int32, tiles, 1)
    lvl = lvl_ref[...]
    attn = jnp.where(lvl == 0, jnp.stack([_nt_dot(q3[u].astype(BF16), k3[u].astype(BF16)) for u in range(units)]), 0.0)
    n = HGRN_CHUNK // 2
    while n >= 1:
        if 2 * n >= SUBLANES:
            blocks = (rows // (2 * n), 2 * n, HEAD_DIM)
            ref = jnp.broadcast_to(b3.reshape(blocks)[:, n - 1:n, :], blocks).reshape(shape3)
            side = jnp.concatenate([k3.reshape(blocks)[:, :n], q3.reshape(blocks)[:, n:]], axis=1).reshape(shape3)
        else:
            ref = jnp.broadcast_to(b_tile[:, n - 1:n, :], tiles)
            for lo in range(2 * n, SUBLANES, 2 * n):
                ref = jnp.where(tile_row >= lo, jnp.broadcast_to(b_tile[:, lo + n - 1:lo + n, :], tiles), ref)
            ref = ref.reshape(shape3)
            side = jnp.where((tile_row & n) != 0, q3.reshape(tiles), k3.reshape(tiles)).reshape(shape3)
        d = b3 - ref
        r_n = (side * jnp.exp2(jnp.minimum(d, -d))).astype(BF16)
        a_n = jnp.stack([_nt_dot(r_n[u], r_n[u]) for u in range(units)])
        attn = jnp.where(lvl == n, a_n, attn)
        n //= 2
    attn_bf = attn.astype(BF16)
    o_intra = jnp.stack([jnp.dot(attn_bf[u], v_bf[u], preferred_element_type=F32) for u in range(units)])

    b_last = b3[:, HGRN_CHUNK - 1:HGRN_CHUNK, :]
    decay = jnp.exp2(b_last)
    q_dec = (q3 * jnp.exp2(b3)).astype(BF16)
    k_dec = (k3 * jnp.exp2(b_last - b3)).astype(BF16)
    upd = [lax.dot_general(v_bf[u], k_dec[u], (((0,), (0,)), ((), ())), preferred_element_type=F32)
           for u in range(units)]
    st = [st_ref[h] for h in range(n_heads)]
    for ci in range(nc):
        for h in range(n_heads):
            u = h * nc + ci
            o_ref[h, ci * HGRN_CHUNK:(ci + 1) * HGRN_CHUNK, :] = (
                o_intra[u] + _nt_dot(q_dec[u], st[h].astype(BF16)))
            st[h] = st[h] * decay[u] + upd[u]
    for h in range(n_heads):
        st_ref[h] = st[h]
        cols = slice(h * HEAD_DIM, (h + 1) * HEAD_DIM)
        o = o_ref[h]
        ms = jnp.mean(o * o, axis=-1, keepdims=True)
        z = z_ref[:, cols].astype(F32)
        y_ref[:, cols] = (o * lax.rsqrt(ms + NORM_EPS) * par_ref[h, 3:4, :] * (z * _sigmoid(z))).astype(BF16)


def _hgrn(proj, proj_f32, par, tri, lvl, batch, seq):
    t = min(HGRN_T, seq)
    nt = seq // t

    hp = HGRN_HEADS_PER_STEP
    width = hp * HEAD_DIM

    def col(base):
        return pl.BlockSpec((t, width), lambda b, h, ti: (b * nt + ti, base // hp + h))

    return pl.pallas_call(
        _hgrn_kernel,
        out_shape=jax.ShapeDtypeStruct((batch * seq, HGRN_HEADS * HEAD_DIM), BF16),
        grid=(batch, HGRN_HEADS // hp, nt),
        in_specs=[
            col(BLK_HQ), col(0), col(BLK_HI), col(BLK_HZ),
            pl.BlockSpec((hp, 8, HEAD_DIM), lambda b, h, ti: (h, 0, 0)),
            pl.BlockSpec((HGRN_CHUNK, HGRN_CHUNK), lambda b, h, ti: (0, 0)),
            pl.BlockSpec((HGRN_CHUNK, HGRN_CHUNK), lambda b, h, ti: (0, 0)),
        ],
        out_specs=pl.BlockSpec((t, width), lambda b, h, ti: (b * nt + ti, h)),
        scratch_shapes=[pltpu.VMEM((hp, HEAD_DIM, HEAD_DIM), F32), pltpu.VMEM((hp, t, HEAD_DIM), F32)],
        compiler_params=pltpu.CompilerParams(
            dimension_semantics=("arbitrary", "arbitrary", "arbitrary"), vmem_limit_bytes=VMEM_LIMIT),
        name="hgrn",
    )(proj, proj_f32, proj, proj, par, tri, lvl)


def _compress_kernel(x_ref, w1_ref, b1_ref, w2_ref, pos_ref, tab_ref, o_ref, x32_ref):
    n_rows = o_ref.shape[-2]
    x32_ref[...] = x_ref[...].astype(F32)
    acc_lo = jnp.zeros((n_rows, HEAD_DIM), F32)
    acc_hi = jnp.zeros((n_rows, HEAD_DIM), F32)
    for r in range(CMP_STRIDE):
        xr = x32_ref[pl.ds(r, n_rows, stride=CMP_STRIDE), :]
        lo, hi = r, CMP_STRIDE + r
        acc_lo = acc_lo + jnp.dot((xr + pos_ref[0, lo:lo + 1, :]).astype(BF16),
                                  w1_ref[0, lo * HEAD_DIM:(lo + 1) * HEAD_DIM, :],
                                  preferred_element_type=F32)
        acc_hi = acc_hi + jnp.dot((xr + pos_ref[0, hi:hi + 1, :]).astype(BF16),
                                  w1_ref[0, hi * HEAD_DIM:(hi + 1) * HEAD_DIM, :],
                                  preferred_element_type=F32)
    pre = acc_lo + pltpu.roll(acc_hi, n_rows - 1, 0) + b1_ref[0]
    mid = pre * _sigmoid(pre)
    out = jnp.dot(mid.astype(BF16), w2_ref[0], preferred_element_type=F32)
    o_ref[0, 0, 0] = (out * tab_ref[0, 0] + pltpu.roll(out, LANES - ROT_DIM // 2, 1) * tab_ref[0, 1]
                      + pltpu.roll(out, ROT_DIM // 2, 1) * tab_ref[0, 2]).astype(BF16)


def _compress(proj, w1, b1, w2, pos, tabs, batch, seq):
    n_rows = seq // CMP_STRIDE
    return pl.pallas_call(
        _compress_kernel,
        out_shape=jax.ShapeDtypeStruct((2, batch, NSA_KV_GROUPS, n_rows, HEAD_DIM), BF16),
        grid=(2, batch, NSA_KV_GROUPS),
        in_specs=[
            pl.BlockSpec((seq, HEAD_DIM), lambda kv, b, g: (b, BLK_KC + 2 * kv + g)),
            pl.BlockSpec((1, CMP_BLOCK * HEAD_DIM, HEAD_DIM), lambda kv, b, g: (kv, 0, 0)),
            pl.BlockSpec((1, 1, HEAD_DIM), lambda kv, b, g: (kv, 0, 0)),
            pl.BlockSpec((1, HEAD_DIM, HEAD_DIM), lambda kv, b, g: (kv, 0, 0)),
            pl.BlockSpec((1, CMP_BLOCK, HEAD_DIM), lambda kv, b, g: (kv, 0, 0)),
            pl.BlockSpec((1, 3, n_rows, HEAD_DIM), lambda kv, b, g: (kv, 0, 0, 0)),
        ],
        out_specs=pl.BlockSpec((1, 1, 1, n_rows, HEAD_DIM), lambda kv, b, g: (kv, b, g, 0, 0)),
        scratch_shapes=[pltpu.VMEM((seq, HEAD_DIM), F32)],
        compiler_params=pltpu.CompilerParams(
            dimension_semantics=("arbitrary", "arbitrary", "arbitrary"), vmem_limit_bytes=VMEM_LIMIT),
        name="compress",
    )(proj, w1, b1, w2, pos, tabs)


def _cmpwin_kernel(q_ref, kc_ref, vc_ref, kw_ref, vw_ref, ng_ref, ov_ref, part_ref, selb_ref,
                   imp_ref, val_ref, *, n_blocks, win_len, key_chunk):
    tq = q_ref.shape[0]
    n_cmp = kc_ref.shape[-2]
    nblkp = selb_ref.shape[-1]
    q0 = pl.program_id(2) * tq
    gates = _sigmoid(ng_ref[...].astype(F32))
    q_heads = [q_ref[:, h * HEAD_DIM:(h + 1) * HEAD_DIM] for h in range(HEADS_PER_GROUP)]

    def attend(s, values_ones, vis, all_masked_possible):
        s = jnp.where(vis, s, NEG_BIG)
        m = jnp.max(s, axis=-1, keepdims=True)
        if all_masked_possible:
            m = jnp.where(m < 0.5 * NEG_BIG, 0.0, m)
        p = jnp.exp2(s - m)
        o = jnp.dot(p.astype(BF16), values_ones, preferred_element_type=F32)
        inv = 1.0 / jnp.maximum(o[:, HEAD_DIM:], 1e-30)
        return o[:, :HEAD_DIM] * inv, p, inv

    w0 = pl.multiple_of(jnp.maximum(q0 + tq - win_len, 0), tq)
    kwin = kw_ref[pl.ds(w0, win_len), :]
    vwin = jnp.concatenate([vw_ref[pl.ds(w0, win_len), :], jnp.ones((win_len, HEAD_DIM), BF16)], axis=1)
    t_w = lax.broadcasted_iota(jnp.int32, (tq, win_len), 0) + q0
    kp_w = lax.broadcasted_iota(jnp.int32, (tq, win_len), 1) + w0
    vis_w = lax.bitcast_convert_type(t_w - kp_w, jnp.uint32) < WINDOW
    s_w = [_nt_dot(qh, kwin) for qh in q_heads]
    for h in range(HEADS_PER_GROUP):
        o_w, _, _ = attend(s_w[h], vwin, vis_w, False)
        g_w = gates[:, 2 * HEADS_PER_GROUP + h:2 * HEADS_PER_GROUP + h + 1]
        part_ref[:, h * HEAD_DIM:(h + 1) * HEAD_DIM] = g_w * o_w

    def compressed(n_keys):
        kc = kc_ref[0, 0, 0, :n_keys, :]
        vc = jnp.concatenate([vc_ref[0, 0, 0, :n_keys, :], jnp.ones((n_keys, HEAD_DIM), BF16)], axis=1)
        t_c = lax.broadcasted_iota(jnp.int32, (tq, n_keys), 0) + q0
        n_c = lax.broadcasted_iota(jnp.int32, (tq, n_keys), 1)
        vis_c = n_c * CMP_STRIDE + (CMP_BLOCK - 1) <= t_c
        p_sum = jnp.zeros((tq, n_keys), F32)
        s_c = [_nt_dot(qh, kc) for qh in q_heads]
        for h in range(HEADS_PER_GROUP):
            sl = slice(h * HEAD_DIM, (h + 1) * HEAD_DIM)
            o_c, p, inv = attend(s_c[h], vc, vis_c, True)
            p_sum = p_sum + p * jnp.concatenate([inv] * (n_keys // LANES), axis=1)
            g_c = gates[:, 0 * HEADS_PER_GROUP + h:0 * HEADS_PER_GROUP + h + 1]
            part_ref[:, sl] = part_ref[:, sl] + g_c * o_c
        p_hi = p_sum.astype(BF16)
        p_lo = (p_sum - p_hi.astype(F32)).astype(BF16)
        imp_ref[...] = (jnp.dot(p_hi, ov_ref[:n_keys, :], preferred_element_type=F32)
                        + jnp.dot(p_lo, ov_ref[:n_keys, :], preferred_element_type=F32))

    n_visible = (q0 + tq - CMP_BLOCK) // CMP_STRIDE + 1
    n_chunks = jnp.minimum((n_visible + key_chunk - 1) // key_chunk, n_cmp // key_chunk)
    for c in range(1, n_cmp // key_chunk + 1):
        @pl.when(n_chunks == c)
        def _():
            compressed(c * key_chunk)

    imp_t = imp_ref[...].T
    t_b = lax.broadcasted_iota(jnp.int32, (nblkp, tq), 1) + q0
    blk = lax.broadcasted_iota(jnp.int32, (nblkp, tq), 0)
    cur = lax.shift_right_logical(t_b, 6)
    near = lax.bitcast_convert_type(cur - blk, jnp.uint32) < 2
    val = jnp.where(near, TAKEN, jnp.where(blk * SEL_BLOCK <= t_b, imp_t, -1.0))
    val = jnp.where(blk == 0, TAKEN, val)
    val_ref[...] = jnp.where(blk < n_blocks, val, -3.0)
    blk_f = blk.astype(F32)
    n_top = min(N_SELECT, n_blocks)
    quota = n_top - 1 - jnp.minimum(cur[0:1, :], 2)

    def pick(it, carry):
        v = val_ref[...]
        m = jnp.max(v, axis=0, keepdims=True)
        idx = jnp.min(jnp.where(v == m, blk_f, float(nblkp)), axis=0, keepdims=True)
        idx = jnp.where(it < quota, idx, -1.0)
        val_ref[...] = jnp.where(blk_f == idx, TAKEN, v)
        return carry

    lax.fori_loop(0, n_top - 1 - jnp.minimum(lax.shift_right_logical(q0, 6), 2), pick, 0)
    selb_ref[0, 0] = jnp.where(val_ref[...].T == TAKEN, 0.0, NEG_BIG).astype(BF16)


def _cmpwin(proj, cmp_kv, ov, batch, seq):
    tq = min(ATT_TQ, seq)
    nq = seq // tq
    n_cmp = seq // CMP_STRIDE
    n_blocks = seq // SEL_BLOCK
    nblkp = ov.shape[1]
    win_len = min(WINDOW + tq, seq)
    q_cols = HEADS_PER_GROUP * HEAD_DIM
    key_chunk = min(CMP_KEY_CHUNK, n_cmp)
    kernel = functools.partial(_cmpwin_kernel, n_blocks=n_blocks, win_len=win_len, key_chunk=key_chunk)
    return pl.pallas_call(
        kernel,
        out_shape=(jax.ShapeDtypeStruct((batch * seq, NSA_HEADS * HEAD_DIM), F32),
                   jax.ShapeDtypeStruct((batch, NSA_KV_GROUPS, seq, nblkp), BF16)),
        grid=(batch, NSA_KV_GROUPS, nq),
        in_specs=[
            pl.BlockSpec((tq, q_cols), lambda b, g, i: (b * nq + i, BLK_NQ * LANES // q_cols + g)),
            pl.BlockSpec((1, 1, 1, n_cmp, HEAD_DIM), lambda b, g, i: (0, b, g, 0, 0)),
            pl.BlockSpec((1, 1, 1, n_cmp, HEAD_DIM), lambda b, g, i: (1, b, g, 0, 0)),
            pl.BlockSpec((seq, HEAD_DIM), lambda b, g, i: (b, BLK_KW + g)),
            pl.BlockSpec((seq, HEAD_DIM), lambda b, g, i: (b, BLK_VW + g)),
            pl.BlockSpec((tq, LANES), lambda b, g, i: (b * nq + i, BLK_NG + g)),
            pl.BlockSpec((n_cmp, nblkp), lambda b, g, i: (0, 0)),
        ],
        out_specs=(pl.BlockSpec((tq, q_cols), lambda b, g, i: (b * nq + i, g)),
                   pl.BlockSpec((1, 1, tq, nblkp), lambda b, g, i: (b, g, i, 0))),
        scratch_shapes=[pltpu.VMEM((tq, nblkp), F32), pltpu.VMEM((nblkp, tq), F32)],
        compiler_params=pltpu.CompilerParams(
            dimension_semantics=("arbitrary", "arbitrary", "arbitrary"), vmem_limit_bytes=VMEM_LIMIT),
        name="cmpwin",
    )(proj, cmp_kv, cmp_kv, proj, proj, proj, ov)


def _select_kernel(q_ref, selb_ref, ks_ref, vs_ref, hot_ref, part_ref, ng_ref, nz_ref, y_ref,
                   qaug_ref, acc_ref, m_ref, p_ref, alpha_ref, *, tk):
    tq = q_ref.shape[0]
    n_halves = selb_ref.shape[-1] // LANES
    q0 = pl.program_id(2) * tq

    for h in range(HEADS_PER_GROUP):
        qh = q_ref[:, h * HEAD_DIM:(h + 1) * HEAD_DIM]
        for half in range(n_halves):
            qaug_ref[half * HEADS_PER_GROUP + h, :, :HEAD_DIM] = qh
            qaug_ref[half * HEADS_PER_GROUP + h, :, HEAD_DIM:] = selb_ref[0, 0, :, half * LANES:(half + 1) * LANES]
    acc_ref[...] = jnp.zeros_like(acc_ref)
    m_ref[...] = jnp.full_like(m_ref, NEG_BIG)

    def tile_step(k0, width, pend_k0, pend_width, causal):
        if k0 is not None:
            k0 = pl.multiple_of(k0, tq)
            hot0 = pl.multiple_of(k0 % hot_ref.shape[0], tq)
            k_t = jnp.concatenate([ks_ref[pl.ds(k0, width), :], hot_ref[pl.ds(hot0, width), :]], axis=1)
            half = k0 // (SEL_BLOCK * LANES)
            if causal:
                t_pos = lax.broadcasted_iota(jnp.int32, (tq, width), 0) + q0
                k_pos = lax.broadcasted_iota(jnp.int32, (tq, width), 1) + k0
                vis = k_pos <= t_pos
        if pend_k0 is not None:
            v_t = jnp.concatenate([vs_ref[pl.ds(pl.multiple_of(pend_k0, tq), pend_width), :],
                                   jnp.ones((pend_width, HEAD_DIM), BF16)], axis=1)
        if k0 is not None:
            s_next = _nt_dot(qaug_ref[half * HEADS_PER_GROUP], k_t)
        for h in range(HEADS_PER_GROUP):
            if k0 is not None:
                s = s_next
                if h + 1 < HEADS_PER_GROUP:
                    s_next = _nt_dot(qaug_ref[half * HEADS_PER_GROUP + h + 1], k_t)
            if pend_k0 is not None:
                alpha = alpha_ref[h]
                acc_ref[h] = (acc_ref[h] * jnp.concatenate([alpha, alpha], axis=1)
                              + jnp.dot(p_ref[h, :, :pend_width], v_t, preferred_element_type=F32))
            if k0 is not None:
                if causal:
                    s = jnp.where(vis, s, NEG_BIG)
                m_prev = m_ref[h]
                m_new = jnp.maximum(m_prev, jnp.max(s, axis=-1, keepdims=True))
                p_ref[h, :, :width] = jnp.exp2(s - jnp.concatenate([m_new] * (width // LANES), axis=1)).astype(BF16)
                alpha_ref[h] = jnp.exp2(m_prev - m_new)
                m_ref[h] = m_new

    n_full = q0 // tk

    @pl.when(n_full > 0)
    def _():
        tile_step(0, tk, None, None, False)

    @pl.when(n_full == 0)
    def _():
        p_ref[...] = jnp.zeros_like(p_ref)
        alpha_ref[...] = jnp.ones_like(alpha_ref)

    def step(kt, carry):
        tile_step(kt * tk, tk, (kt - 1) * tk, tk, False)
        return carry

    lax.fori_loop(1, n_full, step, 0)

    def diagonal(width):
        tile_step(n_full * tk, width, jnp.maximum(n_full - 1, 0) * tk, tk, True)
        tile_step(None, None, n_full * tk, width, False)

    if tk == 2 * tq:
        @pl.when(q0 % tk == 0)
        def _():
            diagonal(tq)

        @pl.when(q0 % tk != 0)
        def _():
            diagonal(tk)
    else:
        diagonal(tk)

    gates = _sigmoid(ng_ref[...].astype(F32))
    for h in range(HEADS_PER_GROUP):
        sl = slice(h * HEAD_DIM, (h + 1) * HEAD_DIM)
        acc = acc_ref[h]
        o_s = acc[:, :HEAD_DIM] / jnp.maximum(acc[:, HEAD_DIM:], 1e-30)
        g_s = gates[:, 1 * HEADS_PER_GROUP + h:1 * HEADS_PER_GROUP + h + 1]
        z = nz_ref[:, sl].astype(F32)
        y_ref[:, sl] = ((part_ref[:, sl] + g_s * o_s) * (z * _sigmoid(z))).astype(BF16)


def _select(proj, selb, block_hot, part, batch, seq):
    tq = min(SEL_TQ, seq)
    tk = min(SEL_TK, seq)
    nq = seq // tq
    nblkp = selb.shape[-1]
    q_cols = HEADS_PER_GROUP * HEAD_DIM
    kernel = functools.partial(_select_kernel, tk=tk)
    return pl.pallas_call(
        kernel,
        out_shape=jax.ShapeDtypeStruct((batch * seq, NSA_HEADS * HEAD_DIM), BF16),
        grid=(batch, NSA_KV_GROUPS, nq),
        in_specs=[
            pl.BlockSpec((tq, q_cols), lambda b, g, i: (b * nq + i, BLK_NQ * LANES // q_cols + g)),
            pl.BlockSpec((1, 1, tq, nblkp), lambda b, g, i: (b, g, i, 0)),
            pl.BlockSpec((seq, HEAD_DIM), lambda b, g, i: (b, BLK_KS + g), pipeline_mode=pl.Buffered(1)),
            pl.BlockSpec((seq, HEAD_DIM), lambda b, g, i: (b, BLK_VS + g), pipeline_mode=pl.Buffered(1)),
            pl.BlockSpec(block_hot.shape, lambda b, g, i: (0, 0), pipeline_mode=pl.Buffered(1)),
            pl.BlockSpec((tq, q_cols), lambda b, g, i: (b * nq + i, g)),
            pl.BlockSpec((tq, LANES), lambda b, g, i: (b * nq + i, BLK_NG + g)),
            pl.BlockSpec((tq, q_cols), lambda b, g, i: (b * nq + i, BLK_NZ * LANES // q_cols + g)),
        ],
        out_specs=pl.BlockSpec((tq, q_cols), lambda b, g, i: (b * nq + i, g)),
        scratch_shapes=[
            pltpu.VMEM((nblkp // LANES * HEADS_PER_GROUP, tq, 2 * HEAD_DIM), BF16),
            pltpu.VMEM((HEADS_PER_GROUP, tq, 2 * HEAD_DIM), F32),
            pltpu.VMEM((HEADS_PER_GROUP, tq, LANES), F32),
            pltpu.VMEM((HEADS_PER_GROUP, tq, tk), BF16),
            pltpu.VMEM((HEADS_PER_GROUP, tq, LANES), F32),
        ],
        compiler_params=pltpu.CompilerParams(
            dimension_semantics=("arbitrary", "arbitrary", "arbitrary"), vmem_limit_bytes=VMEM_LIMIT),
        name="select",
    )(proj, selb, proj, proj, block_hot, part, proj, proj)


def _outproj_kernel(yh_ref, yn_ref, w_ref, x_ref, g_ref, o_ref):
    half = yh_ref.shape[1]
    z = (jnp.dot(yh_ref[...], w_ref[:half, :], preferred_element_type=F32)
         + jnp.dot(yn_ref[...], w_ref[half:, :], preferred_element_type=F32))
    ms = jnp.mean(z * z, axis=-1, keepdims=True)
    o_ref[...] = x_ref[...] + z * lax.rsqrt(ms + NORM_EPS) * g_ref[...]


def _outproj(yh, yn, w, x2, g):
    n = x2.shape[0]
    tm = min(OUT_TM, n)
    return pl.pallas_call(
        _outproj_kernel,
        out_shape=jax.ShapeDtypeStruct((n, D_MODEL), F32),
        grid=(n // tm,),
        in_specs=[
            pl.BlockSpec((tm, yh.shape[1]), lambda i: (i, 0)),
            pl.BlockSpec((tm, yn.shape[1]), lambda i: (i, 0)),
            pl.BlockSpec(w.shape, lambda i: (0, 0)),
            pl.BlockSpec((tm, D_MODEL), lambda i: (i, 0)),
            pl.BlockSpec((1, D_MODEL), lambda i: (0, 0)),
        ],
        out_specs=pl.BlockSpec((tm, D_MODEL), lambda i: (i, 0)),
        compiler_params=pltpu.CompilerParams(
            dimension_semantics=("arbitrary",), vmem_limit_bytes=VMEM_LIMIT),
        name="outproj",
    )(yh, yn, w, x2, g)


def _permute_weights(w):
    def seg(name, width):
        return w[:, _SRC[name]:_SRC[name] + width]

    ng = seg("ng", NSA_HEADS * N_BRANCHES).reshape(-1, NSA_KV_GROUPS, HEADS_PER_GROUP, N_BRANCHES)
    ng = ng.transpose(0, 1, 3, 2).reshape(-1, NSA_KV_GROUPS, N_BRANCHES * HEADS_PER_GROUP)
    ng = jnp.pad(ng, ((0, 0), (0, 0), (0, LANES - N_BRANCHES * HEADS_PER_GROUP))).reshape(-1, NSA_KV_GROUPS * LANES)
    placed = [(BLK_HQ, seg("hq", 1024)), (BLK_HI, seg("hi", 1024)), (BLK_HZ, seg("hz", 1024)),
              (BLK_NQ, seg("nq", 1024) * (HEAD_DIM ** -0.5 * LOG2_E)),
              (BLK_KS, seg("ks", 256)), (BLK_KW, seg("kw", 256)), (BLK_KC, seg("kc", 256)),
              (BLK_VC, seg("vc", 256)), (BLK_VS, seg("vs", 256)), (BLK_VW, seg("vw", 256)),
              (BLK_NZ, seg("nz", 1024)), (BLK_NG, ng), (BLK_HF, seg("hf", 1024))]
    parts, col = [], 0
    for blk, cols in sorted(placed, key=lambda p: p[0]):
        if blk * LANES > col:
            parts.append(jnp.zeros((w.shape[0], blk * LANES - col), w.dtype))
        parts.append(cols)
        col = blk * LANES + cols.shape[1]
    if col < PROJ_COLS:
        parts.append(jnp.zeros((w.shape[0], PROJ_COLS - col), w.dtype))
    return jnp.concatenate(parts, axis=1).astype(BF16)


def _rope_tables(pos):
    half = ROT_DIM // 2
    inv = ROPE_THETA ** (-2.0 * jnp.arange(half, dtype=F32) / ROT_DIM)
    ang = pos.astype(F32)[:, None] * inv[None, :]
    cos, sin = jnp.cos(ang), jnp.sin(ang)
    n = pos.shape[0]
    pad = jnp.zeros((n, HEAD_DIM - ROT_DIM), F32)
    zero = jnp.zeros((n, half), F32)
    c = jnp.concatenate([cos, cos, pad + 1.0], axis=1)
    s1 = jnp.concatenate([-sin, zero, pad], axis=1)
    s2 = jnp.concatenate([zero, sin, pad], axis=1)
    return c, s1, s2


def _pair_levels():
    t = np.arange(HGRN_CHUNK)[:, None]
    s = np.arange(HGRN_CHUNK)[None, :]
    high_bit = 2 ** np.floor(np.log2(np.maximum(t ^ s, 1))).astype(np.int64)
    return jnp.asarray(np.where(t > s, high_bit, np.where(t == s, 0, -1)), jnp.int32)


def _block_one_hot(seq):
    rows = min(seq, SEL_BLOCK * LANES)
    lane = (np.arange(rows) // SEL_BLOCK) % LANES
    return jnp.asarray(lane[:, None] == np.arange(LANES)[None, :], BF16)


def _overlap_matrix(n_cmp, n_blocks, nblkp):
    start = np.arange(n_cmp)[:, None] * CMP_STRIDE
    sel = np.arange(nblkp)[None, :] * SEL_BLOCK
    ov = (start < sel + SEL_BLOCK) & (start + CMP_BLOCK > sel)
    ov &= np.arange(nblkp)[None, :] < n_blocks
    ov &= np.arange(n_cmp)[:, None] < (n_cmp - 1)
    return jnp.asarray(ov, BF16)


def kernel(x, pre_norm, post_norm, w_in, hgrn_lb_logits, hgrn_out_norm, cmp_pos_k, cmp_w1_k, cmp_b1_k,
           cmp_w2_k, cmp_pos_v, cmp_w1_v, cmp_b1_v, cmp_w2_v, w_out):
    batch, seq, _ = x.shape
    depth = w_in.shape[0]
    n_cmp = seq // CMP_STRIDE
    n_blocks = seq // SEL_BLOCK
    nblkp = -(-n_blocks // LANES) * LANES

    lb_probs = jax.nn.softmax(hgrn_lb_logits.astype(F32), axis=0)
    lower = jnp.maximum(jnp.cumsum(lb_probs, axis=0) - lb_probs[0:1], 0.0)
    lower = lower.reshape(depth, HGRN_HEADS, 1, HEAD_DIM)
    g_out = jnp.broadcast_to(hgrn_out_norm.astype(F32)[:, None, None, :], lower.shape)
    hgrn_par = jnp.concatenate(
        [jnp.log(lower), jnp.log1p(-lower), 1.0 - lower, g_out, jnp.zeros((depth, HGRN_HEADS, 4, HEAD_DIM), F32)],
        axis=2)

    rope_c, rope_s1, rope_s2 = _rope_tables(jnp.arange(seq))
    cmp_tab_k = jnp.stack(_rope_tables(jnp.arange(n_cmp) * CMP_STRIDE + CMP_BLOCK - 1))
    cmp_tab_v = jnp.stack([jnp.ones((n_cmp, HEAD_DIM), F32), jnp.zeros((n_cmp, HEAD_DIM), F32),
                           jnp.zeros((n_cmp, HEAD_DIM), F32)])
    cmp_tabs = jnp.stack([cmp_tab_k, cmp_tab_v])
    ov = _overlap_matrix(n_cmp, n_blocks, nblkp)
    tri = jnp.asarray(np.tril(np.ones((HGRN_CHUNK, HGRN_CHUNK), np.float32)))
    lvl = _pair_levels()
    block_hot = _block_one_hot(seq)

    x2 = x.reshape(batch * seq, D_MODEL)
    for layer in range(depth):
        proj, proj_f32 = _proj(x2, pre_norm[layer][None, :], _permute_weights(w_in[layer]),
                               rope_c, rope_s1, rope_s2, seq)
        y_h = _hgrn(proj, proj_f32, hgrn_par[layer], tri, lvl, batch, seq)
        cmp_kv = _compress(
            proj,
            jnp.stack([cmp_w1_k[layer], cmp_w1_v[layer]]).astype(BF16),
            jnp.stack([cmp_b1_k[layer], cmp_b1_v[layer]])[:, None, :],
            jnp.stack([cmp_w2_k[layer], cmp_w2_v[layer]]).astype(BF16),
            jnp.stack([cmp_pos_k[layer], cmp_pos_v[layer]]),
            cmp_tabs, batch, seq)
        part, selb = _cmpwin(proj, cmp_kv, ov, batch, seq)
        y_n = _select(proj, selb, block_hot, part, batch, seq)
        x2 = _outproj(y_h, y_n, w_out[layer].astype(BF16), x2, post_norm[layer][None, :])
    return x2.reshape(batch, seq, D_MODEL)
```

```python
import functools

import numpy as np
import jax
import jax.numpy as jnp
from jax import lax
from jax.experimental import pallas as pl
from jax.experimental.pallas import tpu as pltpu

F32 = jnp.float32
BF16 = jnp.bfloat16

D_MODEL = 1024
HEAD_DIM = 128
HGRN_HEADS = 8
NSA_HEADS = 8
NSA_KV_GROUPS = 2
HEADS_PER_GROUP = NSA_HEADS // NSA_KV_GROUPS
N_BRANCHES = 3
HGRN_CHUNK = 64
CMP_BLOCK = 32
CMP_STRIDE = 16
SEL_BLOCK = 64
N_SELECT = 16
WINDOW = 512
ROPE_THETA = 500000.0
ROT_DIM = HEAD_DIM // 4
NORM_EPS = 1e-6
TAKEN = -4.0
NEG_BIG = -1e30
LOG2_E = 1.4426950408889634

LANES = 128
SUBLANES = 8
BLK_HQ, BLK_HI, BLK_HZ = 0, 8, 16
BLK_NQ, BLK_KS, BLK_KW, BLK_KC, BLK_VC, BLK_VS, BLK_VW = 24, 32, 34, 36, 38, 40, 42
BLK_NZ, BLK_NG, BLK_HF = 44, 52, 56
PROJ_BLOCKS = 64
PROJ_COLS = PROJ_BLOCKS * LANES
PROJ_BF16_COLS = BLK_HF * LANES
_SRC = dict(hq=0, hf=1024, hi=2048, hz=3072, nq=4096, kc=5120, vc=5376, ks=5632, vs=5888,
            kw=6144, vw=6400, ng=6656, nz=6680)

PROJ_TM = 1024
PROJ_TN = 1024
ROPE_FULL_TILE = BLK_NQ * LANES // PROJ_TN
ROPE_HEAD_TILE = BLK_KS * LANES // PROJ_TN
ROPE_HEAD_BLOCKS = 4
F32_TILE = BLK_HF * LANES // PROJ_TN
HGRN_T = 512
HGRN_HEADS_PER_STEP = 4
ATT_TQ = 256
CMP_KEY_CHUNK = 256
SEL_TQ = 512
SEL_TK = 1024
OUT_TM = 512
VMEM_LIMIT = 56 * 1024 * 1024


def _nt_dot(a, b):
    return lax.dot_general(a, b, (((1,), (1,)), ((), ())), preferred_element_type=F32)


def _sigmoid(x):
    return 1.0 / (1.0 + jnp.exp(-x))


def _proj_kernel(x_ref, g_ref, w_ref, c_ref, s1_ref, s2_ref, o_ref, o32_ref, h_ref, acc_ref):
    j = pl.program_id(1)

    @pl.when(j == 0)
    def _():
        x = x_ref[...]
        ms = jnp.mean(x * x, axis=-1, keepdims=True)
        h_ref[...] = (x * lax.rsqrt(ms + NORM_EPS) * g_ref[...]).astype(BF16)

    @pl.when(j == F32_TILE)
    def _():
        o32_ref[...] = jnp.dot(h_ref[...], w_ref[...], preferred_element_type=F32)

    def store(n_rope_blocks):
        acc_ref[...] = jnp.dot(h_ref[...], w_ref[...], preferred_element_type=F32)
        for blk in range(n_rope_blocks):
            sl = slice(blk * LANES, (blk + 1) * LANES)
            a = acc_ref[:, sl]
            o_ref[:, sl] = (a * c_ref[...] + pltpu.roll(a, LANES - ROT_DIM // 2, 1) * s1_ref[...]
                            + pltpu.roll(a, ROT_DIM // 2, 1) * s2_ref[...]).astype(BF16)
        if n_rope_blocks * LANES < PROJ_TN:
            rest = slice(n_rope_blocks * LANES, PROJ_TN)
            o_ref[:, rest] = acc_ref[:, rest].astype(BF16)

    @pl.when(j == ROPE_FULL_TILE)
    def _():
        store(PROJ_TN // LANES)

    @pl.when(j == ROPE_HEAD_TILE)
    def _():
        store(ROPE_HEAD_BLOCKS)

    @pl.when((j != ROPE_FULL_TILE) & (j != ROPE_HEAD_TILE) & (j != F32_TILE))
    def _():
        o_ref[...] = jnp.dot(h_ref[...], w_ref[...], preferred_element_type=F32).astype(BF16)


def _proj(x2, g, w, rope_c, rope_s1, rope_s2, seq):
    n = x2.shape[0]
    tm = min(PROJ_TM, seq)
    pos_tiles = seq // tm
    return pl.pallas_call(
        _proj_kernel,
        out_shape=(jax.ShapeDtypeStruct((n, PROJ_BF16_COLS), BF16),
                   jax.ShapeDtypeStruct((n, PROJ_COLS - PROJ_BF16_COLS), F32)),
        grid=(n // tm, PROJ_COLS // PROJ_TN),
        in_specs=[
            pl.BlockSpec((tm, D_MODEL), lambda i, j: (i, 0)),
            pl.BlockSpec((1, D_MODEL), lambda i, j: (0, 0)),
            pl.BlockSpec((D_MODEL, PROJ_TN), lambda i, j: (0, j)),
            pl.BlockSpec((tm, LANES), lambda i, j: (i % pos_tiles, 0)),
            pl.BlockSpec((tm, LANES), lambda i, j: (i % pos_tiles, 0)),
            pl.BlockSpec((tm, LANES), lambda i, j: (i % pos_tiles, 0)),
        ],
        out_specs=(pl.BlockSpec((tm, PROJ_TN), lambda i, j: (i, jnp.minimum(j, F32_TILE - 1))),
                   pl.BlockSpec((tm, PROJ_TN), lambda i, j: (i, 0))),
        scratch_shapes=[pltpu.VMEM((tm, D_MODEL), BF16), pltpu.VMEM((tm, PROJ_TN), F32)],
        compiler_params=pltpu.CompilerParams(
            dimension_semantics=("arbitrary", "arbitrary"), vmem_limit_bytes=VMEM_LIMIT),
        name="proj",
    )(x2, g, w, rope_c, rope_s1, rope_s2)


def _hgrn_kernel(q_ref, f_ref, i_ref, z_ref, par_ref, tri_ref, lvl_ref, y_ref, st_ref, o_ref):
    @pl.when(pl.program_id(2) == 0)
    def _():
        st_ref[...] = jnp.zeros_like(st_ref)

    t = q_ref.shape[0]
    n_heads = q_ref.shape[1] // HEAD_DIM
    nc = t // HGRN_CHUNK
    units = n_heads * nc
    shape3 = (units, HGRN_CHUNK, HEAD_DIM)

    def stacked(ref):
        return jnp.concatenate([ref[:, h * HEAD_DIM:(h + 1) * HEAD_DIM] for h in range(n_heads)], axis=0)

    lf_heads, k_heads = [], []
    for h in range(n_heads):
        log_lb, log_1m_lb, one_m_lb = par_ref[h, 0:1, :], par_ref[h, 1:2, :], par_ref[h, 2:3, :]
        x = f_ref[:, h * HEAD_DIM:(h + 1) * HEAD_DIM]
        e = jnp.exp(-jnp.abs(x))
        c = log_1m_lb + jnp.minimum(x, 0.0) - jnp.log(1.0 + e)
        lf_heads.append(jnp.maximum(log_lb, c) + jnp.log(1.0 + jnp.exp(-jnp.abs(log_lb - c))))
        k_heads.append(one_m_lb * jnp.where(x >= 0.0, e, 1.0) / (1.0 + e))
    k3 = jnp.concatenate(k_heads, axis=0).reshape(shape3)
    q3 = stacked(q_ref).astype(F32).reshape(shape3)
    v3 = stacked(i_ref).astype(F32).reshape(shape3)
    v_bf = v3.astype(BF16)

    lf3 = (jnp.concatenate(lf_heads, axis=0) * LOG2_E).reshape(shape3)
    b3 = jnp.stack([jnp.dot(tri_ref[...], lf3[u], precision=lax.Precision.HIGHEST,
                            preferred_element_type=F32) for u in range(units)])

    rows = units * HGRN_CHUNK
    tiles = (rows // SUBLANES, SUBLANES, HEAD_DIM)
    b_tile = b3.reshape(tiles)
    tile_row = lax.broadcasted_iota(jnp.int32, tiles, 1)
    lvl = lvl_ref[...]
    attn = jnp.where(lvl == 0, jnp.stack([_nt_dot(q3[u].astype(BF16), k3[u].astype(BF16)) for u in range(units)]), 0.0)
    n = HGRN_CHUNK // 2
    while n >= 1:
        if 2 * n >= SUBLANES:
            blocks = (rows // (2 * n), 2 * n, HEAD_DIM)
            ref = jnp.broadcast_to(b3.reshape(blocks)[:, n - 1:n, :], blocks).reshape(shape3)
            side = jnp.concatenate([k3.reshape(blocks)[:, :n], q3.reshape(blocks)[:, n:]], axis=1).reshape(shape3)
        else:
            ref = jnp.broadcast_to(b_tile[:, n - 1:n, :], tiles)
            for lo in range(2 * n, SUBLANES, 2 * n):
                ref = jnp.where(tile_row >= lo, jnp.broadcast_to(b_tile[:, lo + n - 1:lo + n, :], tiles), ref)
            ref = ref.reshape(shape3)
            side = jnp.where((tile_row & n) != 0, q3.reshape(tiles), k3.reshape(tiles)).reshape(shape3)
        d = b3 - ref
        r_n = (side * jnp.exp2(jnp.minimum(d, -d))).astype(BF16)
        a_n = jnp.stack([_nt_dot(r_n[u], r_n[u]) for u in range(units)])
        attn = jnp.where(lvl == n, a_n, attn)
        n //= 2
    attn_bf = attn.astype(BF16)
    o_intra = jnp.stack([jnp.dot(attn_bf[u], v_bf[u], preferred_element_type=F32) for u in range(units)])

    b_last = b3[:, HGRN_CHUNK - 1:HGRN_CHUNK, :]
    decay = jnp.exp2(b_last)
    q_dec = (q3 * jnp.exp2(b3)).astype(BF16)
    k_dec = (k3 * jnp.exp2(b_last - b3)).astype(BF16)
    upd = [lax.dot_general(v_bf[u], k_dec[u], (((0,), (0,)), ((), ())), preferred_element_type=F32)
           for u in range(units)]
    st = [st_ref[h] for h in range(n_heads)]
    for ci in range(nc):
        for h in range(n_heads):
            u = h * nc + ci
            o_ref[h, ci * HGRN_CHUNK:(ci + 1) * HGRN_CHUNK, :] = (
                o_intra[u] + _nt_dot(q_dec[u], st[h].astype(BF16)))
            st[h] = st[h] * decay[u] + upd[u]
    for h in range(n_heads):
        st_ref[h] = st[h]
        cols = slice(h * HEAD_DIM, (h + 1) * HEAD_DIM)
        o = o_ref[h]
        ms = jnp.mean(o * o, axis=-1, keepdims=True)
        z = z_ref[:, cols].astype(F32)
        y_ref[:, cols] = (o * lax.rsqrt(ms + NORM_EPS) * par_ref[h, 3:4, :] * (z * _sigmoid(z))).astype(BF16)


def _hgrn(proj, proj_f32, par, tri, lvl, batch, seq):
    t = min(HGRN_T, seq)
    nt = seq // t

    hp = HGRN_HEADS_PER_STEP
    width = hp * HEAD_DIM

    def col(base):
        return pl.BlockSpec((t, width), lambda b, h, ti: (b * nt + ti, base // hp + h))

    return pl.pallas_call(
        _hgrn_kernel,
        out_shape=jax.ShapeDtypeStruct((batch * seq, HGRN_HEADS * HEAD_DIM), BF16),
        grid=(batch, HGRN_HEADS // hp, nt),
        in_specs=[
            col(BLK_HQ), col(0), col(BLK_HI), col(BLK_HZ),
            pl.BlockSpec((hp, 8, HEAD_DIM), lambda b, h, ti: (h, 0, 0)),
            pl.BlockSpec((HGRN_CHUNK, HGRN_CHUNK), lambda b, h, ti: (0, 0)),
            pl.BlockSpec((HGRN_CHUNK, HGRN_CHUNK), lambda b, h, ti: (0, 0)),
        ],
        out_specs=pl.BlockSpec((t, width), lambda b, h, ti: (b * nt + ti, h)),
        scratch_shapes=[pltpu.VMEM((hp, HEAD_DIM, HEAD_DIM), F32), pltpu.VMEM((hp, t, HEAD_DIM), F32)],
        compiler_params=pltpu.CompilerParams(
            dimension_semantics=("arbitrary", "arbitrary", "arbitrary"), vmem_limit_bytes=VMEM_LIMIT),
        name="hgrn",
    )(proj, proj_f32, proj, proj, par, tri, lvl)


def _compress_kernel(x_ref, w1_ref, b1_ref, w2_ref, pos_ref, tab_ref, o_ref, x32_ref):
    n_rows = o_ref.shape[-2]
    x32_ref[...] = x_ref[...].astype(F32)
    acc_lo = jnp.zeros((n_rows, HEAD_DIM), F32)
    acc_hi = jnp.zeros((n_rows, HEAD_DIM), F32)
    for r in range(CMP_STRIDE):
        xr = x32_ref[pl.ds(r, n_rows, stride=CMP_STRIDE), :]
        lo, hi = r, CMP_STRIDE + r
        acc_lo = acc_lo + jnp.dot((xr + pos_ref[0, lo:lo + 1, :]).astype(BF16),
                                  w1_ref[0, lo * HEAD_DIM:(lo + 1) * HEAD_DIM, :],
                                  preferred_element_type=F32)
        acc_hi = acc_hi + jnp.dot((xr + pos_ref[0, hi:hi + 1, :]).astype(BF16),
                                  w1_ref[0, hi * HEAD_DIM:(hi + 1) * HEAD_DIM, :],
                                  preferred_element_type=F32)
    pre = acc_lo + pltpu.roll(acc_hi, n_rows - 1, 0) + b1_ref[0]
    mid = pre * _sigmoid(pre)
    out = jnp.dot(mid.astype(BF16), w2_ref[0], preferred_element_type=F32)
    o_ref[0, 0, 0] = (out * tab_ref[0, 0] + pltpu.roll(out, LANES - ROT_DIM // 2, 1) * tab_ref[0, 1]
                      + pltpu.roll(out, ROT_DIM // 2, 1) * tab_ref[0, 2]).astype(BF16)


def _compress(proj, w1, b1, w2, pos, tabs, batch, seq):
    n_rows = seq // CMP_STRIDE
    return pl.pallas_call(
        _compress_kernel,
        out_shape=jax.ShapeDtypeStruct((2, batch, NSA_KV_GROUPS, n_rows, HEAD_DIM), BF16),
        grid=(2, batch, NSA_KV_GROUPS),
        in_specs=[
            pl.BlockSpec((seq, HEAD_DIM), lambda kv, b, g: (b, BLK_KC + 2 * kv + g)),
            pl.BlockSpec((1, CMP_BLOCK * HEAD_DIM, HEAD_DIM), lambda kv, b, g: (kv, 0, 0)),
            pl.BlockSpec((1, 1, HEAD_DIM), lambda kv, b, g: (kv, 0, 0)),
            pl.BlockSpec((1, HEAD_DIM, HEAD_DIM), lambda kv, b, g: (kv, 0, 0)),
            pl.BlockSpec((1, CMP_BLOCK, HEAD_DIM), lambda kv, b, g: (kv, 0, 0)),
            pl.BlockSpec((1, 3, n_rows, HEAD_DIM), lambda kv, b, g: (kv, 0, 0, 0)),
        ],
        out_specs=pl.BlockSpec((1, 1, 1, n_rows, HEAD_DIM), lambda kv, b, g: (kv, b, g, 0, 0)),
        scratch_shapes=[pltpu.VMEM((seq, HEAD_DIM), F32)],
        compiler_params=pltpu.CompilerParams(
            dimension_semantics=("arbitrary", "arbitrary", "arbitrary"), vmem_limit_bytes=VMEM_LIMIT),
        name="compress",
    )(proj, w1, b1, w2, pos, tabs)


def _cmpwin_kernel(q_ref, kc_ref, vc_ref, kw_ref, vw_ref, ng_ref, ov_ref, part_ref, selb_ref,
                   imp_ref, val_ref, *, n_blocks, win_len, key_chunk):
    tq = q_ref.shape[0]
    n_cmp = kc_ref.shape[-2]
    nblkp = selb_ref.shape[2] * LANES
    q0 = pl.program_id(2) * tq
    gates = _sigmoid(ng_ref[...].astype(F32))
    q_heads = [q_ref[:, h * HEAD_DIM:(h + 1) * HEAD_DIM] for h in range(HEADS_PER_GROUP)]

    def attend(s, values_ones, vis, all_masked_possible):
        s = jnp.where(vis, s, NEG_BIG)
        m = jnp.max(s, axis=-1, keepdims=True)
        if all_masked_possible:
            m = jnp.where(m < 0.5 * NEG_BIG, 0.0, m)
        p = jnp.exp2(s - m)
        o = jnp.dot(p.astype(BF16), values_ones, preferred_element_type=F32)
        inv = 1.0 / jnp.maximum(o[:, HEAD_DIM:], 1e-30)
        return o[:, :HEAD_DIM] * inv, p, inv

    w0 = pl.multiple_of(jnp.maximum(q0 + tq - win_len, 0), tq)
    kwin = kw_ref[pl.ds(w0, win_len), :]
    vwin = jnp.concatenate([vw_ref[pl.ds(w0, win_len), :], jnp.ones((win_len, HEAD_DIM), BF16)], axis=1)
    t_w = lax.broadcasted_iota(jnp.int32, (tq, win_len), 0) + q0
    kp_w = lax.broadcasted_iota(jnp.int32, (tq, win_len), 1) + w0
    vis_w = lax.bitcast_convert_type(t_w - kp_w, jnp.uint32) < WINDOW
    s_w = [_nt_dot(qh, kwin) for qh in q_heads]
    for h in range(HEADS_PER_GROUP):
        o_w, _, _ = attend(s_w[h], vwin, vis_w, False)
        g_w = gates[:, 2 * HEADS_PER_GROUP + h:2 * HEADS_PER_GROUP + h + 1]
        part_ref[:, h * HEAD_DIM:(h + 1) * HEAD_DIM] = g_w * o_w

    def compressed(n_keys):
        kc = kc_ref[0, 0, 0, :n_keys, :]
        vc = jnp.concatenate([vc_ref[0, 0, 0, :n_keys, :], jnp.ones((n_keys, HEAD_DIM), BF16)], axis=1)
        t_c = lax.broadcasted_iota(jnp.int32, (tq, n_keys), 0) + q0
        n_c = lax.broadcasted_iota(jnp.int32, (tq, n_keys), 1)
        vis_c = n_c * CMP_STRIDE + (CMP_BLOCK - 1) <= t_c
        p_sum = jnp.zeros((tq, n_keys), F32)
        s_c = [_nt_dot(qh, kc) for qh in q_heads]
        for h in range(HEADS_PER_GROUP):
            sl = slice(h * HEAD_DIM, (h + 1) * HEAD_DIM)
            o_c, p, inv = attend(s_c[h], vc, vis_c, True)
            p_sum = p_sum + p * jnp.concatenate([inv] * (n_keys // LANES), axis=1)
            g_c = gates[:, 0 * HEADS_PER_GROUP + h:0 * HEADS_PER_GROUP + h + 1]
            part_ref[:, sl] = part_ref[:, sl] + g_c * o_c
        p_hi = p_sum.astype(BF16)
        p_lo = (p_sum - p_hi.astype(F32)).astype(BF16)
        imp_ref[...] = (jnp.dot(p_hi, ov_ref[:n_keys, :], preferred_element_type=F32)
                        + jnp.dot(p_lo, ov_ref[:n_keys, :], preferred_element_type=F32))

    n_visible = (q0 + tq - CMP_BLOCK) // CMP_STRIDE + 1
    n_chunks = jnp.minimum((n_visible + key_chunk - 1) // key_chunk, n_cmp // key_chunk)
    for c in range(1, n_cmp // key_chunk + 1):
        @pl.when(n_chunks == c)
        def _():
            compressed(c * key_chunk)

    def select_blocks(n_rows):
        imp_t = imp_ref[:, :n_rows].T
        t_b = lax.broadcasted_iota(jnp.int32, (n_rows, tq), 1) + q0
        blk = lax.broadcasted_iota(jnp.int32, (n_rows, tq), 0)
        cur = lax.shift_right_logical(t_b, 6)
        near = lax.bitcast_convert_type(cur - blk, jnp.uint32) < 2
        val = jnp.where(near, TAKEN, jnp.where(blk * SEL_BLOCK <= t_b, imp_t, -1.0))
        val = jnp.where(blk == 0, TAKEN, val)
        val_ref[:n_rows] = jnp.where(blk < n_blocks, val, -3.0)
        blk_f = blk.astype(F32)
        n_top = min(N_SELECT, n_blocks)
        quota = n_top - 1 - jnp.minimum(cur[0:1, :], 2)

        def pick(it, carry):
            v = val_ref[:n_rows]
            m = jnp.max(v, axis=0, keepdims=True)
            idx = jnp.min(jnp.where(v == m, blk_f, float(nblkp)), axis=0, keepdims=True)
            idx = jnp.where(it < quota, idx, -1.0)
            val_ref[:n_rows] = jnp.where(blk_f == idx, TAKEN, v)
            return carry

        lax.fori_loop(0, n_top - 1 - jnp.minimum(lax.shift_right_logical(q0, 6), 2), pick, 0)
        bias = jnp.where(val_ref[:n_rows].T == TAKEN, 0.0, NEG_BIG).astype(BF16)
        for part in range(nblkp // LANES):
            if part * LANES < n_rows:
                selb_ref[0, 0, part] = bias[:, part * LANES:(part + 1) * LANES]
            else:
                selb_ref[0, 0, part] = jnp.full((tq, LANES), NEG_BIG, BF16)

    if nblkp > LANES:
        @pl.when(q0 + tq <= LANES * SEL_BLOCK)
        def _():
            select_blocks(LANES)

        @pl.when(q0 + tq > LANES * SEL_BLOCK)
        def _():
            select_blocks(nblkp)
    else:
        select_blocks(nblkp)


def _cmpwin(proj, cmp_kv, ov, batch, seq):
    tq = min(ATT_TQ, seq)
    nq = seq // tq
    n_cmp = seq // CMP_STRIDE
    n_blocks = seq // SEL_BLOCK
    nblkp = ov.shape[1]
    win_len = min(WINDOW + tq, seq)
    q_cols = HEADS_PER_GROUP * HEAD_DIM
    key_chunk = min(CMP_KEY_CHUNK, n_cmp)
    kernel = functools.partial(_cmpwin_kernel, n_blocks=n_blocks, win_len=win_len, key_chunk=key_chunk)
    return pl.pallas_call(
        kernel,
        out_shape=(jax.ShapeDtypeStruct((batch * seq, NSA_HEADS * HEAD_DIM), F32),
                   jax.ShapeDtypeStruct((batch, NSA_KV_GROUPS, nblkp // LANES, seq, LANES), BF16)),
        grid=(batch, NSA_KV_GROUPS, nq),
        in_specs=[
            pl.BlockSpec((tq, q_cols), lambda b, g, i: (b * nq + i, BLK_NQ * LANES // q_cols + g)),
            pl.BlockSpec((1, 1, 1, n_cmp, HEAD_DIM), lambda b, g, i: (0, b, g, 0, 0)),
            pl.BlockSpec((1, 1, 1, n_cmp, HEAD_DIM), lambda b, g, i: (1, b, g, 0, 0)),
            pl.BlockSpec((seq, HEAD_DIM), lambda b, g, i: (b, BLK_KW + g)),
            pl.BlockSpec((seq, HEAD_DIM), lambda b, g, i: (b, BLK_VW + g)),
            pl.BlockSpec((tq, LANES), lambda b, g, i: (b * nq + i, BLK_NG + g)),
            pl.BlockSpec((n_cmp, nblkp), lambda b, g, i: (0, 0)),
        ],
        out_specs=(pl.BlockSpec((tq, q_cols), lambda b, g, i: (b * nq + i, g)),
                   pl.BlockSpec((1, 1, nblkp // LANES, tq, LANES), lambda b, g, i: (b, g, 0, i, 0))),
        scratch_shapes=[pltpu.VMEM((tq, nblkp), F32), pltpu.VMEM((nblkp, tq), F32)],
        compiler_params=pltpu.CompilerParams(
            dimension_semantics=("arbitrary", "arbitrary", "arbitrary"), vmem_limit_bytes=VMEM_LIMIT),
        name="cmpwin",
    )(proj, cmp_kv, cmp_kv, proj, proj, proj, ov)


def _select_kernel(q_ref, selb_ref, ks_ref, vs_ref, hot_ref, part_ref, ng_ref, nz_ref, y_ref,
                   acc_ref, m_ref, p_ref, alpha_ref, *, tk):
    tq = q_ref.shape[0]
    q0 = pl.program_id(2) * tq

    def query_rows(h, slab):
        return jnp.concatenate([q_ref[:, h * HEAD_DIM:(h + 1) * HEAD_DIM], selb_ref[0, 0, slab]], axis=1)

    acc_ref[...] = jnp.zeros_like(acc_ref)
    m_ref[...] = jnp.full_like(m_ref, NEG_BIG)

    def tile_step(kt, pending, causal):
        if kt is not None:
            k0 = pl.multiple_of(kt * tk, tk)
            hot0 = pl.multiple_of(k0 % hot_ref.shape[0], tk)
            k_t = jnp.concatenate([ks_ref[pl.ds(k0, tk), :], hot_ref[pl.ds(hot0, tk), :]], axis=1)
            slab = k0 // (SEL_BLOCK * LANES)
            if causal:
                t_pos = lax.broadcasted_iota(jnp.int32, (tq, tk), 0) + q0
                k_pos = lax.broadcasted_iota(jnp.int32, (tq, tk), 1) + k0
                vis = k_pos <= t_pos
        if pending is not None:
            v_t = jnp.concatenate([vs_ref[pl.ds(pl.multiple_of(pending * tk, tk), tk), :],
                                   jnp.ones((tk, HEAD_DIM), BF16)], axis=1)
        if kt is not None:
            s_next = _nt_dot(query_rows(0, slab), k_t)
        for h in range(HEADS_PER_GROUP):
            if kt is not None:
                s = s_next
                if h + 1 < HEADS_PER_GROUP:
                    s_next = _nt_dot(query_rows(h + 1, slab), k_t)
            if pending is not None:
                alpha = alpha_ref[h]
                acc_ref[h] = (acc_ref[h] * jnp.concatenate([alpha, alpha], axis=1)
                              + jnp.dot(p_ref[h], v_t, preferred_element_type=F32))
            if kt is not None:
                if causal:
                    s = jnp.where(vis, s, NEG_BIG)
                m_prev = m_ref[h]
                m_new = jnp.maximum(m_prev, jnp.max(s, axis=-1, keepdims=True))
                p_ref[h] = jnp.exp2(s - jnp.concatenate([m_new] * (tk // LANES), axis=1)).astype(BF16)
                alpha_ref[h] = jnp.exp2(m_prev - m_new)
                m_ref[h] = m_new

    n_full = q0 // tk

    @pl.when(n_full > 0)
    def _():
        tile_step(0, None, False)

    @pl.when(n_full == 0)
    def _():
        p_ref[...] = jnp.zeros_like(p_ref)
        alpha_ref[...] = jnp.ones_like(alpha_ref)

    def step(kt, carry):
        tile_step(kt, kt - 1, False)
        return carry

    lax.fori_loop(1, n_full, step, 0)
    tile_step(n_full, jnp.maximum(n_full - 1, 0), True)
    tile_step(None, n_full, False)

    gates = _sigmoid(ng_ref[...].astype(F32))
    for h in range(HEADS_PER_GROUP):
        sl = slice(h * HEAD_DIM, (h + 1) * HEAD_DIM)
        acc = acc_ref[h]
        o_s = acc[:, :HEAD_DIM] / jnp.maximum(acc[:, HEAD_DIM:], 1e-30)
        g_s = gates[:, 1 * HEADS_PER_GROUP + h:1 * HEADS_PER_GROUP + h + 1]
        z = nz_ref[:, sl].astype(F32)
        y_ref[:, sl] = ((part_ref[:, sl] + g_s * o_s) * (z * _sigmoid(z))).astype(BF16)


def _select(proj, selb, block_hot, part, batch, seq):
    tq = min(SEL_TQ, seq)
    tk = min(SEL_TK, seq)
    nq = seq // tq
    n_slabs = selb.shape[2]
    q_cols = HEADS_PER_GROUP * HEAD_DIM
    kernel = functools.partial(_select_kernel, tk=tk)
    return pl.pallas_call(
        kernel,
        out_shape=jax.ShapeDtypeStruct((batch * seq, NSA_HEADS * HEAD_DIM), BF16),
        grid=(batch, NSA_KV_GROUPS, nq),
        in_specs=[
            pl.BlockSpec((tq, q_cols), lambda b, g, i: (b * nq + i, BLK_NQ * LANES // q_cols + g)),
            pl.BlockSpec((1, 1, n_slabs, tq, LANES), lambda b, g, i: (b, g, 0, i, 0)),
            pl.BlockSpec((seq, HEAD_DIM), lambda b, g, i: (b, BLK_KS + g), pipeline_mode=pl.Buffered(1)),
            pl.BlockSpec((seq, HEAD_DIM), lambda b, g, i: (b, BLK_VS + g), pipeline_mode=pl.Buffered(1)),
            pl.BlockSpec(block_hot.shape, lambda b, g, i: (0, 0), pipeline_mode=pl.Buffered(1)),
            pl.BlockSpec((tq, q_cols), lambda b, g, i: (b * nq + i, g)),
            pl.BlockSpec((tq, LANES), lambda b, g, i: (b * nq + i, BLK_NG + g)),
            pl.BlockSpec((tq, q_cols), lambda b, g, i: (b * nq + i, BLK_NZ * LANES // q_cols + g)),
        ],
        out_specs=pl.BlockSpec((tq, q_cols), lambda b, g, i: (b * nq + i, g)),
        scratch_shapes=[
            pltpu.VMEM((HEADS_PER_GROUP, tq, 2 * HEAD_DIM), F32),
            pltpu.VMEM((HEADS_PER_GROUP, tq, LANES), F32),
            pltpu.VMEM((HEADS_PER_GROUP, tq, tk), BF16),
            pltpu.VMEM((HEADS_PER_GROUP, tq, LANES), F32),
        ],
        compiler_params=pltpu.CompilerParams(
            dimension_semantics=("arbitrary", "arbitrary", "arbitrary"), vmem_limit_bytes=VMEM_LIMIT),
        name="select",
    )(proj, selb, proj, proj, block_hot, part, proj, proj)


def _outproj_kernel(yh_ref, yn_ref, w_ref, x_ref, g_ref, o_ref):
    half = yh_ref.shape[1]
    z = (jnp.dot(yh_ref[...], w_ref[:half, :], preferred_element_type=F32)
         + jnp.dot(yn_ref[...], w_ref[half:, :], preferred_element_type=F32))
    ms = jnp.mean(z * z, axis=-1, keepdims=True)
    o_ref[...] = x_ref[...] + z * lax.rsqrt(ms + NORM_EPS) * g_ref[...]


def _outproj(yh, yn, w, x2, g):
    n = x2.shape[0]
    tm = min(OUT_TM, n)
    return pl.pallas_call(
        _outproj_kernel,
        out_shape=jax.ShapeDtypeStruct((n, D_MODEL), F32),
        grid=(n // tm,),
        in_specs=[
            pl.BlockSpec((tm, yh.shape[1]), lambda i: (i, 0)),
            pl.BlockSpec((tm, yn.shape[1]), lambda i: (i, 0)),
            pl.BlockSpec(w.shape, lambda i: (0, 0)),
            pl.BlockSpec((tm, D_MODEL), lambda i: (i, 0)),
            pl.BlockSpec((1, D_MODEL), lambda i: (0, 0)),
        ],
        out_specs=pl.BlockSpec((tm, D_MODEL), lambda i: (i, 0)),
        compiler_params=pltpu.CompilerParams(
            dimension_semantics=("arbitrary",), vmem_limit_bytes=VMEM_LIMIT),
        name="outproj",
    )(yh, yn, w, x2, g)


def _permute_weights(w):
    def seg(name, width):
        return w[:, _SRC[name]:_SRC[name] + width]

    ng = seg("ng", NSA_HEADS * N_BRANCHES).reshape(-1, NSA_KV_GROUPS, HEADS_PER_GROUP, N_BRANCHES)
    ng = ng.transpose(0, 1, 3, 2).reshape(-1, NSA_KV_GROUPS, N_BRANCHES * HEADS_PER_GROUP)
    ng = jnp.pad(ng, ((0, 0), (0, 0), (0, LANES - N_BRANCHES * HEADS_PER_GROUP))).reshape(-1, NSA_KV_GROUPS * LANES)
    placed = [(BLK_HQ, seg("hq", 1024)), (BLK_HI, seg("hi", 1024)), (BLK_HZ, seg("hz", 1024)),
              (BLK_NQ, seg("nq", 1024) * (HEAD_DIM ** -0.5 * LOG2_E)),
              (BLK_KS, seg("ks", 256)), (BLK_KW, seg("kw", 256)), (BLK_KC, seg("kc", 256)),
              (BLK_VC, seg("vc", 256)), (BLK_VS, seg("vs", 256)), (BLK_VW, seg("vw", 256)),
              (BLK_NZ, seg("nz", 1024)), (BLK_NG, ng), (BLK_HF, seg("hf", 1024))]
    parts, col = [], 0
    for blk, cols in sorted(placed, key=lambda p: p[0]):
        if blk * LANES > col:
            parts.append(jnp.zeros((w.shape[0], blk * LANES - col), w.dtype))
        parts.append(cols)
        col = blk * LANES + cols.shape[1]
    if col < PROJ_COLS:
        parts.append(jnp.zeros((w.shape[0], PROJ_COLS - col), w.dtype))
    return jnp.concatenate(parts, axis=1).astype(BF16)


def _rope_tables(pos):
    half = ROT_DIM // 2
    inv = ROPE_THETA ** (-2.0 * jnp.arange(half, dtype=F32) / ROT_DIM)
    ang = pos.astype(F32)[:, None] * inv[None, :]
    cos, sin = jnp.cos(ang), jnp.sin(ang)
    n = pos.shape[0]
    pad = jnp.zeros((n, HEAD_DIM - ROT_DIM), F32)
    zero = jnp.zeros((n, half), F32)
    c = jnp.concatenate([cos, cos, pad + 1.0], axis=1)
    s1 = jnp.concatenate([-sin, zero, pad], axis=1)
    s2 = jnp.concatenate([zero, sin, pad], axis=1)
    return c, s1, s2


def _pair_levels():
    t = np.arange(HGRN_CHUNK)[:, None]
    s = np.arange(HGRN_CHUNK)[None, :]
    high_bit = 2 ** np.floor(np.log2(np.maximum(t ^ s, 1))).astype(np.int64)
    return jnp.asarray(np.where(t > s, high_bit, np.where(t == s, 0, -1)), jnp.int32)


def _block_one_hot(seq):
    rows = min(seq, SEL_BLOCK * LANES)
    lane = (np.arange(rows) // SEL_BLOCK) % LANES
    return jnp.asarray(lane[:, None] == np.arange(LANES)[None, :], BF16)


def _overlap_matrix(n_cmp, n_blocks, nblkp):
    start = np.arange(n_cmp)[:, None] * CMP_STRIDE
    sel = np.arange(nblkp)[None, :] * SEL_BLOCK
    ov = (start < sel + SEL_BLOCK) & (start + CMP_BLOCK > sel)
    ov &= np.arange(nblkp)[None, :] < n_blocks
    ov &= np.arange(n_cmp)[:, None] < (n_cmp - 1)
    return jnp.asarray(ov, BF16)


def kernel(x, pre_norm, post_norm, w_in, hgrn_lb_logits, hgrn_out_norm, cmp_pos_k, cmp_w1_k, cmp_b1_k,
           cmp_w2_k, cmp_pos_v, cmp_w1_v, cmp_b1_v, cmp_w2_v, w_out):
    batch, seq, _ = x.shape
    depth = w_in.shape[0]
    n_cmp = seq // CMP_STRIDE
    n_blocks = seq // SEL_BLOCK
    nblkp = -(-n_blocks // LANES) * LANES

    lb_probs = jax.nn.softmax(hgrn_lb_logits.astype(F32), axis=0)
    lower = jnp.maximum(jnp.cumsum(lb_probs, axis=0) - lb_probs[0:1], 0.0)
    lower = lower.reshape(depth, HGRN_HEADS, 1, HEAD_DIM)
    g_out = jnp.broadcast_to(hgrn_out_norm.astype(F32)[:, None, None, :], lower.shape)
    hgrn_par = jnp.concatenate(
        [jnp.log(lower), jnp.log1p(-lower), 1.0 - lower, g_out, jnp.zeros((depth, HGRN_HEADS, 4, HEAD_DIM), F32)],
        axis=2)

    rope_c, rope_s1, rope_s2 = _rope_tables(jnp.arange(seq))
    cmp_tab_k = jnp.stack(_rope_tables(jnp.arange(n_cmp) * CMP_STRIDE + CMP_BLOCK - 1))
    cmp_tab_v = jnp.stack([jnp.ones((n_cmp, HEAD_DIM), F32), jnp.zeros((n_cmp, HEAD_DIM), F32),
                           jnp.zeros((n_cmp, HEAD_DIM), F32)])
    cmp_tabs = jnp.stack([cmp_tab_k, cmp_tab_v])
    ov = _overlap_matrix(n_cmp, n_blocks, nblkp)
    tri = jnp.asarray(np.tril(np.ones((HGRN_CHUNK, HGRN_CHUNK), np.float32)))
    lvl = _pair_levels()
    block_hot = _block_one_hot(seq)

    x2 = x.reshape(batch * seq, D_MODEL)
    for layer in range(depth):
        proj, proj_f32 = _proj(x2, pre_norm[layer][None, :], _permute_weights(w_in[layer]),
                               rope_c, rope_s1, rope_s2, seq)
        y_h = _hgrn(proj, proj_f32, hgrn_par[layer], tri, lvl, batch, seq)
        cmp_kv = _compress(
            proj,
            jnp.stack([cmp_w1_k[layer], cmp_w1_v[layer]]).astype(BF16),
            jnp.stack([cmp_b1_k[layer], cmp_b1_v[layer]])[:, None, :],
            jnp.stack([cmp_w2_k[layer], cmp_w2_v[layer]]).astype(BF16),
            jnp.stack([cmp_pos_k[layer], cmp_pos_v[layer]]),
            cmp_tabs, batch, seq)
        part, selb = _cmpwin(proj, cmp_kv, ov, batch, seq)
        y_n = _select(proj, selb, block_hot, part, batch, seq)
        x2 = _outproj(y_h, y_n, w_out[layer].astype(BF16), x2, post_norm[layer][None, :])
    return x2.reshape(batch, seq, D_MODEL)
```

```python
import functools

import numpy as np
import jax
import jax.numpy as jnp
from jax import lax
from jax.experimental import pallas as pl
from jax.experimental.pallas import tpu as pltpu

F32 = jnp.float32
BF16 = jnp.bfloat16

D_MODEL = 1024
HEAD_DIM = 128
HGRN_HEADS = 8
NSA_HEADS = 8
NSA_KV_GROUPS = 2
HEADS_PER_GROUP = NSA_HEADS // NSA_KV_GROUPS
N_BRANCHES = 3
HGRN_CHUNK = 64
CMP_BLOCK = 32
CMP_STRIDE = 16
SEL_BLOCK = 64
N_SELECT = 16
WINDOW = 512
ROPE_THETA = 500000.0
ROT_DIM = HEAD_DIM // 4
NORM_EPS = 1e-6
TAKEN = -4.0
NEG_BIG = -1e30
LOG2_E = 1.4426950408889634

LANES = 128
SUBLANES = 8
BLK_HQ, BLK_HI, BLK_HZ = 0, 8, 16
BLK_NQ, BLK_KS, BLK_KW, BLK_KC, BLK_VC, BLK_VS, BLK_VW = 24, 32, 34, 36, 38, 40, 42
BLK_NZ, BLK_NG, BLK_HF = 44, 52, 56
PROJ_BLOCKS = 64
PROJ_COLS = PROJ_BLOCKS * LANES
PROJ_BF16_COLS = BLK_HF * LANES
_SRC = dict(hq=0, hf=1024, hi=2048, hz=3072, nq=4096, kc=5120, vc=5376, ks=5632, vs=5888,
            kw=6144, vw=6400, ng=6656, nz=6680)

PROJ_TM = 1024
PROJ_TN = 1024
ROPE_FULL_TILE = BLK_NQ * LANES // PROJ_TN
ROPE_HEAD_TILE = BLK_KS * LANES // PROJ_TN
ROPE_HEAD_BLOCKS = 4
F32_TILE = BLK_HF * LANES // PROJ_TN
HGRN_T = 1024
HGRN_HEADS_PER_STEP = 4
ATT_TQ = 256
CMP_KEY_CHUNK = 256
SEL_TQ = 512
SEL_TK = 1024
OUT_TM = 1024
VMEM_LIMIT = 56 * 1024 * 1024


def _nt_dot(a, b):
    return lax.dot_general(a, b, (((1,), (1,)), ((), ())), preferred_element_type=F32)


def _sigmoid(x):
    return 1.0 / (1.0 + jnp.exp(-x))


def _proj_kernel(x_ref, g_ref, w_ref, c_ref, s1_ref, s2_ref, o_ref, o32_ref, h_ref, acc_ref):
    j = pl.program_id(1)

    @pl.when(j == 0)
    def _():
        x = x_ref[...]
        ms = jnp.mean(x * x, axis=-1, keepdims=True)
        h_ref[...] = (x * lax.rsqrt(ms + NORM_EPS) * g_ref[...]).astype(BF16)

    @pl.when(j == F32_TILE)
    def _():
        o32_ref[...] = jnp.dot(h_ref[...], w_ref[...], preferred_element_type=F32)

    def store(n_rope_blocks):
        acc_ref[...] = jnp.dot(h_ref[...], w_ref[...], preferred_element_type=F32)
        for blk in range(n_rope_blocks):
            sl = slice(blk * LANES, (blk + 1) * LANES)
            a = acc_ref[:, sl]
            o_ref[:, sl] = (a * c_ref[...] + pltpu.roll(a, LANES - ROT_DIM // 2, 1) * s1_ref[...]
                            + pltpu.roll(a, ROT_DIM // 2, 1) * s2_ref[...]).astype(BF16)
        if n_rope_blocks * LANES < PROJ_TN:
            rest = slice(n_rope_blocks * LANES, PROJ_TN)
            o_ref[:, rest] = acc_ref[:, rest].astype(BF16)

    @pl.when(j == ROPE_FULL_TILE)
    def _():
        store(PROJ_TN // LANES)

    @pl.when(j == ROPE_HEAD_TILE)
    def _():
        store(ROPE_HEAD_BLOCKS)

    @pl.when((j != ROPE_FULL_TILE) & (j != ROPE_HEAD_TILE) & (j != F32_TILE))
    def _():
        o_ref[...] = jnp.dot(h_ref[...], w_ref[...], preferred_element_type=F32).astype(BF16)


def _proj(x2, g, w, rope_c, rope_s1, rope_s2, seq):
    n = x2.shape[0]
    tm = min(PROJ_TM, seq)
    pos_tiles = seq // tm
    return pl.pallas_call(
        _proj_kernel,
        out_shape=(jax.ShapeDtypeStruct((n, PROJ_BF16_COLS), BF16),
                   jax.ShapeDtypeStruct((n, PROJ_COLS - PROJ_BF16_COLS), F32)),
        grid=(n // tm, PROJ_COLS // PROJ_TN),
        in_specs=[
            pl.BlockSpec((tm, D_MODEL), lambda i, j: (i, 0)),
            pl.BlockSpec((1, D_MODEL), lambda i, j: (0, 0)),
            pl.BlockSpec((D_MODEL, PROJ_TN), lambda i, j: (0, j)),
            pl.BlockSpec((tm, LANES), lambda i, j: (i % pos_tiles, 0)),
            pl.BlockSpec((tm, LANES), lambda i, j: (i % pos_tiles, 0)),
            pl.BlockSpec((tm, LANES), lambda i, j: (i % pos_tiles, 0)),
        ],
        out_specs=(pl.BlockSpec((tm, PROJ_TN), lambda i, j: (i, jnp.minimum(j, F32_TILE - 1))),
                   pl.BlockSpec((tm, PROJ_TN), lambda i, j: (i, 0))),
        scratch_shapes=[pltpu.VMEM((tm, D_MODEL), BF16), pltpu.VMEM((tm, PROJ_TN), F32)],
        compiler_params=pltpu.CompilerParams(
            dimension_semantics=("arbitrary", "arbitrary"), vmem_limit_bytes=VMEM_LIMIT),
        name="proj",
    )(x2, g, w, rope_c, rope_s1, rope_s2)


def _hgrn_kernel(q_ref, f_ref, i_ref, z_ref, par_ref, tri_ref, lvl_ref, y_ref, st_ref, o_ref):
    @pl.when(pl.program_id(2) == 0)
    def _():
        st_ref[...] = jnp.zeros_like(st_ref)

    t = q_ref.shape[0]
    n_heads = q_ref.shape[1] // HEAD_DIM
    nc = t // HGRN_CHUNK
    units = n_heads * nc
    shape3 = (units, HGRN_CHUNK, HEAD_DIM)

    def stacked(ref):
        return jnp.concatenate([ref[:, h * HEAD_DIM:(h + 1) * HEAD_DIM] for h in range(n_heads)], axis=0)

    lf_heads, k_heads = [], []
    for h in range(n_heads):
        log_lb, log_1m_lb, one_m_lb = par_ref[h, 0:1, :], par_ref[h, 1:2, :], par_ref[h, 2:3, :]
        x = f_ref[:, h * HEAD_DIM:(h + 1) * HEAD_DIM]
        e = jnp.exp(-jnp.abs(x))
        c = log_1m_lb + jnp.minimum(x, 0.0) - jnp.log(1.0 + e)
        lf_heads.append(jnp.maximum(log_lb, c) + jnp.log(1.0 + jnp.exp(-jnp.abs(log_lb - c))))
        k_heads.append(one_m_lb * jnp.where(x >= 0.0, e, 1.0) / (1.0 + e))
    k3 = jnp.concatenate(k_heads, axis=0).reshape(shape3)
    q3 = stacked(q_ref).astype(F32).reshape(shape3)
    v3 = stacked(i_ref).astype(F32).reshape(shape3)
    v_bf = v3.astype(BF16)

    lf3 = (jnp.concatenate(lf_heads, axis=0) * LOG2_E).reshape(shape3)
    b3 = jnp.stack([jnp.dot(tri_ref[...], lf3[u], precision=lax.Precision.HIGHEST,
                            preferred_element_type=F32) for u in range(units)])

    rows = units * HGRN_CHUNK
    tiles = (rows // SUBLANES, SUBLANES, HEAD_DIM)
    b_tile = b3.reshape(tiles)
    tile_row = lax.broadcasted_iota(jnp.int32, tiles, 1)
    lvl = lvl_ref[...]
    attn = jnp.where(lvl == 0, jnp.stack([_nt_dot(q3[u].astype(BF16), k3[u].astype(BF16)) for u in range(units)]), 0.0)
    n = HGRN_CHUNK // 2
    while n >= 1:
        if 2 * n >= SUBLANES:
            blocks = (rows // (2 * n), 2 * n, HEAD_DIM)
            ref = jnp.broadcast_to(b3.reshape(blocks)[:, n - 1:n, :], blocks).reshape(shape3)
            side = jnp.concatenate([k3.reshape(blocks)[:, :n], q3.reshape(blocks)[:, n:]], axis=1).reshape(shape3)
        else:
            ref = jnp.broadcast_to(b_tile[:, n - 1:n, :], tiles)
            for lo in range(2 * n, SUBLANES, 2 * n):
                ref = jnp.where(tile_row >= lo, jnp.broadcast_to(b_tile[:, lo + n - 1:lo + n, :], tiles), ref)
            ref = ref.reshape(shape3)
            side = jnp.where((tile_row & n) != 0, q3.reshape(tiles), k3.reshape(tiles)).reshape(shape3)
        d = b3 - ref
        r_n = (side * jnp.exp2(jnp.minimum(d, -d))).astype(BF16)
        a_n = jnp.stack([_nt_dot(r_n[u], r_n[u]) for u in range(units)])
        attn = jnp.where(lvl == n, a_n, attn)
        n //= 2
    attn_bf = attn.astype(BF16)
    o_intra = jnp.stack([jnp.dot(attn_bf[u], v_bf[u], preferred_element_type=F32) for u in range(units)])

    b_last = b3[:, HGRN_CHUNK - 1:HGRN_CHUNK, :]
    decay = jnp.exp2(b_last)
    q_dec = (q3 * jnp.exp2(b3)).astype(BF16)
    k_dec = (k3 * jnp.exp2(b_last - b3)).astype(BF16)
    upd = [lax.dot_general(v_bf[u], k_dec[u], (((0,), (0,)), ((), ())), preferred_element_type=F32)
           for u in range(units)]
    st = [st_ref[h] for h in range(n_heads)]
    for ci in range(nc):
        for h in range(n_heads):
            u = h * nc + ci
            o_ref[h, ci * HGRN_CHUNK:(ci + 1) * HGRN_CHUNK, :] = (
                o_intra[u] + _nt_dot(q_dec[u], st[h].astype(BF16)))
            st[h] = st[h] * decay[u] + upd[u]
    for h in range(n_heads):
        st_ref[h] = st[h]
        cols = slice(h * HEAD_DIM, (h + 1) * HEAD_DIM)
        o = o_ref[h]
        ms = jnp.mean(o * o, axis=-1, keepdims=True)
        z = z_ref[:, cols].astype(F32)
        y_ref[:, cols] = (o * lax.rsqrt(ms + NORM_EPS) * par_ref[h, 3:4, :] * (z * _sigmoid(z))).astype(BF16)


def _hgrn(proj, proj_f32, par, tri, lvl, batch, seq):
    t = min(HGRN_T, seq)
    nt = seq // t

    hp = HGRN_HEADS_PER_STEP
    width = hp * HEAD_DIM

    def col(base):
        return pl.BlockSpec((t, width), lambda b, h, ti: (b * nt + ti, base // hp + h))

    return pl.pallas_call(
        _hgrn_kernel,
        out_shape=jax.ShapeDtypeStruct((batch * seq, HGRN_HEADS * HEAD_DIM), BF16),
        grid=(batch, HGRN_HEADS // hp, nt),
        in_specs=[
            col(BLK_HQ), col(0), col(BLK_HI), col(BLK_HZ),
            pl.BlockSpec((hp, 8, HEAD_DIM), lambda b, h, ti: (h, 0, 0)),
            pl.BlockSpec((HGRN_CHUNK, HGRN_CHUNK), lambda b, h, ti: (0, 0)),
            pl.BlockSpec((HGRN_CHUNK, HGRN_CHUNK), lambda b, h, ti: (0, 0)),
        ],
        out_specs=pl.BlockSpec((t, width), lambda b, h, ti: (b * nt + ti, h)),
        scratch_shapes=[pltpu.VMEM((hp, HEAD_DIM, HEAD_DIM), F32), pltpu.VMEM((hp, t, HEAD_DIM), F32)],
        compiler_params=pltpu.CompilerParams(
            dimension_semantics=("arbitrary", "arbitrary", "arbitrary"), vmem_limit_bytes=VMEM_LIMIT),
        name="hgrn",
    )(proj, proj_f32, proj, proj, par, tri, lvl)


def _compress_kernel(x_ref, w1_ref, b1_ref, w2_ref, pos_ref, tab_ref, o_ref, x32_ref):
    n_rows = o_ref.shape[-2]
    x32_ref[...] = x_ref[...].astype(F32)
    acc_lo = jnp.zeros((n_rows, HEAD_DIM), F32)
    acc_hi = jnp.zeros((n_rows, HEAD_DIM), F32)
    for r in range(CMP_STRIDE):
        xr = x32_ref[pl.ds(r, n_rows, stride=CMP_STRIDE), :]
        lo, hi = r, CMP_STRIDE + r
        acc_lo = acc_lo + jnp.dot((xr + pos_ref[0, lo:lo + 1, :]).astype(BF16),
                                  w1_ref[0, lo * HEAD_DIM:(lo + 1) * HEAD_DIM, :],
                                  preferred_element_type=F32)
        acc_hi = acc_hi + jnp.dot((xr + pos_ref[0, hi:hi + 1, :]).astype(BF16),
                                  w1_ref[0, hi * HEAD_DIM:(hi + 1) * HEAD_DIM, :],
                                  preferred_element_type=F32)
    pre = acc_lo + pltpu.roll(acc_hi, n_rows - 1, 0) + b1_ref[0]
    mid = pre * _sigmoid(pre)
    out = jnp.dot(mid.astype(BF16), w2_ref[0], preferred_element_type=F32)
    o_ref[0, 0, 0] = (out * tab_ref[0, 0] + pltpu.roll(out, LANES - ROT_DIM // 2, 1) * tab_ref[0, 1]
                      + pltpu.roll(out, ROT_DIM // 2, 1) * tab_ref[0, 2]).astype(BF16)


def _compress(proj, w1, b1, w2, pos, tabs, batch, seq):
    n_rows = seq // CMP_STRIDE
    return pl.pallas_call(
        _compress_kernel,
        out_shape=jax.ShapeDtypeStruct((2, batch, NSA_KV_GROUPS, n_rows, HEAD_DIM), BF16),
        grid=(2, batch, NSA_KV_GROUPS),
        in_specs=[
            pl.BlockSpec((seq, HEAD_DIM), lambda kv, b, g: (b, BLK_KC + 2 * kv + g)),
            pl.BlockSpec((1, CMP_BLOCK * HEAD_DIM, HEAD_DIM), lambda kv, b, g: (kv, 0, 0)),
            pl.BlockSpec((1, 1, HEAD_DIM), lambda kv, b, g: (kv, 0, 0)),
            pl.BlockSpec((1, HEAD_DIM, HEAD_DIM), lambda kv, b, g: (kv, 0, 0)),
            pl.BlockSpec((1, CMP_BLOCK, HEAD_DIM), lambda kv, b, g: (kv, 0, 0)),
            pl.BlockSpec((1, 3, n_rows, HEAD_DIM), lambda kv, b, g: (kv, 0, 0, 0)),
        ],
        out_specs=pl.BlockSpec((1, 1, 1, n_rows, HEAD_DIM), lambda kv, b, g: (kv, b, g, 0, 0)),
        scratch_shapes=[pltpu.VMEM((seq, HEAD_DIM), F32)],
        compiler_params=pltpu.CompilerParams(
            dimension_semantics=("arbitrary", "arbitrary", "arbitrary"), vmem_limit_bytes=VMEM_LIMIT),
        name="compress",
    )(proj, w1, b1, w2, pos, tabs)


def _cmpwin_kernel(q_ref, kc_ref, vc_ref, kw_ref, vw_ref, ng_ref, ov_ref, part_ref, selb_ref,
                   imp_ref, val_ref, *, n_blocks, win_len, key_chunk):
    tq = q_ref.shape[0]
    n_cmp = kc_ref.shape[-2]
    nblkp = selb_ref.shape[2] * LANES
    q0 = pl.program_id(2) * tq
    gates = _sigmoid(ng_ref[...].astype(F32))
    q_heads = [q_ref[:, h * HEAD_DIM:(h + 1) * HEAD_DIM] for h in range(HEADS_PER_GROUP)]

    def attend(s, values_ones, vis, all_masked_possible):
        s = jnp.where(vis, s, NEG_BIG)
        m = jnp.max(s, axis=-1, keepdims=True)
        if all_masked_possible:
            m = jnp.where(m < 0.5 * NEG_BIG, 0.0, m)
        p = jnp.exp2(s - m)
        o = jnp.dot(p.astype(BF16), values_ones, preferred_element_type=F32)
        inv = 1.0 / jnp.maximum(o[:, HEAD_DIM:], 1e-30)
        return o[:, :HEAD_DIM] * inv, p, inv

    w0 = pl.multiple_of(jnp.maximum(q0 + tq - win_len, 0), tq)
    kwin = kw_ref[pl.ds(w0, win_len), :]
    vwin = jnp.concatenate([vw_ref[pl.ds(w0, win_len), :], jnp.ones((win_len, HEAD_DIM), BF16)], axis=1)
    t_w = lax.broadcasted_iota(jnp.int32, (tq, win_len), 0) + q0
    kp_w = lax.broadcasted_iota(jnp.int32, (tq, win_len), 1) + w0
    vis_w = lax.bitcast_convert_type(t_w - kp_w, jnp.uint32) < WINDOW
    s_w = [_nt_dot(qh, kwin) for qh in q_heads]
    for h in range(HEADS_PER_GROUP):
        o_w, _, _ = attend(s_w[h], vwin, vis_w, False)
        g_w = gates[:, 2 * HEADS_PER_GROUP + h:2 * HEADS_PER_GROUP + h + 1]
        part_ref[:, h * HEAD_DIM:(h + 1) * HEAD_DIM] = g_w * o_w

    def compressed(n_keys):
        kc = kc_ref[0, 0, 0, :n_keys, :]
        vc = jnp.concatenate([vc_ref[0, 0, 0, :n_keys, :], jnp.ones((n_keys, HEAD_DIM), BF16)], axis=1)
        t_c = lax.broadcasted_iota(jnp.int32, (tq, n_keys), 0) + q0
        n_c = lax.broadcasted_iota(jnp.int32, (tq, n_keys), 1)
        vis_c = n_c * CMP_STRIDE + (CMP_BLOCK - 1) <= t_c
        p_sum = jnp.zeros((tq, n_keys), F32)
        s_c = [_nt_dot(qh, kc) for qh in q_heads]
        for h in range(HEADS_PER_GROUP):
            sl = slice(h * HEAD_DIM, (h + 1) * HEAD_DIM)
            o_c, p, inv = attend(s_c[h], vc, vis_c, True)
            p_sum = p_sum + p * jnp.concatenate([inv] * (n_keys // LANES), axis=1)
            g_c = gates[:, 0 * HEADS_PER_GROUP + h:0 * HEADS_PER_GROUP + h + 1]
            part_ref[:, sl] = part_ref[:, sl] + g_c * o_c
        p_hi = p_sum.astype(BF16)
        p_lo = (p_sum - p_hi.astype(F32)).astype(BF16)
        imp_ref[...] = (jnp.dot(p_hi, ov_ref[:n_keys, :], preferred_element_type=F32)
                        + jnp.dot(p_lo, ov_ref[:n_keys, :], preferred_element_type=F32))

    n_visible = (q0 + tq - CMP_BLOCK) // CMP_STRIDE + 1
    n_chunks = jnp.minimum((n_visible + key_chunk - 1) // key_chunk, n_cmp // key_chunk)
    for c in range(1, n_cmp // key_chunk + 1):
        @pl.when(n_chunks == c)
        def _():
            compressed(c * key_chunk)

    def select_blocks(n_rows):
        imp_t = imp_ref[:, :n_rows].T
        t_b = lax.broadcasted_iota(jnp.int32, (n_rows, tq), 1) + q0
        blk = lax.broadcasted_iota(jnp.int32, (n_rows, tq), 0)
        cur = lax.shift_right_logical(t_b, 6)
        near = lax.bitcast_convert_type(cur - blk, jnp.uint32) < 2
        val = jnp.where(near, TAKEN, jnp.where(blk * SEL_BLOCK <= t_b, imp_t, -1.0))
        val = jnp.where(blk == 0, TAKEN, val)
        val_ref[:n_rows] = jnp.where(blk < n_blocks, val, -3.0)
        blk_f = blk.astype(F32)
        n_top = min(N_SELECT, n_blocks)
        quota = n_top - 1 - jnp.minimum(cur[0:1, :], 2)

        def pick(it, carry):
            v = val_ref[:n_rows]
            m = jnp.max(v, axis=0, keepdims=True)
            idx = jnp.min(jnp.where(v == m, blk_f, float(nblkp)), axis=0, keepdims=True)
            idx = jnp.where(it < quota, idx, -1.0)
            val_ref[:n_rows] = jnp.where(blk_f == idx, TAKEN, v)
            return carry

        lax.fori_loop(0, n_top - 1 - jnp.minimum(lax.shift_right_logical(q0, 6), 2), pick, 0)
        bias = jnp.where(val_ref[:n_rows].T == TAKEN, 0.0, NEG_BIG).astype(BF16)
        for part in range(nblkp // LANES):
            if part * LANES < n_rows:
                selb_ref[0, 0, part] = bias[:, part * LANES:(part + 1) * LANES]
            else:
                selb_ref[0, 0, part] = jnp.full((tq, LANES), NEG_BIG, BF16)

    if nblkp > LANES:
        @pl.when(q0 + tq <= LANES * SEL_BLOCK)
        def _():
            select_blocks(LANES)

        @pl.when(q0 + tq > LANES * SEL_BLOCK)
        def _():
            select_blocks(nblkp)
    else:
        select_blocks(nblkp)


def _cmpwin(proj, cmp_kv, ov, batch, seq):
    tq = min(ATT_TQ, seq)
    nq = seq // tq
    n_cmp = seq // CMP_STRIDE
    n_blocks = seq // SEL_BLOCK
    nblkp = ov.shape[1]
    win_len = min(WINDOW + tq, seq)
    q_cols = HEADS_PER_GROUP * HEAD_DIM
    key_chunk = min(CMP_KEY_CHUNK, n_cmp)
    kernel = functools.partial(_cmpwin_kernel, n_blocks=n_blocks, win_len=win_len, key_chunk=key_chunk)
    return pl.pallas_call(
        kernel,
        out_shape=(jax.ShapeDtypeStruct((batch * seq, NSA_HEADS * HEAD_DIM), F32),
                   jax.ShapeDtypeStruct((batch, NSA_KV_GROUPS, nblkp // LANES, seq, LANES), BF16)),
        grid=(batch, NSA_KV_GROUPS, nq),
        in_specs=[
            pl.BlockSpec((tq, q_cols), lambda b, g, i: (b * nq + i, BLK_NQ * LANES // q_cols + g)),
            pl.BlockSpec((1, 1, 1, n_cmp, HEAD_DIM), lambda b, g, i: (0, b, g, 0, 0)),
            pl.BlockSpec((1, 1, 1, n_cmp, HEAD_DIM), lambda b, g, i: (1, b, g, 0, 0)),
            pl.BlockSpec((seq, HEAD_DIM), lambda b, g, i: (b, BLK_KW + g)),
            pl.BlockSpec((seq, HEAD_DIM), lambda b, g, i: (b, BLK_VW + g)),
            pl.BlockSpec((tq, LANES), lambda b, g, i: (b * nq + i, BLK_NG + g)),
            pl.BlockSpec((n_cmp, nblkp), lambda b, g, i: (0, 0)),
        ],
        out_specs=(pl.BlockSpec((tq, q_cols), lambda b, g, i: (b * nq + i, g)),
                   pl.BlockSpec((1, 1, nblkp // LANES, tq, LANES), lambda b, g, i: (b, g, 0, i, 0))),
        scratch_shapes=[pltpu.VMEM((tq, nblkp), F32), pltpu.VMEM((nblkp, tq), F32)],
        compiler_params=pltpu.CompilerParams(
            dimension_semantics=("arbitrary", "arbitrary", "arbitrary"), vmem_limit_bytes=VMEM_LIMIT),
        name="cmpwin",
    )(proj, cmp_kv, cmp_kv, proj, proj, proj, ov)


def _select_kernel(q_ref, selb_ref, ks_ref, vs_ref, hot_ref, part_ref, ng_ref, nz_ref, y_ref,
                   acc_ref, m_ref, p_ref, alpha_ref, *, tk):
    tq = q_ref.shape[0]
    q0 = pl.program_id(2) * tq

    def query_rows(h, slab):
        return jnp.concatenate([q_ref[:, h * HEAD_DIM:(h + 1) * HEAD_DIM], selb_ref[0, 0, slab]], axis=1)

    acc_ref[...] = jnp.zeros_like(acc_ref)
    m_ref[...] = jnp.full_like(m_ref, NEG_BIG)

    def tile_step(kt, pending, causal):
        if kt is not None:
            k0 = pl.multiple_of(kt * tk, tk)
            hot0 = pl.multiple_of(k0 % hot_ref.shape[0], tk)
            k_t = jnp.concatenate([ks_ref[pl.ds(k0, tk), :], hot_ref[pl.ds(hot0, tk), :]], axis=1)
            slab = k0 // (SEL_BLOCK * LANES)
            if causal:
                t_pos = lax.broadcasted_iota(jnp.int32, (tq, tk), 0) + q0
                k_pos = lax.broadcasted_iota(jnp.int32, (tq, tk), 1) + k0
                vis = k_pos <= t_pos
        if pending is not None:
            v_t = jnp.concatenate([vs_ref[pl.ds(pl.multiple_of(pending * tk, tk), tk), :],
                                   jnp.ones((tk, HEAD_DIM), BF16)], axis=1)
        if kt is not None:
            s_next = _nt_dot(query_rows(0, slab), k_t)
        for h in range(HEADS_PER_GROUP):
            if kt is not None:
                s = s_next
                if h + 1 < HEADS_PER_GROUP:
                    s_next = _nt_dot(query_rows(h + 1, slab), k_t)
            if pending is not None:
                alpha = alpha_ref[h]
                acc_ref[h] = (acc_ref[h] * jnp.concatenate([alpha, alpha], axis=1)
                              + jnp.dot(p_ref[h], v_t, preferred_element_type=F32))
            if kt is not None:
                if causal:
                    s = jnp.where(vis, s, NEG_BIG)
                m_prev = m_ref[h]
                m_new = jnp.maximum(m_prev, jnp.max(s, axis=-1, keepdims=True))
                p_ref[h] = jnp.exp2(s - jnp.concatenate([m_new] * (tk // LANES), axis=1)).astype(BF16)
                alpha_ref[h] = jnp.exp2(m_prev - m_new)
                m_ref[h] = m_new

    n_full = q0 // tk

    @pl.when(n_full > 0)
    def _():
        tile_step(0, None, False)

    @pl.when(n_full == 0)
    def _():
        p_ref[...] = jnp.zeros_like(p_ref)
        alpha_ref[...] = jnp.ones_like(alpha_ref)

    def step(kt, carry):
        tile_step(kt, kt - 1, False)
        return carry

    lax.fori_loop(1, n_full, step, 0)
    tile_step(n_full, jnp.maximum(n_full - 1, 0), True)
    tile_step(None, n_full, False)

    gates = _sigmoid(ng_ref[...].astype(F32))
    for h in range(HEADS_PER_GROUP):
        sl = slice(h * HEAD_DIM, (h + 1) * HEAD_DIM)
        acc = acc_ref[h]
        o_s = acc[:, :HEAD_DIM] / jnp.maximum(acc[:, HEAD_DIM:], 1e-30)
        g_s = gates[:, 1 * HEADS_PER_GROUP + h:1 * HEADS_PER_GROUP + h + 1]
        z = nz_ref[:, sl].astype(F32)
        y_ref[:, sl] = ((part_ref[:, sl] + g_s * o_s) * (z * _sigmoid(z))).astype(BF16)


def _select(proj, selb, block_hot, part, batch, seq):
    tq = min(SEL_TQ, seq)
    tk = min(SEL_TK, seq)
    nq = seq // tq
    n_slabs = selb.shape[2]
    q_cols = HEADS_PER_GROUP * HEAD_DIM
    kernel = functools.partial(_select_kernel, tk=tk)
    return pl.pallas_call(
        kernel,
        out_shape=jax.ShapeDtypeStruct((batch * seq, NSA_HEADS * HEAD_DIM), BF16),
        grid=(batch, NSA_KV_GROUPS, nq),
        in_specs=[
            pl.BlockSpec((tq, q_cols), lambda b, g, i: (b * nq + i, BLK_NQ * LANES // q_cols + g)),
            pl.BlockSpec((1, 1, n_slabs, tq, LANES), lambda b, g, i: (b, g, 0, i, 0)),
            pl.BlockSpec((seq, HEAD_DIM), lambda b, g, i: (b, BLK_KS + g), pipeline_mode=pl.Buffered(1)),
            pl.BlockSpec((seq, HEAD_DIM), lambda b, g, i: (b, BLK_VS + g), pipeline_mode=pl.Buffered(1)),
            pl.BlockSpec(block_hot.shape, lambda b, g, i: (0, 0), pipeline_mode=pl.Buffered(1)),
            pl.BlockSpec((tq, q_cols), lambda b, g, i: (b * nq + i, g)),
            pl.BlockSpec((tq, LANES), lambda b, g, i: (b * nq + i, BLK_NG + g)),
            pl.BlockSpec((tq, q_cols), lambda b, g, i: (b * nq + i, BLK_NZ * LANES // q_cols + g)),
        ],
        out_specs=pl.BlockSpec((tq, q_cols), lambda b, g, i: (b * nq + i, g)),
        scratch_shapes=[
            pltpu.VMEM((HEADS_PER_GROUP, tq, 2 * HEAD_DIM), F32),
            pltpu.VMEM((HEADS_PER_GROUP, tq, LANES), F32),
            pltpu.VMEM((HEADS_PER_GROUP, tq, tk), BF16),
            pltpu.VMEM((HEADS_PER_GROUP, tq, LANES), F32),
        ],
        compiler_params=pltpu.CompilerParams(
            dimension_semantics=("arbitrary", "arbitrary", "arbitrary"), vmem_limit_bytes=VMEM_LIMIT),
        name="select",
    )(proj, selb, proj, proj, block_hot, part, proj, proj)


def _outproj_kernel(yh_ref, yn_ref, w_ref, x_ref, g_ref, o_ref):
    half = yh_ref.shape[1]
    z = (jnp.dot(yh_ref[...], w_ref[:half, :], preferred_element_type=F32)
         + jnp.dot(yn_ref[...], w_ref[half:, :], preferred_element_type=F32))
    ms = jnp.mean(z * z, axis=-1, keepdims=True)
    o_ref[...] = x_ref[...] + z * lax.rsqrt(ms + NORM_EPS) * g_ref[...]


def _outproj(yh, yn, w, x2, g):
    n = x2.shape[0]
    tm = min(OUT_TM, n)
    return pl.pallas_call(
        _outproj_kernel,
        out_shape=jax.ShapeDtypeStruct((n, D_MODEL), F32),
        grid=(n // tm,),
        in_specs=[
            pl.BlockSpec((tm, yh.shape[1]), lambda i: (i, 0)),
            pl.BlockSpec((tm, yn.shape[1]), lambda i: (i, 0)),
            pl.BlockSpec(w.shape, lambda i: (0, 0)),
            pl.BlockSpec((tm, D_MODEL), lambda i: (i, 0)),
            pl.BlockSpec((1, D_MODEL), lambda i: (0, 0)),
        ],
        out_specs=pl.BlockSpec((tm, D_MODEL), lambda i: (i, 0)),
        compiler_params=pltpu.CompilerParams(
            dimension_semantics=("arbitrary",), vmem_limit_bytes=VMEM_LIMIT),
        name="outproj",
    )(yh, yn, w, x2, g)


def _permute_weights(w):
    def seg(name, width):
        return w[:, _SRC[name]:_SRC[name] + width]

    ng = seg("ng", NSA_HEADS * N_BRANCHES).reshape(-1, NSA_KV_GROUPS, HEADS_PER_GROUP, N_BRANCHES)
    ng = ng.transpose(0, 1, 3, 2).reshape(-1, NSA_KV_GROUPS, N_BRANCHES * HEADS_PER_GROUP)
    ng = jnp.pad(ng, ((0, 0), (0, 0), (0, LANES - N_BRANCHES * HEADS_PER_GROUP))).reshape(-1, NSA_KV_GROUPS * LANES)
    placed = [(BLK_HQ, seg("hq", 1024)), (BLK_HI, seg("hi", 1024)), (BLK_HZ, seg("hz", 1024)),
              (BLK_NQ, seg("nq", 1024) * (HEAD_DIM ** -0.5 * LOG2_E)),
              (BLK_KS, seg("ks", 256)), (BLK_KW, seg("kw", 256)), (BLK_KC, seg("kc", 256)),
              (BLK_VC, seg("vc", 256)), (BLK_VS, seg("vs", 256)), (BLK_VW, seg("vw", 256)),
              (BLK_NZ, seg("nz", 1024)), (BLK_NG, ng), (BLK_HF, seg("hf", 1024))]
    parts, col = [], 0
    for blk, cols in sorted(placed, key=lambda p: p[0]):
        if blk * LANES > col:
            parts.append(jnp.zeros((w.shape[0], blk * LANES - col), w.dtype))
        parts.append(cols)
        col = blk * LANES + cols.shape[1]
    if col < PROJ_COLS:
        parts.append(jnp.zeros((w.shape[0], PROJ_COLS - col), w.dtype))
    return jnp.concatenate(parts, axis=1).astype(BF16)


def _rope_tables(pos):
    half = ROT_DIM // 2
    inv = ROPE_THETA ** (-2.0 * jnp.arange(half, dtype=F32) / ROT_DIM)
    ang = pos.astype(F32)[:, None] * inv[None, :]
    cos, sin = jnp.cos(ang), jnp.sin(ang)
    n = pos.shape[0]
    pad = jnp.zeros((n, HEAD_DIM - ROT_DIM), F32)
    zero = jnp.zeros((n, half), F32)
    c = jnp.concatenate([cos, cos, pad + 1.0], axis=1)
    s1 = jnp.concatenate([-sin, zero, pad], axis=1)
    s2 = jnp.concatenate([zero, sin, pad], axis=1)
    return c, s1, s2


def _pair_levels():
    t = np.arange(HGRN_CHUNK)[:, None]
    s = np.arange(HGRN_CHUNK)[None, :]
    high_bit = 2 ** np.floor(np.log2(np.maximum(t ^ s, 1))).astype(np.int64)
    return jnp.asarray(np.where(t > s, high_bit, np.where(t == s, 0, -1)), jnp.int32)


def _block_one_hot(seq):
    rows = min(seq, SEL_BLOCK * LANES)
    lane = (np.arange(rows) // SEL_BLOCK) % LANES
    return jnp.asarray(lane[:, None] == np.arange(LANES)[None, :], BF16)


def _overlap_matrix(n_cmp, n_blocks, nblkp):
    start = np.arange(n_cmp)[:, None] * CMP_STRIDE
    sel = np.arange(nblkp)[None, :] * SEL_BLOCK
    ov = (start < sel + SEL_BLOCK) & (start + CMP_BLOCK > sel)
    ov &= np.arange(nblkp)[None, :] < n_blocks
    ov &= np.arange(n_cmp)[:, None] < (n_cmp - 1)
    return jnp.asarray(ov, BF16)


def kernel(x, pre_norm, post_norm, w_in, hgrn_lb_logits, hgrn_out_norm, cmp_pos_k, cmp_w1_k, cmp_b1_k,
           cmp_w2_k, cmp_pos_v, cmp_w1_v, cmp_b1_v, cmp_w2_v, w_out):
    batch, seq, _ = x.shape
    depth = w_in.shape[0]
    n_cmp = seq // CMP_STRIDE
    n_blocks = seq // SEL_BLOCK
    nblkp = -(-n_blocks // LANES) * LANES

    lb_probs = jax.nn.softmax(hgrn_lb_logits.astype(F32), axis=0)
    lower = jnp.maximum(jnp.cumsum(lb_probs, axis=0) - lb_probs[0:1], 0.0)
    lower = lower.reshape(depth, HGRN_HEADS, 1, HEAD_DIM)
    g_out = jnp.broadcast_to(hgrn_out_norm.astype(F32)[:, None, None, :], lower.shape)
    hgrn_par = jnp.concatenate(
        [jnp.log(lower), jnp.log1p(-lower), 1.0 - lower, g_out, jnp.zeros((depth, HGRN_HEADS, 4, HEAD_DIM), F32)],
        axis=2)

    rope_c, rope_s1, rope_s2 = _rope_tables(jnp.arange(seq))
    cmp_tab_k = jnp.stack(_rope_tables(jnp.arange(n_cmp) * CMP_STRIDE + CMP_BLOCK - 1))
    cmp_tab_v = jnp.stack([jnp.ones((n_cmp, HEAD_DIM), F32), jnp.zeros((n_cmp, HEAD_DIM), F32),
                           jnp.zeros((n_cmp, HEAD_DIM), F32)])
    cmp_tabs = jnp.stack([cmp_tab_k, cmp_tab_v])
    ov = _overlap_matrix(n_cmp, n_blocks, nblkp)
    tri = jnp.asarray(np.tril(np.ones((HGRN_CHUNK, HGRN_CHUNK), np.float32)))
    lvl = _pair_levels()
    block_hot = _block_one_hot(seq)

    x2 = x.reshape(batch * seq, D_MODEL)
    for layer in range(depth):
        proj, proj_f32 = _proj(x2, pre_norm[layer][None, :], _permute_weights(w_in[layer]),
                               rope_c, rope_s1, rope_s2, seq)
        y_h = _hgrn(proj, proj_f32, hgrn_par[layer], tri, lvl, batch, seq)
        cmp_kv = _compress(
            proj,
            jnp.stack([cmp_w1_k[layer], cmp_w1_v[layer]]).astype(BF16),
            jnp.stack([cmp_b1_k[layer], cmp_b1_v[layer]])[:, None, :],
            jnp.stack([cmp_w2_k[layer], cmp_w2_v[layer]]).astype(BF16),
            jnp.stack([cmp_pos_k[layer], cmp_pos_v[layer]]),
            cmp_tabs, batch, seq)
        part, selb = _cmpwin(proj, cmp_kv, ov, batch, seq)
        y_n = _select(proj, selb, block_hot, part, batch, seq)
        x2 = _outproj(y_h, y_n, w_out[layer].astype(BF16), x2, post_norm[layer][None, :])
    return x2.reshape(batch, seq, D_MODEL)
```

```python
import functools

import numpy as np
import jax
import jax.numpy as jnp
from jax import lax
from jax.experimental import pallas as pl
from jax.experimental.pallas import tpu as pltpu

F32 = jnp.float32
BF16 = jnp.bfloat16

D_MODEL = 1024
HEAD_DIM = 128
HGRN_HEADS = 8
NSA_HEADS = 8
NSA_KV_GROUPS = 2
HEADS_PER_GROUP = NSA_HEADS // NSA_KV_GROUPS
N_BRANCHES = 3
HGRN_CHUNK = 64
CMP_BLOCK = 32
CMP_STRIDE = 16
SEL_BLOCK = 64
N_SELECT = 16
WINDOW = 512
ROPE_THETA = 500000.0
ROT_DIM = HEAD_DIM // 4
NORM_EPS = 1e-6
TAKEN = -4.0
NEG_BIG = -1e30
LOG2_E = 1.4426950408889634

LANES = 128
SUBLANES = 8
BLK_HQ, BLK_HI, BLK_HZ = 0, 8, 16
BLK_NQ, BLK_KS, BLK_KW, BLK_KC, BLK_VC, BLK_VS, BLK_VW = 24, 32, 34, 36, 38, 40, 42
BLK_NZ, BLK_NG, BLK_HF = 44, 52, 56
PROJ_BLOCKS = 64
PROJ_COLS = PROJ_BLOCKS * LANES
PROJ_BF16_COLS = BLK_HF * LANES
_SRC = dict(hq=0, hf=1024, hi=2048, hz=3072, nq=4096, kc=5120, vc=5376, ks=5632, vs=5888,
            kw=6144, vw=6400, ng=6656, nz=6680)

PROJ_TM = 1024
PROJ_TN = 1024
ROPE_FULL_TILE = BLK_NQ * LANES // PROJ_TN
ROPE_HEAD_TILE = BLK_KS * LANES // PROJ_TN
ROPE_HEAD_BLOCKS = 4
F32_TILE = BLK_HF * LANES // PROJ_TN
HGRN_T = 1024
HGRN_HEADS_PER_STEP = 4
ATT_TQ = 512
WIN_TQ = 256
CMP_KEY_CHUNK = 256
SEL_TQ = 512
SEL_TK = 1024
OUT_TM = 1024
VMEM_LIMIT = 56 * 1024 * 1024


def _nt_dot(a, b):
    return lax.dot_general(a, b, (((1,), (1,)), ((), ())), preferred_element_type=F32)


def _sigmoid(x):
    return 1.0 / (1.0 + jnp.exp(-x))


def _proj_kernel(x_ref, g_ref, w_ref, c_ref, s1_ref, s2_ref, o_ref, o32_ref, h_ref, acc_ref):
    j = pl.program_id(1)

    @pl.when(j == 0)
    def _():
        x = x_ref[...]
        ms = jnp.mean(x * x, axis=-1, keepdims=True)
        h_ref[...] = (x * lax.rsqrt(ms + NORM_EPS) * g_ref[...]).astype(BF16)

    @pl.when(j == F32_TILE)
    def _():
        o32_ref[...] = jnp.dot(h_ref[...], w_ref[...], preferred_element_type=F32)

    def store(n_rope_blocks):
        acc_ref[...] = jnp.dot(h_ref[...], w_ref[...], preferred_element_type=F32)
        for blk in range(n_rope_blocks):
            sl = slice(blk * LANES, (blk + 1) * LANES)
            a = acc_ref[:, sl]
            o_ref[:, sl] = (a * c_ref[...] + pltpu.roll(a, LANES - ROT_DIM // 2, 1) * s1_ref[...]
                            + pltpu.roll(a, ROT_DIM // 2, 1) * s2_ref[...]).astype(BF16)
        if n_rope_blocks * LANES < PROJ_TN:
            rest = slice(n_rope_blocks * LANES, PROJ_TN)
            o_ref[:, rest] = acc_ref[:, rest].astype(BF16)

    @pl.when(j == ROPE_FULL_TILE)
    def _():
        store(PROJ_TN // LANES)

    @pl.when(j == ROPE_HEAD_TILE)
    def _():
        store(ROPE_HEAD_BLOCKS)

    @pl.when((j != ROPE_FULL_TILE) & (j != ROPE_HEAD_TILE) & (j != F32_TILE))
    def _():
        o_ref[...] = jnp.dot(h_ref[...], w_ref[...], preferred_element_type=F32).astype(BF16)


def _proj(x2, g, w, rope_c, rope_s1, rope_s2, seq):
    n = x2.shape[0]
    tm = min(PROJ_TM, seq)
    pos_tiles = seq // tm
    return pl.pallas_call(
        _proj_kernel,
        out_shape=(jax.ShapeDtypeStruct((n, PROJ_BF16_COLS), BF16),
                   jax.ShapeDtypeStruct((n, PROJ_COLS - PROJ_BF16_COLS), F32)),
        grid=(n // tm, PROJ_COLS // PROJ_TN),
        in_specs=[
            pl.BlockSpec((tm, D_MODEL), lambda i, j: (i, 0)),
            pl.BlockSpec((1, D_MODEL), lambda i, j: (0, 0)),
            pl.BlockSpec((D_MODEL, PROJ_TN), lambda i, j: (0, j)),
            pl.BlockSpec((tm, LANES), lambda i, j: (i % pos_tiles, 0)),
            pl.BlockSpec((tm, LANES), lambda i, j: (i % pos_tiles, 0)),
            pl.BlockSpec((tm, LANES), lambda i, j: (i % pos_tiles, 0)),
        ],
        out_specs=(pl.BlockSpec((tm, PROJ_TN), lambda i, j: (i, jnp.minimum(j, F32_TILE - 1))),
                   pl.BlockSpec((tm, PROJ_TN), lambda i, j: (i, 0))),
        scratch_shapes=[pltpu.VMEM((tm, D_MODEL), BF16), pltpu.VMEM((tm, PROJ_TN), F32)],
        compiler_params=pltpu.CompilerParams(
            dimension_semantics=("arbitrary", "arbitrary"), vmem_limit_bytes=VMEM_LIMIT),
        name="proj",
    )(x2, g, w, rope_c, rope_s1, rope_s2)


def _hgrn_kernel(q_ref, f_ref, i_ref, z_ref, par_ref, tri_ref, lvl_ref, y_ref, st_ref, o_ref):
    @pl.when(pl.program_id(2) == 0)
    def _():
        st_ref[...] = jnp.zeros_like(st_ref)

    t = q_ref.shape[0]
    n_heads = q_ref.shape[1] // HEAD_DIM
    nc = t // HGRN_CHUNK
    units = n_heads * nc
    shape3 = (units, HGRN_CHUNK, HEAD_DIM)

    def stacked(ref):
        return jnp.concatenate([ref[:, h * HEAD_DIM:(h + 1) * HEAD_DIM] for h in range(n_heads)], axis=0)

    lf_heads, k_heads = [], []
    for h in range(n_heads):
        log_lb, log_1m_lb, one_m_lb = par_ref[h, 0:1, :], par_ref[h, 1:2, :], par_ref[h, 2:3, :]
        x = f_ref[:, h * HEAD_DIM:(h + 1) * HEAD_DIM]
        e = jnp.exp(-jnp.abs(x))
        c = log_1m_lb + jnp.minimum(x, 0.0) - jnp.log(1.0 + e)
        lf_heads.append(jnp.maximum(log_lb, c) + jnp.log(1.0 + jnp.exp(-jnp.abs(log_lb - c))))
        k_heads.append(one_m_lb * jnp.where(x >= 0.0, e, 1.0) / (1.0 + e))
    k3 = jnp.concatenate(k_heads, axis=0).reshape(shape3)
    q3 = stacked(q_ref).astype(F32).reshape(shape3)
    v3 = stacked(i_ref).astype(F32).reshape(shape3)
    v_bf = v3.astype(BF16)

    lf3 = (jnp.concatenate(lf_heads, axis=0) * LOG2_E).reshape(shape3)
    b3 = jnp.stack([jnp.dot(tri_ref[...], lf3[u], precision=lax.Precision.HIGHEST,
                            preferred_element_type=F32) for u in range(units)])

    rows = units * HGRN_CHUNK
    tiles = (rows // SUBLANES, SUBLANES, HEAD_DIM)
    b_tile = b3.reshape(tiles)
    tile_row = lax.broadcasted_iota(jnp.int32, tiles, 1)
    lvl = lvl_ref[...]
    attn = jnp.where(lvl == 0, jnp.stack([_nt_dot(q3[u].astype(BF16), k3[u].astype(BF16)) for u in range(units)]), 0.0)
    n = HGRN_CHUNK // 2
    while n >= 1:
        if 2 * n >= SUBLANES:
            blocks = (rows // (2 * n), 2 * n, HEAD_DIM)
            ref = jnp.broadcast_to(b3.reshape(blocks)[:, n - 1:n, :], blocks).reshape(shape3)
            side = jnp.concatenate([k3.reshape(blocks)[:, :n], q3.reshape(blocks)[:, n:]], axis=1).reshape(shape3)
        else:
            ref = jnp.broadcast_to(b_tile[:, n - 1:n, :], tiles)
            for lo in range(2 * n, SUBLANES, 2 * n):
                ref = jnp.where(tile_row >= lo, jnp.broadcast_to(b_tile[:, lo + n - 1:lo + n, :], tiles), ref)
            ref = ref.reshape(shape3)
            side = jnp.where((tile_row & n) != 0, q3.reshape(tiles), k3.reshape(tiles)).reshape(shape3)
        d = b3 - ref
        r_n = (side * jnp.exp2(jnp.minimum(d, -d))).astype(BF16)
        a_n = jnp.stack([_nt_dot(r_n[u], r_n[u]) for u in range(units)])
        attn = jnp.where(lvl == n, a_n, attn)
        n //= 2
    attn_bf = attn.astype(BF16)
    o_intra = jnp.stack([jnp.dot(attn_bf[u], v_bf[u], preferred_element_type=F32) for u in range(units)])

    b_last = b3[:, HGRN_CHUNK - 1:HGRN_CHUNK, :]
    decay = jnp.exp2(b_last)
    q_dec = (q3 * jnp.exp2(b3)).astype(BF16)
    k_dec = (k3 * jnp.exp2(b_last - b3)).astype(BF16)
    upd = [lax.dot_general(v_bf[u], k_dec[u], (((0,), (0,)), ((), ())), preferred_element_type=F32)
           for u in range(units)]
    st = [st_ref[h] for h in range(n_heads)]
    for ci in range(nc):
        for h in range(n_heads):
            u = h * nc + ci
            o_ref[h, ci * HGRN_CHUNK:(ci + 1) * HGRN_CHUNK, :] = (
                o_intra[u] + _nt_dot(q_dec[u], st[h].astype(BF16)))
            st[h] = st[h] * decay[u] + upd[u]
    for h in range(n_heads):
        st_ref[h] = st[h]
        cols = slice(h * HEAD_DIM, (h + 1) * HEAD_DIM)
        o = o_ref[h]
        ms = jnp.mean(o * o, axis=-1, keepdims=True)
        z = z_ref[:, cols].astype(F32)
        y_ref[:, cols] = (o * lax.rsqrt(ms + NORM_EPS) * par_ref[h, 3:4, :] * (z * _sigmoid(z))).astype(BF16)


def _hgrn(proj, proj_f32, par, tri, lvl, batch, seq):
    t = min(HGRN_T, seq)
    nt = seq // t

    hp = HGRN_HEADS_PER_STEP
    width = hp * HEAD_DIM

    def col(base):
        return pl.BlockSpec((t, width), lambda b, h, ti: (b * nt + ti, base // hp + h))

    return pl.pallas_call(
        _hgrn_kernel,
        out_shape=jax.ShapeDtypeStruct((batch * seq, HGRN_HEADS * HEAD_DIM), BF16),
        grid=(batch, HGRN_HEADS // hp, nt),
        in_specs=[
            col(BLK_HQ), col(0), col(BLK_HI), col(BLK_HZ),
            pl.BlockSpec((hp, 8, HEAD_DIM), lambda b, h, ti: (h, 0, 0)),
            pl.BlockSpec((HGRN_CHUNK, HGRN_CHUNK), lambda b, h, ti: (0, 0)),
            pl.BlockSpec((HGRN_CHUNK, HGRN_CHUNK), lambda b, h, ti: (0, 0)),
        ],
        out_specs=pl.BlockSpec((t, width), lambda b, h, ti: (b * nt + ti, h)),
        scratch_shapes=[pltpu.VMEM((hp, HEAD_DIM, HEAD_DIM), F32), pltpu.VMEM((hp, t, HEAD_DIM), F32)],
        compiler_params=pltpu.CompilerParams(
            dimension_semantics=("arbitrary", "arbitrary", "arbitrary"), vmem_limit_bytes=VMEM_LIMIT),
        name="hgrn",
    )(proj, proj_f32, proj, proj, par, tri, lvl)


def _compress_kernel(x_ref, w1_ref, b1_ref, w2_ref, pos_ref, tab_ref, o_ref, x32_ref):
    n_rows = o_ref.shape[-2]
    x32_ref[...] = x_ref[...].astype(F32)
    acc_lo = jnp.zeros((n_rows, HEAD_DIM), F32)
    acc_hi = jnp.zeros((n_rows, HEAD_DIM), F32)
    for r in range(CMP_STRIDE):
        xr = x32_ref[pl.ds(r, n_rows, stride=CMP_STRIDE), :]
        lo, hi = r, CMP_STRIDE + r
        acc_lo = acc_lo + jnp.dot((xr + pos_ref[0, lo:lo + 1, :]).astype(BF16),
                                  w1_ref[0, lo * HEAD_DIM:(lo + 1) * HEAD_DIM, :],
                                  preferred_element_type=F32)
        acc_hi = acc_hi + jnp.dot((xr + pos_ref[0, hi:hi + 1, :]).astype(BF16),
                                  w1_ref[0, hi * HEAD_DIM:(hi + 1) * HEAD_DIM, :],
                                  preferred_element_type=F32)
    pre = acc_lo + pltpu.roll(acc_hi, n_rows - 1, 0) + b1_ref[0]
    mid = pre * _sigmoid(pre)
    out = jnp.dot(mid.astype(BF16), w2_ref[0], preferred_element_type=F32)
    o_ref[0, 0, 0] = (out * tab_ref[0, 0] + pltpu.roll(out, LANES - ROT_DIM // 2, 1) * tab_ref[0, 1]
                      + pltpu.roll(out, ROT_DIM // 2, 1) * tab_ref[0, 2]).astype(BF16)


def _compress(proj, w1, b1, w2, pos, tabs, batch, seq):
    n_rows = seq // CMP_STRIDE
    return pl.pallas_call(
        _compress_kernel,
        out_shape=jax.ShapeDtypeStruct((2, batch, NSA_KV_GROUPS, n_rows, HEAD_DIM), BF16),
        grid=(2, batch, NSA_KV_GROUPS),
        in_specs=[
            pl.BlockSpec((seq, HEAD_DIM), lambda kv, b, g: (b, BLK_KC + 2 * kv + g)),
            pl.BlockSpec((1, CMP_BLOCK * HEAD_DIM, HEAD_DIM), lambda kv, b, g: (kv, 0, 0)),
            pl.BlockSpec((1, 1, HEAD_DIM), lambda kv, b, g: (kv, 0, 0)),
            pl.BlockSpec((1, HEAD_DIM, HEAD_DIM), lambda kv, b, g: (kv, 0, 0)),
            pl.BlockSpec((1, CMP_BLOCK, HEAD_DIM), lambda kv, b, g: (kv, 0, 0)),
            pl.BlockSpec((1, 3, n_rows, HEAD_DIM), lambda kv, b, g: (kv, 0, 0, 0)),
        ],
        out_specs=pl.BlockSpec((1, 1, 1, n_rows, HEAD_DIM), lambda kv, b, g: (kv, b, g, 0, 0)),
        scratch_shapes=[pltpu.VMEM((seq, HEAD_DIM), F32)],
        compiler_params=pltpu.CompilerParams(
            dimension_semantics=("arbitrary", "arbitrary", "arbitrary"), vmem_limit_bytes=VMEM_LIMIT),
        name="compress",
    )(proj, w1, b1, w2, pos, tabs)


def _cmpwin_kernel(q_ref, kc_ref, vc_ref, kw_ref, vw_ref, ng_ref, ov_ref, part_ref, selb_ref,
                   imp_ref, val_ref, *, n_blocks, win_len, key_chunk):
    tq = q_ref.shape[0]
    n_cmp = kc_ref.shape[-2]
    nblkp = selb_ref.shape[2] * LANES
    q0 = pl.program_id(2) * tq
    gates = _sigmoid(ng_ref[...].astype(F32))
    q_heads = [q_ref[:, h * HEAD_DIM:(h + 1) * HEAD_DIM] for h in range(HEADS_PER_GROUP)]

    def attend(s, values_ones, vis, all_masked_possible):
        s = jnp.where(vis, s, NEG_BIG)
        m = jnp.max(s, axis=-1, keepdims=True)
        if all_masked_possible:
            m = jnp.where(m < 0.5 * NEG_BIG, 0.0, m)
        p = jnp.exp2(s - m)
        o = jnp.dot(p.astype(BF16), values_ones, preferred_element_type=F32)
        inv = 1.0 / jnp.maximum(o[:, HEAD_DIM:], 1e-30)
        return o[:, :HEAD_DIM] * inv, p, inv

    tw = win_len - WINDOW
    for r0 in range(0, tq, tw):
        rows = slice(r0, r0 + tw)
        w0 = pl.multiple_of(jnp.maximum(q0 + r0 + tw - win_len, 0), tw)
        kwin = kw_ref[pl.ds(w0, win_len), :]
        vwin = jnp.concatenate([vw_ref[pl.ds(w0, win_len), :], jnp.ones((win_len, HEAD_DIM), BF16)], axis=1)
        t_w = lax.broadcasted_iota(jnp.int32, (tw, win_len), 0) + (q0 + r0)
        kp_w = lax.broadcasted_iota(jnp.int32, (tw, win_len), 1) + w0
        vis_w = lax.bitcast_convert_type(t_w - kp_w, jnp.uint32) < WINDOW
        s_w = [_nt_dot(qh[rows], kwin) for qh in q_heads]
        for h in range(HEADS_PER_GROUP):
            o_w, _, _ = attend(s_w[h], vwin, vis_w, False)
            g_w = gates[rows, 2 * HEADS_PER_GROUP + h:2 * HEADS_PER_GROUP + h + 1]
            part_ref[rows, h * HEAD_DIM:(h + 1) * HEAD_DIM] = g_w * o_w

    def compressed(n_keys):
        kc = kc_ref[0, 0, 0, :n_keys, :]
        vc = jnp.concatenate([vc_ref[0, 0, 0, :n_keys, :], jnp.ones((n_keys, HEAD_DIM), BF16)], axis=1)
        t_c = lax.broadcasted_iota(jnp.int32, (tq, n_keys), 0) + q0
        n_c = lax.broadcasted_iota(jnp.int32, (tq, n_keys), 1)
        vis_c = n_c * CMP_STRIDE + (CMP_BLOCK - 1) <= t_c
        p_sum = jnp.zeros((tq, n_keys), F32)
        s_c = [_nt_dot(qh, kc) for qh in q_heads]
        for h in range(HEADS_PER_GROUP):
            sl = slice(h * HEAD_DIM, (h + 1) * HEAD_DIM)
            o_c, p, inv = attend(s_c[h], vc, vis_c, True)
            p_sum = p_sum + p * jnp.concatenate([inv] * (n_keys // LANES), axis=1)
            g_c = gates[:, 0 * HEADS_PER_GROUP + h:0 * HEADS_PER_GROUP + h + 1]
            part_ref[:, sl] = part_ref[:, sl] + g_c * o_c
        p_hi = p_sum.astype(BF16)
        p_lo = (p_sum - p_hi.astype(F32)).astype(BF16)
        imp_ref[...] = (jnp.dot(p_hi, ov_ref[:n_keys, :], preferred_element_type=F32)
                        + jnp.dot(p_lo, ov_ref[:n_keys, :], preferred_element_type=F32))

    n_visible = (q0 + tq - CMP_BLOCK) // CMP_STRIDE + 1
    n_chunks = jnp.minimum((n_visible + key_chunk - 1) // key_chunk, n_cmp // key_chunk)
    for c in range(1, n_cmp // key_chunk + 1):
        @pl.when(n_chunks == c)
        def _():
            compressed(c * key_chunk)

    def select_blocks(n_rows):
        imp_t = imp_ref[:, :n_rows].T
        t_b = lax.broadcasted_iota(jnp.int32, (n_rows, tq), 1) + q0
        blk = lax.broadcasted_iota(jnp.int32, (n_rows, tq), 0)
        cur = lax.shift_right_logical(t_b, 6)
        near = lax.bitcast_convert_type(cur - blk, jnp.uint32) < 2
        val = jnp.where(near, TAKEN, jnp.where(blk * SEL_BLOCK <= t_b, imp_t, -1.0))
        val = jnp.where(blk == 0, TAKEN, val)
        val_ref[:n_rows] = jnp.where(blk < n_blocks, val, -3.0)
        blk_f = blk.astype(F32)
        n_top = min(N_SELECT, n_blocks)
        quota = n_top - 1 - jnp.minimum(cur[0:1, :], 2)

        def pick(it, carry):
            v = val_ref[:n_rows]
            m = jnp.max(v, axis=0, keepdims=True)
            idx = jnp.min(jnp.where(v == m, blk_f, float(nblkp)), axis=0, keepdims=True)
            idx = jnp.where(it < quota, idx, -1.0)
            val_ref[:n_rows] = jnp.where(blk_f == idx, TAKEN, v)
            return carry

        lax.fori_loop(0, n_top - 1 - jnp.minimum(lax.shift_right_logical(q0, 6), 2), pick, 0)
        bias = jnp.where(val_ref[:n_rows].T == TAKEN, 0.0, NEG_BIG).astype(BF16)
        for part in range(nblkp // LANES):
            if part * LANES < n_rows:
                selb_ref[0, 0, part] = bias[:, part * LANES:(part + 1) * LANES]
            else:
                selb_ref[0, 0, part] = jnp.full((tq, LANES), NEG_BIG, BF16)

    if nblkp > LANES:
        @pl.when(q0 + tq <= LANES * SEL_BLOCK)
        def _():
            select_blocks(LANES)

        @pl.when(q0 + tq > LANES * SEL_BLOCK)
        def _():
            select_blocks(nblkp)
    else:
        select_blocks(nblkp)


def _cmpwin(proj, cmp_kv, ov, batch, seq):
    tq = min(ATT_TQ, seq)
    nq = seq // tq
    n_cmp = seq // CMP_STRIDE
    n_blocks = seq // SEL_BLOCK
    nblkp = ov.shape[1]
    win_len = min(WINDOW + min(WIN_TQ, tq), seq)
    q_cols = HEADS_PER_GROUP * HEAD_DIM
    key_chunk = min(CMP_KEY_CHUNK, n_cmp)
    kernel = functools.partial(_cmpwin_kernel, n_blocks=n_blocks, win_len=win_len, key_chunk=key_chunk)
    return pl.pallas_call(
        kernel,
        out_shape=(jax.ShapeDtypeStruct((batch * seq, NSA_HEADS * HEAD_DIM), F32),
                   jax.ShapeDtypeStruct((batch, NSA_KV_GROUPS, nblkp // LANES, seq, LANES), BF16)),
        grid=(batch, NSA_KV_GROUPS, nq),
        in_specs=[
            pl.BlockSpec((tq, q_cols), lambda b, g, i: (b * nq + i, BLK_NQ * LANES // q_cols + g)),
            pl.BlockSpec((1, 1, 1, n_cmp, HEAD_DIM), lambda b, g, i: (0, b, g, 0, 0)),
            pl.BlockSpec((1, 1, 1, n_cmp, HEAD_DIM), lambda b, g, i: (1, b, g, 0, 0)),
            pl.BlockSpec((seq, HEAD_DIM), lambda b, g, i: (b, BLK_KW + g)),
            pl.BlockSpec((seq, HEAD_DIM), lambda b, g, i: (b, BLK_VW + g)),
            pl.BlockSpec((tq, LANES), lambda b, g, i: (b * nq + i, BLK_NG + g)),
            pl.BlockSpec((n_cmp, nblkp), lambda b, g, i: (0, 0)),
        ],
        out_specs=(pl.BlockSpec((tq, q_cols), lambda b, g, i: (b * nq + i, g)),
                   pl.BlockSpec((1, 1, nblkp // LANES, tq, LANES), lambda b, g, i: (b, g, 0, i, 0))),
        scratch_shapes=[pltpu.VMEM((tq, nblkp), F32), pltpu.VMEM((nblkp, tq), F32)],
        compiler_params=pltpu.CompilerParams(
            dimension_semantics=("arbitrary", "arbitrary", "arbitrary"), vmem_limit_bytes=VMEM_LIMIT),
        name="cmpwin",
    )(proj, cmp_kv, cmp_kv, proj, proj, proj, ov)


def _select_kernel(q_ref, selb_ref, ks_ref, vs_ref, hot_ref, part_ref, ng_ref, nz_ref, y_ref,
                   acc_ref, m_ref, p_ref, alpha_ref, *, tk):
    tq = q_ref.shape[0]
    q0 = pl.program_id(2) * tq

    def query_rows(h, slab):
        return jnp.concatenate([q_ref[:, h * HEAD_DIM:(h + 1) * HEAD_DIM], selb_ref[0, 0, slab]], axis=1)

    acc_ref[...] = jnp.zeros_like(acc_ref)
    m_ref[...] = jnp.full_like(m_ref, NEG_BIG)

    def tile_step(kt, pending, causal):
        if kt is not None:
            k0 = pl.multiple_of(kt * tk, tk)
            hot0 = pl.multiple_of(k0 % hot_ref.shape[0], tk)
            k_t = jnp.concatenate([ks_ref[pl.ds(k0, tk), :], hot_ref[pl.ds(hot0, tk), :]], axis=1)
            slab = k0 // (SEL_BLOCK * LANES)
            if causal:
                t_pos = lax.broadcasted_iota(jnp.int32, (tq, tk), 0) + q0
                k_pos = lax.broadcasted_iota(jnp.int32, (tq, tk), 1) + k0
                vis = k_pos <= t_pos
        if pending is not None:
            v_t = jnp.concatenate([vs_ref[pl.ds(pl.multiple_of(pending * tk, tk), tk), :],
                                   jnp.ones((tk, HEAD_DIM), BF16)], axis=1)
        if kt is not None:
            s_next = _nt_dot(query_rows(0, slab), k_t)
        for h in range(HEADS_PER_GROUP):
            if kt is not None:
                s = s_next
                if h + 1 < HEADS_PER_GROUP:
                    s_next = _nt_dot(query_rows(h + 1, slab), k_t)
            if pending is not None:
                alpha = alpha_ref[h]
                acc_ref[h] = (acc_ref[h] * jnp.concatenate([alpha, alpha], axis=1)
                              + jnp.dot(p_ref[h], v_t, preferred_element_type=F32))
            if kt is not None:
                if causal:
                    s = jnp.where(vis, s, NEG_BIG)
                m_prev = m_ref[h]
                m_new = jnp.maximum(m_prev, jnp.max(s, axis=-1, keepdims=True))
                p_ref[h] = jnp.exp2(s - jnp.concatenate([m_new] * (tk // LANES), axis=1)).astype(BF16)
                alpha_ref[h] = jnp.exp2(m_prev - m_new)
                m_ref[h] = m_new

    n_full = q0 // tk

    @pl.when(n_full > 0)
    def _():
        tile_step(0, None, False)

    @pl.when(n_full == 0)
    def _():
        p_ref[...] = jnp.zeros_like(p_ref)
        alpha_ref[...] = jnp.ones_like(alpha_ref)

    def step(kt, carry):
        tile_step(kt, kt - 1, False)
        return carry

    lax.fori_loop(1, n_full, step, 0)
    tile_step(n_full, jnp.maximum(n_full - 1, 0), True)
    tile_step(None, n_full, False)

    gates = _sigmoid(ng_ref[...].astype(F32))
    for h in range(HEADS_PER_GROUP):
        sl = slice(h * HEAD_DIM, (h + 1) * HEAD_DIM)
        acc = acc_ref[h]
        o_s = acc[:, :HEAD_DIM] / jnp.maximum(acc[:, HEAD_DIM:], 1e-30)
        g_s = gates[:, 1 * HEADS_PER_GROUP + h:1 * HEADS_PER_GROUP + h + 1]
        z = nz_ref[:, sl].astype(F32)
        y_ref[:, sl] = ((part_ref[:, sl] + g_s * o_s) * (z * _sigmoid(z))).astype(BF16)


def _select(proj, selb, block_hot, part, batch, seq):
    tq = min(SEL_TQ, seq)
    tk = min(SEL_TK, seq)
    nq = seq // tq
    n_slabs = selb.shape[2]
    q_cols = HEADS_PER_GROUP * HEAD_DIM
    kernel = functools.partial(_select_kernel, tk=tk)
    return pl.pallas_call(
        kernel,
        out_shape=jax.ShapeDtypeStruct((batch * seq, NSA_HEADS * HEAD_DIM), BF16),
        grid=(batch, NSA_KV_GROUPS, nq),
        in_specs=[
            pl.BlockSpec((tq, q_cols), lambda b, g, i: (b * nq + i, BLK_NQ * LANES // q_cols + g)),
            pl.BlockSpec((1, 1, n_slabs, tq, LANES), lambda b, g, i: (b, g, 0, i, 0)),
            pl.BlockSpec((seq, HEAD_DIM), lambda b, g, i: (b, BLK_KS + g), pipeline_mode=pl.Buffered(1)),
            pl.BlockSpec((seq, HEAD_DIM), lambda b, g, i: (b, BLK_VS + g), pipeline_mode=pl.Buffered(1)),
            pl.BlockSpec(block_hot.shape, lambda b, g, i: (0, 0), pipeline_mode=pl.Buffered(1)),
            pl.BlockSpec((tq, q_cols), lambda b, g, i: (b * nq + i, g)),
            pl.BlockSpec((tq, LANES), lambda b, g, i: (b * nq + i, BLK_NG + g)),
            pl.BlockSpec((tq, q_cols), lambda b, g, i: (b * nq + i, BLK_NZ * LANES // q_cols + g)),
        ],
        out_specs=pl.BlockSpec((tq, q_cols), lambda b, g, i: (b * nq + i, g)),
        scratch_shapes=[
            pltpu.VMEM((HEADS_PER_GROUP, tq, 2 * HEAD_DIM), F32),
            pltpu.VMEM((HEADS_PER_GROUP, tq, LANES), F32),
            pltpu.VMEM((HEADS_PER_GROUP, tq, tk), BF16),
            pltpu.VMEM((HEADS_PER_GROUP, tq, LANES), F32),
        ],
        compiler_params=pltpu.CompilerParams(
            dimension_semantics=("arbitrary", "arbitrary", "arbitrary"), vmem_limit_bytes=VMEM_LIMIT),
        name="select",
    )(proj, selb, proj, proj, block_hot, part, proj, proj)


def _outproj_kernel(yh_ref, yn_ref, w_ref, x_ref, g_ref, o_ref):
    half = yh_ref.shape[1]
    z = (jnp.dot(yh_ref[...], w_ref[:half, :], preferred_element_type=F32)
         + jnp.dot(yn_ref[...], w_ref[half:, :], preferred_element_type=F32))
    ms = jnp.mean(z * z, axis=-1, keepdims=True)
    o_ref[...] = x_ref[...] + z * lax.rsqrt(ms + NORM_EPS) * g_ref[...]


def _outproj(yh, yn, w, x2, g):
    n = x2.shape[0]
    tm = min(OUT_TM, n)
    return pl.pallas_call(
        _outproj_kernel,
        out_shape=jax.ShapeDtypeStruct((n, D_MODEL), F32),
        grid=(n // tm,),
        in_specs=[
            pl.BlockSpec((tm, yh.shape[1]), lambda i: (i, 0)),
            pl.BlockSpec((tm, yn.shape[1]), lambda i: (i, 0)),
            pl.BlockSpec(w.shape, lambda i: (0, 0)),
            pl.BlockSpec((tm, D_MODEL), lambda i: (i, 0)),
            pl.BlockSpec((1, D_MODEL), lambda i: (0, 0)),
        ],
        out_specs=pl.BlockSpec((tm, D_MODEL), lambda i: (i, 0)),
        compiler_params=pltpu.CompilerParams(
            dimension_semantics=("arbitrary",), vmem_limit_bytes=VMEM_LIMIT),
        name="outproj",
    )(yh, yn, w, x2, g)


def _permute_weights(w):
    def seg(name, width):
        return w[:, _SRC[name]:_SRC[name] + width]

    ng = seg("ng", NSA_HEADS * N_BRANCHES).reshape(-1, NSA_KV_GROUPS, HEADS_PER_GROUP, N_BRANCHES)
    ng = ng.transpose(0, 1, 3, 2).reshape(-1, NSA_KV_GROUPS, N_BRANCHES * HEADS_PER_GROUP)
    ng = jnp.pad(ng, ((0, 0), (0, 0), (0, LANES - N_BRANCHES * HEADS_PER_GROUP))).reshape(-1, NSA_KV_GROUPS * LANES)
    placed = [(BLK_HQ, seg("hq", 1024)), (BLK_HI, seg("hi", 1024)), (BLK_HZ, seg("hz", 1024)),
              (BLK_NQ, seg("nq", 1024) * (HEAD_DIM ** -0.5 * LOG2_E)),
              (BLK_KS, seg("ks", 256)), (BLK_KW, seg("kw", 256)), (BLK_KC, seg("kc", 256)),
              (BLK_VC, seg("vc", 256)), (BLK_VS, seg("vs", 256)), (BLK_VW, seg("vw", 256)),
              (BLK_NZ, seg("nz", 1024)), (BLK_NG, ng), (BLK_HF, seg("hf", 1024))]
    parts, col = [], 0
    for blk, cols in sorted(placed, key=lambda p: p[0]):
        if blk * LANES > col:
            parts.append(jnp.zeros((w.shape[0], blk * LANES - col), w.dtype))
        parts.append(cols)
        col = blk * LANES + cols.shape[1]
    if col < PROJ_COLS:
        parts.append(jnp.zeros((w.shape[0], PROJ_COLS - col), w.dtype))
    return jnp.concatenate(parts, axis=1).astype(BF16)


def _rope_tables(pos):
    half = ROT_DIM // 2
    inv = ROPE_THETA ** (-2.0 * jnp.arange(half, dtype=F32) / ROT_DIM)
    ang = pos.astype(F32)[:, None] * inv[None, :]
    cos, sin = jnp.cos(ang), jnp.sin(ang)
    n = pos.shape[0]
    pad = jnp.zeros((n, HEAD_DIM - ROT_DIM), F32)
    zero = jnp.zeros((n, half), F32)
    c = jnp.concatenate([cos, cos, pad + 1.0], axis=1)
    s1 = jnp.concatenate([-sin, zero, pad], axis=1)
    s2 = jnp.concatenate([zero, sin, pad], axis=1)
    return c, s1, s2


def _pair_levels():
    t = np.arange(HGRN_CHUNK)[:, None]
    s = np.arange(HGRN_CHUNK)[None, :]
    high_bit = 2 ** np.floor(np.log2(np.maximum(t ^ s, 1))).astype(np.int64)
    return jnp.asarray(np.where(t > s, high_bit, np.where(t == s, 0, -1)), jnp.int32)


def _block_one_hot(seq):
    rows = min(seq, SEL_BLOCK * LANES)
    lane = (np.arange(rows) // SEL_BLOCK) % LANES
    return jnp.asarray(lane[:, None] == np.arange(LANES)[None, :], BF16)


def _overlap_matrix(n_cmp, n_blocks, nblkp):
    start = np.arange(n_cmp)[:, None] * CMP_STRIDE
    sel = np.arange(nblkp)[None, :] * SEL_BLOCK
    ov = (start < sel + SEL_BLOCK) & (start + CMP_BLOCK > sel)
    ov &= np.arange(nblkp)[None, :] < n_blocks
    ov &= np.arange(n_cmp)[:, None] < (n_cmp - 1)
    return jnp.asarray(ov, BF16)


def kernel(x, pre_norm, post_norm, w_in, hgrn_lb_logits, hgrn_out_norm, cmp_pos_k, cmp_w1_k, cmp_b1_k,
           cmp_w2_k, cmp_pos_v, cmp_w1_v, cmp_b1_v, cmp_w2_v, w_out):
    batch, seq, _ = x.shape
    depth = w_in.shape[0]
    n_cmp = seq // CMP_STRIDE
    n_blocks = seq // SEL_BLOCK
    nblkp = -(-n_blocks // LANES) * LANES

    lb_probs = jax.nn.softmax(hgrn_lb_logits.astype(F32), axis=0)
    lower = jnp.maximum(jnp.cumsum(lb_probs, axis=0) - lb_probs[0:1], 0.0)
    lower = lower.reshape(depth, HGRN_HEADS, 1, HEAD_DIM)
    g_out = jnp.broadcast_to(hgrn_out_norm.astype(F32)[:, None, None, :], lower.shape)
    hgrn_par = jnp.concatenate(
        [jnp.log(lower), jnp.log1p(-lower), 1.0 - lower, g_out, jnp.zeros((depth, HGRN_HEADS, 4, HEAD_DIM), F32)],
        axis=2)

    rope_c, rope_s1, rope_s2 = _rope_tables(jnp.arange(seq))
    cmp_tab_k = jnp.stack(_rope_tables(jnp.arange(n_cmp) * CMP_STRIDE + CMP_BLOCK - 1))
    cmp_tab_v = jnp.stack([jnp.ones((n_cmp, HEAD_DIM), F32), jnp.zeros((n_cmp, HEAD_DIM), F32),
                           jnp.zeros((n_cmp, HEAD_DIM), F32)])
    cmp_tabs = jnp.stack([cmp_tab_k, cmp_tab_v])
    ov = _overlap_matrix(n_cmp, n_blocks, nblkp)
    tri = jnp.asarray(np.tril(np.ones((HGRN_CHUNK, HGRN_CHUNK), np.float32)))
    lvl = _pair_levels()
    block_hot = _block_one_hot(seq)

    x2 = x.reshape(batch * seq, D_MODEL)
    for layer in range(depth):
        proj, proj_f32 = _proj(x2, pre_norm[layer][None, :], _permute_weights(w_in[layer]),
                               rope_c, rope_s1, rope_s2, seq)
        y_h = _hgrn(proj, proj_f32, hgrn_par[layer], tri, lvl, batch, seq)
        cmp_kv = _compress(
            proj,
            jnp.stack([cmp_w1_k[layer], cmp_w1_v[layer]]).astype(BF16),
            jnp.stack([cmp_b1_k[layer], cmp_b1_v[layer]])[:, None, :],
            jnp.stack([cmp_w2_k[layer], cmp_w2_v[layer]]).astype(BF16),
            jnp.stack([cmp_pos_k[layer], cmp_pos_v[layer]]),
            cmp_tabs, batch, seq)
        part, selb = _cmpwin(proj, cmp_kv, ov, batch, seq)
        y_n = _select(proj, selb, block_hot, part, batch, seq)
        x2 = _outproj(y_h, y_n, w_out[layer].astype(BF16), x2, post_norm[layer][None, :])
    return x2.reshape(batch, seq, D_MODEL)
```
